```python
import math
import jax, jax.numpy as jnp
from jax import lax
import numpy as np

D_MODEL = 2048
BATCH = 2
SEQ = 16384
DEPTH = 1

HEAD_DIM = 128
A_HEADS = 8
A_KV_HEADS = 2
A_GROUP = A_HEADS // A_KV_HEADS
A_WIDTH = A_HEADS * HEAD_DIM
A_KV_WIDTH = A_KV_HEADS * HEAD_DIM
WINDOW = 128
A_BLOCK = 128
NUM_BUCKETS = 32
MAX_DISTANCE = 128
B_HEADS = 8
B_WIDTH = B_HEADS * HEAD_DIM
GRID_W = 64
NB_ROWS = 8
NB_COLS = 16
IN_COLS = A_WIDTH + 2 * A_KV_WIDTH + 3 * B_WIDTH + 2 * D_MODEL
N_GROUPS = 4
EXPERTS_PER_GROUP = 8
N_EXPERTS = N_GROUPS * EXPERTS_PER_GROUP
TOP_K = 2
D_EXPERT = 512
MOE_BLOCK = 128
EPS = 1e-6
NEG_INF = -1e30

kernel_name = "hybrid_window_gqa_natten_hier_moe"


def rms_norm(x, g):
    xf = x.astype(jnp.float32)
    y = xf * lax.rsqrt(jnp.mean(xf * xf, axis=-1, keepdims=True) + EPS)
    return (y * g.astype(jnp.float32)).astype(x.dtype)


def t5_bucket(rel):
    nb = NUM_BUCKETS // 2
    ret = jnp.where(rel > 0, nb, 0)
    n = jnp.abs(rel)
    max_exact = nb // 2
    nf = jnp.maximum(n, 1).astype(jnp.float32)
    large = max_exact + (jnp.log(nf / max_exact) / math.log(MAX_DISTANCE / max_exact)
                         * (nb - max_exact)).astype(jnp.int32)
    large = jnp.minimum(large, nb - 1)
    return ret + jnp.where(n < max_exact, n, large)


def window_gqa(q, k, v, g_q, g_k, rel_table, sink):
    B, S = q.shape[0], q.shape[1]
    nb = S // A_BLOCK
    q = rms_norm(q.reshape(B, S, A_HEADS, HEAD_DIM), g_q)
    k = rms_norm(k.reshape(B, S, A_KV_HEADS, HEAD_DIM), g_k)
    v = v.reshape(B, S, A_KV_HEADS, HEAD_DIM)
    qb = q.reshape(B, nb, A_BLOCK, A_KV_HEADS, A_GROUP, HEAD_DIM)

    def band(t):
        tp = jnp.pad(t, ((0, 0), (WINDOW, WINDOW), (0, 0), (0, 0)))
        tp = tp.reshape(B, nb + 2, A_BLOCK, A_KV_HEADS, HEAD_DIM)
        return jnp.concatenate([tp[:, :-2], tp[:, 1:-1], tp[:, 2:]], axis=2)

    kw, vw = band(k), band(v)
    s = jnp.einsum('bnqhgd,bnkhd->bnhgqk', qb, kw).astype(jnp.float32) * (HEAD_DIM ** -0.5)
    q_loc = jnp.arange(A_BLOCK)
    k_loc = jnp.arange(3 * A_BLOCK) - WINDOW
    rel = k_loc[None, :] - q_loc[:, None]
    bias = rel_table[t5_bucket(rel)].astype(jnp.float32)
    bias = jnp.transpose(bias, (2, 0, 1)).reshape(A_KV_HEADS, A_GROUP, A_BLOCK, 3 * A_BLOCK)
    k_pos = jnp.arange(nb)[:, None] * A_BLOCK + k_loc[None, :]
    valid = (jnp.abs(rel) <= WINDOW)[None] & ((k_pos >= 0) & (k_pos < S))[:, None, :]
    s = jnp.where(valid[None, :, None, None], s + bias, NEG_INF)
    sink_col = jnp.broadcast_to(sink.astype(jnp.float32).reshape(A_KV_HEADS, A_GROUP, 1, 1),
                                s.shape[:-1] + (1,))
    p = jax.nn.softmax(jnp.concatenate([s, sink_col], axis=-1), axis=-1)[..., :-1]
    o = jnp.einsum('bnhgqk,bnkhd->bnqhgd', p.astype(v.dtype), vw)
    return o.reshape(B, S, A_WIDTH)


def neighbourhood_attention(q, k, v, g_q, g_k, rpb):
    B, S = q.shape[0], q.shape[1]
    R = S // GRID_W
    kh = min(NB_ROWS, R)
    q = rms_norm(q.reshape(B, R, GRID_W, B_HEADS, HEAD_DIM), g_q)
    k = rms_norm(k.reshape(B, R, GRID_W, B_HEADS, HEAD_DIM), g_k)
    v = v.reshape(B, R, GRID_W, B_HEADS, HEAD_DIM)
    cols = jnp.arange(GRID_W)
    col_idx = jnp.clip(cols - NB_COLS // 2, 0, GRID_W - NB_COLS)[:, None] + jnp.arange(NB_COLS)[None, :]
    dc = col_idx - cols[:, None] + (NB_COLS - 1)
    scale = HEAD_DIM ** -0.5

    def row(r):
        rs = jnp.clip(r - kh // 2, 0, R - kh)
        q_r = lax.dynamic_index_in_dim(q, r, axis=1, keepdims=False)
        k_nb = lax.dynamic_slice_in_dim(k, rs, kh, axis=1)[:, :, col_idx]
        v_nb = lax.dynamic_slice_in_dim(v, rs, kh, axis=1)[:, :, col_idx]
        s = jnp.einsum('bqhd,brqjhd->bhqrj', q_r, k_nb).astype(jnp.float32) * scale
        dr = rs + jnp.arange(kh) - r + (NB_ROWS - 1)
        bias = rpb[:, dr[None, :, None], dc[:, None, :]].astype(jnp.float32)
        s = s + bias[None]
        p = jax.nn.softmax(s.reshape(B, B_HEADS, GRID_W, kh * NB_COLS), axis=-1).reshape(s.shape)
        return jnp.einsum('bhqrj,brqjhd->bqhd', p.astype(v.dtype), v_nb)

    o = lax.map(row, jnp.arange(R))
    return jnp.moveaxis(o, 0, 1).reshape(B, S, B_WIDTH)


def mixer_sublayer(x, norm_g, w_in, q_norm_a, k_norm_a, sink_a, q_norm_b, k_norm_b, rpb_b,
                   w_up_a, w_up_b, w_out, rel_table):
    xn = rms_norm(x, norm_g)
    proj = xn @ w_in
    cuts = np.cumsum([A_WIDTH, A_KV_WIDTH, A_KV_WIDTH, B_WIDTH, B_WIDTH, B_WIDTH, D_MODEL])
    qa, ka, va, qb, kb, vb, ga, gb = jnp.split(proj, [int(c) for c in cuts], axis=-1)
    oa = window_gqa(qa, ka, va, q_norm_a, k_norm_a, rel_table, sink_a)
    ob = neighbourhood_attention(qb, kb, vb, q_norm_b, k_norm_b, rpb_b)
    merged = jax.nn.sigmoid(ga) * (oa @ w_up_a) + jax.nn.sigmoid(gb) * (ob @ w_up_b)
    return x + merged @ w_out


def hier_moe(xn, w_rg, b_rg, w_re, b_re, w_gate, w_up, w_down):
    T, D = xn.shape
    g_logits = (xn @ w_rg).astype(jnp.float32) + b_rg.astype(jnp.float32)
    g_idx = jnp.argmax(g_logits, axis=-1)
    p_g = jnp.take_along_axis(jax.nn.softmax(g_logits, axis=-1), g_idx[:, None], axis=-1)
    e_all = ((xn @ w_re).astype(jnp.float32) + b_re.astype(jnp.float32)).reshape(T, N_GROUPS, EXPERTS_PER_GROUP)
    e_logits = jnp.take_along_axis(e_all, g_idx[:, None, None], axis=1)[:, 0]
    top_v, top_e = lax.top_k(e_logits, TOP_K)
    weight = p_g * jax.nn.softmax(top_v, axis=-1)
    expert_id = g_idx[:, None] * EXPERTS_PER_GROUP + top_e

    A = T * TOP_K
    flat_e = expert_id.reshape(A).astype(jnp.int32)
    flat_tok = jnp.repeat(jnp.arange(T, dtype=jnp.int32), TOP_K)
    flat_w = weight.reshape(A)
    order = jnp.argsort(flat_e)
    se, st, sw = flat_e[order], flat_tok[order], flat_w[order]
    counts = jnp.bincount(flat_e, length=N_EXPERTS).astype(jnp.int32)
    padded = ((counts + MOE_BLOCK - 1) // MOE_BLOCK) * MOE_BLOCK
    end_pad = jnp.cumsum(padded)
    start_pad = end_pad - padded
    start_orig = jnp.cumsum(counts) - counts
    dest = start_pad[se] + (jnp.arange(A, dtype=jnp.int32) - start_orig[se])
    P = A + N_EXPERTS * MOE_BLOCK
    NB = P // MOE_BLOCK
    buf_tok = jnp.full((P,), T, jnp.int32).at[dest].set(st)
    buf_w = jnp.zeros((P,), jnp.float32).at[dest].set(sw)
    blk_e = jnp.minimum(jnp.searchsorted(end_pad, jnp.arange(NB, dtype=jnp.int32) * MOE_BLOCK, side='right'),
                        N_EXPERTS - 1)
    x_pad = jnp.concatenate([xn, jnp.zeros((1, D), xn.dtype)], axis=0)

    def expert_block(args):
        e, tok, w = args
        rows = x_pad[tok]
        h = jax.nn.silu(rows @ w_gate[e]) * (rows @ w_up[e])
        return (h @ w_down[e]) * w[:, None].astype(rows.dtype)

    yb = lax.map(expert_block, (blk_e, buf_tok.reshape(NB, MOE_BLOCK), buf_w.reshape(NB, MOE_BLOCK)))
    y = jnp.zeros((T + 1, D), xn.dtype).at[buf_tok].add(yb.reshape(P, D))
    return y[:T]


def setup_inputs(seed: int = 0) -> dict:
    key = jax.random.key(seed)
    ks = jax.random.split(key, 21)
    f32 = jnp.float32
    L = DEPTH

    def nrm(k, shape, scale):
        return jax.random.normal(k, shape, f32) * scale

    return {
        "x": nrm(ks[0], (BATCH, SEQ, D_MODEL), 1.0),
        "norm1_g": 1.0 + nrm(ks[1], (L, D_MODEL), 0.02),
        "w_in": nrm(ks[2], (L, D_MODEL, IN_COLS), D_MODEL ** -0.5),
        "q_norm_a": 1.0 + nrm(ks[3], (L, HEAD_DIM), 0.02),
        "k_norm_a": 1.0 + nrm(ks[4], (L, HEAD_DIM), 0.02),
        "sink_a": nrm(ks[5], (L, A_HEADS), 0.5),
        "q_norm_b": 1.0 + nrm(ks[6], (L, HEAD_DIM), 0.02),
        "k_norm_b": 1.0 + nrm(ks[7], (L, HEAD_DIM), 0.02),
        "rpb_b": nrm(ks[8], (L, B_HEADS, 2 * NB_ROWS - 1, 2 * NB_COLS - 1), 0.1),
        "w_up_a": nrm(ks[9], (L, A_WIDTH, D_MODEL), A_WIDTH ** -0.5),
        "w_up_b": nrm(ks[10], (L, B_WIDTH, D_MODEL), B_WIDTH ** -0.5),
        "w_out": nrm(ks[11], (L, D_MODEL, D_MODEL), D_MODEL ** -0.5),
        "rel_bias_table": nrm(ks[12], (NUM_BUCKETS, A_HEADS), 0.1),
        "norm2_g": 1.0 + nrm(ks[13], (L, D_MODEL), 0.02),
        "w_router_group": nrm(ks[14], (L, D_MODEL, N_GROUPS), D_MODEL ** -0.5),
        "b_router_group": nrm(ks[15], (L, N_GROUPS), 0.01),
        "w_router_expert": nrm(ks[16], (L, D_MODEL, N_EXPERTS), D_MODEL ** -0.5),
        "b_router_expert": nrm(ks[17], (L, N_EXPERTS), 0.01),
        "w_gate": nrm(ks[18], (L, N_EXPERTS, D_MODEL, D_EXPERT), D_MODEL ** -0.5),
        "w_up": nrm(ks[19], (L, N_EXPERTS, D_MODEL, D_EXPERT), D_MODEL ** -0.5),
        "w_down": nrm(ks[20], (L, N_EXPERTS, D_EXPERT, D_MODEL), D_EXPERT ** -0.5),
    }


def reference(x, norm1_g, w_in, q_norm_a, k_norm_a, sink_a, q_norm_b, k_norm_b, rpb_b,
              w_up_a, w_up_b, w_out, rel_bias_table, norm2_g, w_router_group, b_router_group,
              w_router_expert, b_router_expert, w_gate, w_up, w_down):
    B, S, D = x.shape
    h = x
    for l in range(DEPTH):
        h = mixer_sublayer(h, norm1_g[l], w_in[l], q_norm_a[l], k_norm_a[l], sink_a[l],
                           q_norm_b[l], k_norm_b[l], rpb_b[l], w_up_a[l], w_up_b[l], w_out[l],
                           rel_bias_table)
        hn = rms_norm(h, norm2_g[l]).reshape(B * S, D)
        h = h + hier_moe(hn, w_router_group[l], b_router_group[l], w_router_expert[l],
                         b_router_expert[l], w_gate[l], w_up[l], w_down[l]).reshape(B, S, D)
    return h
```

```python
import functools
import math

import jax
import jax.numpy as jnp
from jax import lax
from jax.experimental import pallas as pl
from jax.experimental.pallas import tpu as pltpu

F32 = jnp.float32
BF16 = jnp.bfloat16

D_MODEL = 2048
HEAD_DIM = 128
A_HEADS = 8
A_KV_HEADS = 2
A_GROUP = A_HEADS // A_KV_HEADS
A_WIDTH = A_HEADS * HEAD_DIM
A_KV_WIDTH = A_KV_HEADS * HEAD_DIM
WINDOW = 128
A_BLOCK = 128
NUM_BUCKETS = 32
MAX_DISTANCE = 128
B_HEADS = 8
B_WIDTH = B_HEADS * HEAD_DIM
GRID_W = 64
NB_ROWS = 8
NB_COLS = 16
IN_COLS = A_WIDTH + 2 * A_KV_WIDTH + 3 * B_WIDTH + 2 * D_MODEL
N_GROUPS = 4
EXPERTS_PER_GROUP = 8
N_EXPERTS = N_GROUPS * EXPERTS_PER_GROUP
TOP_K = 2
D_EXPERT = 512
EPS = 1e-6
NEG_INF = -1e30

COL_QA = 0
COL_QB = COL_QA + A_WIDTH
COL_KB = COL_QB + B_WIDTH
COL_VB = COL_KB + B_WIDTH
COL_GA = COL_VB + B_WIDTH
COL_GB = COL_GA + D_MODEL
COL_KA = COL_GB + D_MODEL
COL_VA = COL_KA + A_KV_WIDTH

PROJ_TM = 1024
PROJ_TN = 512
NORM_ROWS = 128
NA_ROWS = 4
NA_BLOCK = NA_ROWS * GRID_W
MERGE_TM = 256
MOE_TM = 256
COMB_TM = 256
ROUTER_LANES = 128
VMEM_LIMIT = 56 * 1024 * 1024


def _head_norm(a, gain):
    outs = []
    for hd in range(a.shape[1] // HEAD_DIM):
        sl = slice(hd * HEAD_DIM, (hd + 1) * HEAD_DIM)
        ah = a[:, sl]
        ss = jnp.sum(ah * ah, axis=-1, keepdims=True)
        outs.append(ah * lax.rsqrt(ss * (1.0 / HEAD_DIM) + EPS) * gain[:, sl])
    return outs[0] if len(outs) == 1 else jnp.concatenate(outs, axis=1)


def _inproj_kernel(x_ref, g1_ref, w_ref, gain_ref, o_ref, xn_ref):
    j = pl.program_id(1)

    @pl.when(j == 0)
    def _():
        def norm_rows(c, carry):
            rows = pl.ds(pl.multiple_of(c * NORM_ROWS, NORM_ROWS), NORM_ROWS)
            x = x_ref[rows, :]
            ms = jnp.mean(x * x, axis=-1, keepdims=True)
            xn_ref[rows, :] = (x * lax.rsqrt(ms + EPS) * g1_ref[...]).astype(BF16)
            return carry

        lax.fori_loop(0, PROJ_TM // NORM_ROWS, norm_rows, 0)

    acc = jnp.dot(xn_ref[...], w_ref[...], preferred_element_type=F32)
    gain = gain_ref[...]
    n_norm = COL_VB // PROJ_TN
    n_plain = COL_GA // PROJ_TN
    n_gate = COL_KA // PROJ_TN

    @pl.when(j < n_norm)
    def _():
        o_ref[...] = _head_norm(acc, gain).astype(o_ref.dtype)

    @pl.when(jnp.logical_and(j >= n_norm, j < n_plain))
    def _():
        o_ref[...] = acc.astype(o_ref.dtype)

    @pl.when(jnp.logical_and(j >= n_plain, j < n_gate))
    def _():
        o_ref[...] = jax.nn.sigmoid(acc).astype(o_ref.dtype)

    @pl.when(j >= n_gate)
    def _():
        o_ref[:, :A_KV_WIDTH] = _head_norm(acc[:, :A_KV_WIDTH], gain[:, :A_KV_WIDTH]).astype(o_ref.dtype)
        o_ref[:, A_KV_WIDTH:] = acc[:, A_KV_WIDTH:].astype(o_ref.dtype)


def _in_projection(x2, g1, w_cat, gain_cat):
    t = x2.shape[0]
    return pl.pallas_call(
        _inproj_kernel,
        grid=(t // PROJ_TM, IN_COLS // PROJ_TN),
        in_specs=[
            pl.BlockSpec((PROJ_TM, D_MODEL), lambda i, j: (i, 0)),
            pl.BlockSpec((1, D_MODEL), lambda i, j: (0, 0)),
            pl.BlockSpec((D_MODEL, PROJ_TN), lambda i, j: (0, j)),
            pl.BlockSpec((1, PROJ_TN), lambda i, j: (0, j)),
        ],
        out_specs=pl.BlockSpec((PROJ_TM, PROJ_TN), lambda i, j: (i, j)),
        out_shape=jax.ShapeDtypeStruct((t, IN_COLS), BF16),
        scratch_shapes=[pltpu.VMEM((PROJ_TM, D_MODEL), BF16)],
        compiler_params=pltpu.CompilerParams(
            dimension_semantics=("parallel", "arbitrary"), vmem_limit_bytes=VMEM_LIMIT),
        name="in_projection",
    )(x2, g1, w_cat, gain_cat)


def _t5_bucket(rel):
    nb = NUM_BUCKETS // 2
    ret = jnp.where(rel > 0, nb, 0)
    n = jnp.abs(rel)
    max_exact = nb // 2
    nf = jnp.maximum(n, 1).astype(jnp.float32)
    large = max_exact + (jnp.log(nf / max_exact) / math.log(MAX_DISTANCE / max_exact)
                         * (nb - max_exact)).astype(jnp.int32)
    large = jnp.minimum(large, nb - 1)
    return ret + jnp.where(n < max_exact, n, large)


def _window_bias(rel_table):
    q_loc = jnp.arange(A_BLOCK)
    k_loc = jnp.arange(3 * A_BLOCK) - WINDOW
    rel = k_loc[None, :] - q_loc[:, None]
    bias = rel_table[_t5_bucket(rel)].astype(F32)
    bias = jnp.transpose(bias, (2, 0, 1))
    in_window = jnp.abs(rel) <= WINDOW
    has_prev = jnp.array([False, True, True])[:, None, None]
    has_next = jnp.array([True, True, False])[:, None, None]
    k_blk = (jnp.arange(3 * A_BLOCK) // A_BLOCK)[None, None, :]
    valid = in_window[None] & ((k_blk != 0) | has_prev) & ((k_blk != 2) | has_next)
    full = jnp.where(valid[:, None], bias[None], NEG_INF)
    return full.reshape(3, A_KV_HEADS, A_GROUP * A_BLOCK, 3 * A_BLOCK)


def _window_kernel(sink_ref, q_ref, kp_ref, ko_ref, kn_ref, vp_ref, vo_ref, vn_ref, bias_ref, o_ref):
    for kvh in range(A_KV_HEADS):
        hs = slice(kvh * HEAD_DIM, (kvh + 1) * HEAD_DIM)
        heads = [kvh * A_GROUP + g for g in range(A_GROUP)]
        q4 = jnp.concatenate([q_ref[:, h * HEAD_DIM:(h + 1) * HEAD_DIM] for h in heads], axis=0)
        kc = jnp.concatenate([kp_ref[:, hs], ko_ref[:, hs], kn_ref[:, hs]], axis=0)
        vc = jnp.concatenate([vp_ref[:, hs], vo_ref[:, hs], vn_ref[:, hs]], axis=0)
        s = lax.dot_general(q4, kc, (((1,), (1,)), ((), ())), preferred_element_type=F32)
        s = s + bias_ref[0, kvh]
        m = jnp.max(s, axis=-1, keepdims=True)
        p = jnp.exp(s - m)
        sink = jnp.concatenate([jnp.full((A_BLOCK, 1), sink_ref[h], F32) for h in heads], axis=0)
        denom = jnp.sum(p, axis=-1, keepdims=True) + jnp.exp(sink - m)
        o = jnp.dot(p.astype(BF16), vc, preferred_element_type=F32) / denom
        for g, h in enumerate(heads):
            o_ref[:, h * HEAD_DIM:(h + 1) * HEAD_DIM] = o[g * A_BLOCK:(g + 1) * A_BLOCK].astype(o_ref.dtype)


def _window_attention(proj, bias_a, sink, batch, seq):
    nb = seq // A_BLOCK

    def cls(n):
        return jnp.where(n == 0, 0, jnp.where(n == nb - 1, 2, 1))

    def row(b, n):
        return b * nb + n

    kcol, vcol = COL_KA // A_KV_WIDTH, COL_VA // A_KV_WIDTH
    prev = lambda b, n: row(b, jnp.maximum(n - 1, 0))
    nxt = lambda b, n: row(b, jnp.minimum(n + 1, nb - 1))
    kv_spec = lambda rfn, col: pl.BlockSpec((A_BLOCK, A_KV_WIDTH), lambda b, n: (rfn(b, n), col))
    return pl.pallas_call(
        _window_kernel,
        grid=(batch, nb),
        in_specs=[
            pl.BlockSpec(memory_space=pltpu.SMEM),
            pl.BlockSpec((A_BLOCK, A_WIDTH), lambda b, n: (row(b, n), COL_QA // A_WIDTH)),
            kv_spec(prev, kcol), kv_spec(row, kcol), kv_spec(nxt, kcol),
            kv_spec(prev, vcol), kv_spec(row, vcol), kv_spec(nxt, vcol),
            pl.BlockSpec((1, A_KV_HEADS, A_GROUP * A_BLOCK, 3 * A_BLOCK), lambda b, n: (cls(n), 0, 0, 0)),
        ],
        out_specs=pl.BlockSpec((A_BLOCK, A_WIDTH), lambda b, n: (row(b, n), 0)),
        out_shape=jax.ShapeDtypeStruct((batch * seq, A_WIDTH), BF16),
        compiler_params=pltpu.CompilerParams(
            dimension_semantics=("parallel", "arbitrary"), vmem_limit_bytes=VMEM_LIMIT),
        name="window_attention",
    )(sink, proj, proj, proj, proj, proj, proj, proj, bias_a)


def _neighbourhood_bias(rpb, rows):
    kh = min(NB_ROWS, rows)
    nblk = rows // NA_ROWS
    r0 = jnp.array([0, NA_ROWS, (nblk - 1) * NA_ROWS])[:, None]
    r = r0 + jnp.arange(NA_ROWS)[None, :]
    rs = jnp.clip(r - kh // 2, 0, rows - kh)
    key_r = (r0 - NA_ROWS) + jnp.arange(3 * NA_ROWS)[None, :]
    row_ok = (key_r[:, None, :] >= rs[:, :, None]) & (key_r[:, None, :] < rs[:, :, None] + kh)
    dr = jnp.clip(key_r[:, None, :] - r[:, :, None] + (NB_ROWS - 1), 0, 2 * NB_ROWS - 2)
    cols = jnp.arange(GRID_W)
    cs = jnp.clip(cols - NB_COLS // 2, 0, GRID_W - NB_COLS)
    col_ok = (cols[None, :] >= cs[:, None]) & (cols[None, :] < cs[:, None] + NB_COLS)
    dc = jnp.clip(cols[None, :] - cols[:, None] + (NB_COLS - 1), 0, 2 * NB_COLS - 2)
    bias = rpb.astype(F32)[:, dr[:, :, None, :, None], dc[None, None, :, None, :]]
    ok = row_ok[:, :, None, :, None] & col_ok[None, None, :, None, :]
    full = jnp.where(ok[None], bias, NEG_INF)
    full = jnp.transpose(full, (1, 0, 2, 3, 4, 5))
    return full.reshape(3, B_HEADS, NA_BLOCK, 3 * NA_BLOCK)


def _neighbourhood_kernel(q_ref, kp_ref, ko_ref, kn_ref, vp_ref, vo_ref, vn_ref, bias_ref, o_ref):
    for h in range(B_HEADS):
        hs = slice(h * HEAD_DIM, (h + 1) * HEAD_DIM)
        kc = jnp.concatenate([kp_ref[:, hs], ko_ref[:, hs], kn_ref[:, hs]], axis=0)
        vc = jnp.concatenate([vp_ref[:, hs], vo_ref[:, hs], vn_ref[:, hs]], axis=0)
        s = lax.dot_general(q_ref[:, hs], kc, (((1,), (1,)), ((), ())), preferred_element_type=F32)
        s = s + bias_ref[0, h]
        m = jnp.max(s, axis=-1, keepdims=True)
        p = jnp.exp(s - m)
        denom = jnp.sum(p, axis=-1, keepdims=True)
        o = jnp.dot(p.astype(BF16), vc, preferred_element_type=F32) / denom
        o_ref[:, hs] = o.astype(o_ref.dtype)


def _neighbourhood_attention(proj, bias_b, batch, seq):
    nb = seq // NA_BLOCK

    def cls(n):
        return jnp.where(n == 0, 0, jnp.where(n == nb - 1, 2, 1))

    def row(b, n):
        return b * nb + n

    prev = lambda b, n: row(b, jnp.maximum(n - 1, 0))
    nxt = lambda b, n: row(b, jnp.minimum(n + 1, nb - 1))
    spec = lambda rfn, col: pl.BlockSpec((NA_BLOCK, B_WIDTH), lambda b, n: (rfn(b, n), col))
    qcol, kcol, vcol = COL_QB // B_WIDTH, COL_KB // B_WIDTH, COL_VB // B_WIDTH
    return pl.pallas_call(
        _neighbourhood_kernel,
        grid=(batch, nb),
        in_specs=[
            spec(row, qcol),
            spec(prev, kcol), spec(row, kcol), spec(nxt, kcol),
            spec(prev, vcol), spec(row, vcol), spec(nxt, vcol),
            pl.BlockSpec((1, B_HEADS, NA_BLOCK, 3 * NA_BLOCK), lambda b, n: (cls(n), 0, 0, 0)),
        ],
        out_specs=pl.BlockSpec((NA_BLOCK, B_WIDTH), lambda b, n: (row(b, n), 0)),
        out_shape=jax.ShapeDtypeStruct((batch * seq, B_WIDTH), BF16),
        compiler_params=pltpu.CompilerParams(
            dimension_semantics=("parallel", "arbitrary"), vmem_limit_bytes=VMEM_LIMIT),
        name="neighbourhood_attention",
    )(proj, proj, proj, proj, proj, proj, proj, bias_b)


def _merge_kernel(oa_ref, ob_ref, ga_ref, gb_ref, x_ref, wa_ref, wb_ref, wo_ref, g2_ref, wr_ref, br_ref,
                  h_ref, hn_ref, ids_ref, wcol_ref):
    ua = jnp.dot(oa_ref[...], wa_ref[...], preferred_element_type=F32)
    ub = jnp.dot(ob_ref[...], wb_ref[...], preferred_element_type=F32)
    merged = ga_ref[...].astype(F32) * ua + gb_ref[...].astype(F32) * ub
    h = x_ref[...] + jnp.dot(merged.astype(BF16), wo_ref[...], preferred_element_type=F32)
    h_ref[...] = h
    ms = jnp.mean(h * h, axis=-1, keepdims=True)
    hn = h * lax.rsqrt(ms + EPS) * g2_ref[...]
    hn_ref[...] = hn

    hn_hi = hn.astype(BF16)
    hn_lo = (hn - hn_hi.astype(F32)).astype(BF16)
    lg = jnp.dot(jnp.concatenate([hn_hi, hn_lo], axis=0), wr_ref[...], preferred_element_type=F32)
    tm = hn.shape[0]
    lg_hi, lg_lo = lg[:tm], lg[tm:]
    logits = lg_hi + pltpu.roll(lg_hi, ROUTER_LANES // 2, axis=1) + lg_lo + br_ref[...]
    lt = logits.T

    gl = lt[N_EXPERTS:N_EXPERTS + N_GROUPS]
    gmax = jnp.max(gl, axis=0, keepdims=True)
    giota = lax.broadcasted_iota(jnp.int32, gl.shape, 0)
    g_idx = jnp.min(jnp.where(gl == gmax, giota, N_GROUPS), axis=0, keepdims=True)
    p_g = 1.0 / jnp.sum(jnp.exp(gl - gmax), axis=0, keepdims=True)
    e_sel = jnp.zeros((EXPERTS_PER_GROUP, tm), F32)
    for g in range(N_GROUPS):
        e_sel = jnp.where(g_idx == g, lt[g * EXPERTS_PER_GROUP:(g + 1) * EXPERTS_PER_GROUP], e_sel)
    eiota = lax.broadcasted_iota(jnp.int32, e_sel.shape, 0)
    m1 = jnp.max(e_sel, axis=0, keepdims=True)
    i1 = jnp.min(jnp.where(e_sel == m1, eiota, EXPERTS_PER_GROUP), axis=0, keepdims=True)
    rest = jnp.where(eiota == i1, -jnp.inf, e_sel)
    m2 = jnp.max(rest, axis=0, keepdims=True)
    i2 = jnp.min(jnp.where(rest == m2, eiota, EXPERTS_PER_GROUP), axis=0, keepdims=True)
    t = jnp.exp(m2 - m1)
    w1 = p_g / (1.0 + t)
    w2 = p_g * t / (1.0 + t)
    ids_ref[...] = jnp.concatenate([g_idx * EXPERTS_PER_GROUP + i1, g_idx * EXPERTS_PER_GROUP + i2], axis=0)
    riota = lax.broadcasted_iota(jnp.int32, (ROUTER_LANES, tm), 0)
    wrows = jnp.where(riota == 0, w1, jnp.where(riota == 1, w2, 0.0))
    wcol_ref[...] = wrows.T


def _merge_and_route(oa, ob, proj, x2, wa, wb, wo, g2, wr_cat, br):
    t = x2.shape[0]
    const = lambda shape: pl.BlockSpec(shape, lambda i: (0, 0), pipeline_mode=pl.Buffered(1))
    return pl.pallas_call(
        _merge_kernel,
        grid=(t // MERGE_TM,),
        in_specs=[
            pl.BlockSpec((MERGE_TM, A_WIDTH), lambda i: (i, 0)),
            pl.BlockSpec((MERGE_TM, B_WIDTH), lambda i: (i, 0)),
            pl.BlockSpec((MERGE_TM, D_MODEL), lambda i: (i, COL_GA // D_MODEL)),
            pl.BlockSpec((MERGE_TM, D_MODEL), lambda i: (i, COL_GB // D_MODEL)),
            pl.BlockSpec((MERGE_TM, D_MODEL), lambda i: (i, 0)),
            const((A_WIDTH, D_MODEL)), const((B_WIDTH, D_MODEL)), const((D_MODEL, D_MODEL)),
            const((1, D_MODEL)), const((D_MODEL, ROUTER_LANES)), const((1, ROUTER_LANES)),
        ],
        out_specs=[
            pl.BlockSpec((MERGE_TM, D_MODEL), lambda i: (i, 0)),
            pl.BlockSpec((MERGE_TM, D_MODEL), lambda i: (i, 0)),
            pl.BlockSpec((TOP_K, MERGE_TM), lambda i: (0, i)),
            pl.BlockSpec((MERGE_TM, ROUTER_LANES), lambda i: (i, 0)),
        ],
        out_shape=[
            jax.ShapeDtypeStruct((t, D_MODEL), F32),
            jax.ShapeDtypeStruct((t, D_MODEL), F32),
            jax.ShapeDtypeStruct((TOP_K, t), jnp.int32),
            jax.ShapeDtypeStruct((t, ROUTER_LANES), F32),
        ],
        compiler_params=pltpu.CompilerParams(
            dimension_semantics=("parallel",), vmem_limit_bytes=VMEM_LIMIT),
        name="merge_and_route",
    )(oa, ob, proj, proj, x2, wa, wb, wo, g2, wr_cat, br)


def _expert_kernel(tok_ref, texp_ref, nused_ref, hn_hbm, wg_ref, wu_ref, wd_ref, y_ref, xbuf, sem):
    i = pl.program_id(0)
    n_used = nused_ref[0]
    slot = lax.rem(i, 2)

    def row_copy(tok, dst_slot, r):
        return pltpu.make_async_copy(hn_hbm.at[pl.ds(tok, 1), :], xbuf.at[dst_slot, pl.ds(r, 1), :],
                                     sem.at[dst_slot])

    def issue(tile, dst_slot):
        base = tile * MOE_TM

        def body(r, carry):
            row_copy(tok_ref[base + r], dst_slot, r).start()
            return carry

        lax.fori_loop(0, MOE_TM, body, 0)

    @pl.when(jnp.logical_and(i == 0, n_used > 0))
    def _():
        issue(0, 0)

    @pl.when(i + 1 < n_used)
    def _():
        issue(i + 1, 1 - slot)

    @pl.when(i < n_used)
    def _():
        pltpu.make_async_copy(hn_hbm.at[pl.ds(0, MOE_TM), :], xbuf.at[slot], sem.at[slot]).wait()
        xb = xbuf[slot].astype(BF16)
        g = jnp.dot(xb, wg_ref[0], preferred_element_type=F32)
        u = jnp.dot(xb, wu_ref[0], preferred_element_type=F32)
        a = (g * jax.nn.sigmoid(g) * u).astype(BF16)
        y_ref[...] = jnp.dot(a, wd_ref[0], preferred_element_type=F32)

    @pl.when(i >= n_used)
    def _():
        y_ref[...] = jnp.zeros_like(y_ref)


def _expert_ffn(tok_sorted, tile_expert, n_used, hn, wg, wu, wd, n_tiles):
    grid_spec = pltpu.PrefetchScalarGridSpec(
        num_scalar_prefetch=3,
        grid=(n_tiles,),
        in_specs=[
            pl.BlockSpec(memory_space=pl.ANY),
            pl.BlockSpec((1, D_MODEL, D_EXPERT), lambda i, tok, te, nu: (te[i], 0, 0)),
            pl.BlockSpec((1, D_MODEL, D_EXPERT), lambda i, tok, te, nu: (te[i], 0, 0)),
            pl.BlockSpec((1, D_EXPERT, D_MODEL), lambda i, tok, te, nu: (te[i], 0, 0)),
        ],
        out_specs=pl.BlockSpec((MOE_TM, D_MODEL), lambda i, tok, te, nu: (i, 0)),
        scratch_shapes=[pltpu.VMEM((2, MOE_TM, D_MODEL), F32), pltpu.SemaphoreType.DMA((2,))],
    )
    return pl.pallas_call(
        _expert_kernel,
        grid_spec=grid_spec,
        out_shape=jax.ShapeDtypeStruct((n_tiles * MOE_TM, D_MODEL), F32),
        compiler_params=pltpu.CompilerParams(
            dimension_semantics=("arbitrary",), vmem_limit_bytes=VMEM_LIMIT),
        name="expert_ffn",
    )(tok_sorted, tile_expert, n_used, hn, wg, wu, wd)


def _combine_kernel(pos_ref, h_ref, wcol_ref, y_hbm, o_ref, ybuf, sem):
    i = pl.program_id(0)
    n = pl.num_programs(0)
    slot = lax.rem(i, 2)
    n_tok = n * COMB_TM

    def row_copy(src_row, dst_slot, k, r):
        return pltpu.make_async_copy(y_hbm.at[pl.ds(src_row, 1), :], ybuf.at[dst_slot, k, pl.ds(r, 1), :],
                                     sem.at[dst_slot])

    def issue(tile, dst_slot):
        base = tile * COMB_TM

        def body(r, carry):
            for k in range(TOP_K):
                row_copy(pos_ref[k * n_tok + base + r], dst_slot, k, r).start()
            return carry

        lax.fori_loop(0, COMB_TM, body, 0)

    @pl.when(i == 0)
    def _():
        issue(0, 0)

    @pl.when(i + 1 < n)
    def _():
        issue(i + 1, 1 - slot)

    for k in range(TOP_K):
        pltpu.make_async_copy(y_hbm.at[pl.ds(0, COMB_TM), :], ybuf.at[slot, k], sem.at[slot]).wait()
    w = wcol_ref[...]
    o_ref[...] = h_ref[...] + w[:, 0:1] * ybuf[slot, 0] + w[:, 1:2] * ybuf[slot, 1]


def _combine(pos, h, wcol, y):
    t = h.shape[0]
    grid_spec = pltpu.PrefetchScalarGridSpec(
        num_scalar_prefetch=1,
        grid=(t // COMB_TM,),
        in_specs=[
            pl.BlockSpec((COMB_TM, D_MODEL), lambda i, pos: (i, 0)),
            pl.BlockSpec((COMB_TM, ROUTER_LANES), lambda i, pos: (i, 0)),
            pl.BlockSpec(memory_space=pl.ANY),
        ],
        out_specs=pl.BlockSpec((COMB_TM, D_MODEL), lambda i, pos: (i, 0)),
        scratch_shapes=[pltpu.VMEM((2, TOP_K, COMB_TM, D_MODEL), F32), pltpu.SemaphoreType.DMA((2,))],
    )
    return pl.pallas_call(
        _combine_kernel,
        grid_spec=grid_spec,
        out_shape=jax.ShapeDtypeStruct((t, D_MODEL), F32),
        compiler_params=pltpu.CompilerParams(
            dimension_semantics=("arbitrary",), vmem_limit_bytes=VMEM_LIMIT),
        name="combine",
    )(pos, h, wcol, y)


def _dispatch_plan(ids, n_tok):
    n_assign = TOP_K * n_tok
    n_tiles = n_assign // MOE_TM + N_EXPERTS
    flat_e = ids.reshape(n_assign)
    order = jnp.argsort(flat_e, stable=True).astype(jnp.int32)
    se = flat_e[order]
    counts = jnp.bincount(flat_e, length=N_EXPERTS).astype(jnp.int32)
    padded = ((counts + MOE_TM - 1) // MOE_TM) * MOE_TM
    end_pad = jnp.cumsum(padded)
    start_pad = end_pad - padded
    start_orig = jnp.cumsum(counts) - counts
    dest = start_pad[se] + (jnp.arange(n_assign, dtype=jnp.int32) - start_orig[se])
    tok_sorted = jnp.zeros((n_tiles * MOE_TM,), jnp.int32).at[dest].set(order % n_tok)
    pos = jnp.zeros((n_assign,), jnp.int32).at[order].set(dest)
    tile_start = jnp.arange(n_tiles, dtype=jnp.int32) * MOE_TM
    tile_expert = jnp.minimum(jnp.searchsorted(end_pad, tile_start, side='right'), N_EXPERTS - 1).astype(jnp.int32)
    n_used = (end_pad[-1] // MOE_TM).astype(jnp.int32).reshape(1)
    return tok_sorted, pos, tile_expert, n_used, n_tiles


def _layer(x, norm1_g, w_in, q_norm_a, k_norm_a, sink_a, q_norm_b, k_norm_b, rpb_b, w_up_a, w_up_b, w_out,
           rel_table, norm2_g, w_rg, b_rg, w_re, b_re, w_gate, w_up, w_down):
    batch, seq, d = x.shape
    t = batch * seq
    x2 = x.reshape(t, d)

    cuts = [0, A_WIDTH, A_WIDTH + A_KV_WIDTH, A_WIDTH + 2 * A_KV_WIDTH]
    cuts += [cuts[-1] + B_WIDTH, cuts[-1] + 2 * B_WIDTH, cuts[-1] + 3 * B_WIDTH]
    cuts += [cuts[-1] + D_MODEL, cuts[-1] + 2 * D_MODEL]
    qa, ka, va, qb, kb, vb, ga, gb = [w_in[:, cuts[n]:cuts[n + 1]] for n in range(8)]
    w_cat = jnp.concatenate([qa, qb, kb, vb, ga, gb, ka, va], axis=1).astype(BF16)
    scale = HEAD_DIM ** -0.5
    ones = lambda n: jnp.ones((n,), F32)
    gain_cat = jnp.concatenate([
        jnp.tile(q_norm_a.astype(F32) * scale, A_HEADS), jnp.tile(q_norm_b.astype(F32) * scale, B_HEADS),
        jnp.tile(k_norm_b.astype(F32), B_HEADS), ones(B_WIDTH + 2 * D_MODEL),
        jnp.tile(k_norm_a.astype(F32), A_KV_HEADS), ones(A_KV_WIDTH)]).reshape(1, IN_COLS)

    proj = _in_projection(x2, norm1_g.astype(F32).reshape(1, d), w_cat, gain_cat)
    oa = _window_attention(proj, _window_bias(rel_table), sink_a.astype(F32), batch, seq)
    ob = _neighbourhood_attention(proj, _neighbourhood_bias(rpb_b, seq // GRID_W), batch, seq)

    wr = jnp.concatenate([w_re.astype(F32), w_rg.astype(F32)], axis=1)
    wr_hi = wr.astype(BF16)
    wr_lo = (wr - wr_hi.astype(F32)).astype(BF16)
    half = ROUTER_LANES // 2
    padc = lambda a: jnp.pad(a, ((0, 0), (0, half - a.shape[1])))
    wr_cat = jnp.concatenate([padc(wr_hi), padc(wr_lo)], axis=1)
    br = jnp.pad(jnp.concatenate([b_re.astype(F32), b_rg.astype(F32)]), (0, ROUTER_LANES - N_EXPERTS - N_GROUPS))

    h, hn, ids, wcol = _merge_and_route(
        oa, ob, proj, x2, w_up_a.astype(BF16), w_up_b.astype(BF16), w_out.astype(BF16),
        norm2_g.astype(F32).reshape(1, d), wr_cat, br.reshape(1, ROUTER_LANES))

    tok_sorted, pos, tile_expert, n_used, n_tiles = _dispatch_plan(ids, t)
    y = _expert_ffn(tok_sorted, tile_expert, n_used, hn,
                    w_gate.astype(BF16), w_up.astype(BF16), w_down.astype(BF16), n_tiles)
    out = _combine(pos, h, wcol, y)
    return out.reshape(batch, seq, d)


def kernel(x, norm1_g, w_in, q_norm_a, k_norm_a, sink_a, q_norm_b, k_norm_b, rpb_b, w_up_a, w_up_b, w_out,
           rel_bias_table, norm2_g, w_router_group, b_router_group, w_router_expert, b_router_expert,
           w_gate, w_up, w_down):
    return _layer(x, norm1_g[0], w_in[0], q_norm_a[0], k_norm_a[0], sink_a[0], q_norm_b[0], k_norm_b[0],
                  rpb_b[0], w_up_a[0], w_up_b[0], w_out[0], rel_bias_table, norm2_g[0],
                  w_router_group[0], b_router_group[0], w_router_expert[0], b_router_expert[0],
                  w_gate[0], w_up[0], w_down[0])
```

```python
import functools
import math

import jax
import jax.numpy as jnp
from jax import lax
from jax.experimental import pallas as pl
from jax.experimental.pallas import tpu as pltpu

F32 = jnp.float32
BF16 = jnp.bfloat16

D_MODEL = 2048
HEAD_DIM = 128
A_HEADS = 8
A_KV_HEADS = 2
A_GROUP = A_HEADS // A_KV_HEADS
A_WIDTH = A_HEADS * HEAD_DIM
A_KV_WIDTH = A_KV_HEADS * HEAD_DIM
WINDOW = 128
A_BLOCK = 128
NUM_BUCKETS = 32
MAX_DISTANCE = 128
B_HEADS = 8
B_WIDTH = B_HEADS * HEAD_DIM
GRID_W = 64
NB_ROWS = 8
NB_COLS = 16
IN_COLS = A_WIDTH + 2 * A_KV_WIDTH + 3 * B_WIDTH + 2 * D_MODEL
N_GROUPS = 4
EXPERTS_PER_GROUP = 8
N_EXPERTS = N_GROUPS * EXPERTS_PER_GROUP
TOP_K = 2
D_EXPERT = 512
EPS = 1e-6
NEG_INF = -1e30

COL_QA = 0
COL_QB = COL_QA + A_WIDTH
COL_KB = COL_QB + B_WIDTH
COL_VB = COL_KB + B_WIDTH
COL_GA = COL_VB + B_WIDTH
COL_GB = COL_GA + D_MODEL
COL_KA = COL_GB + D_MODEL
COL_VA = COL_KA + A_KV_WIDTH

PROJ_TM = 1024
PROJ_TN = 512
NORM_ROWS = 128
NA_ROWS = 4
NA_BLOCK = NA_ROWS * GRID_W
MERGE_TM = 256
MOE_TM = 256
COMB_TM = 256
ROUTER_LANES = 128
VMEM_LIMIT = 56 * 1024 * 1024


def _head_norm(a, gain):
    outs = []
    for hd in range(a.shape[1] // HEAD_DIM):
        sl = slice(hd * HEAD_DIM, (hd + 1) * HEAD_DIM)
        ah = a[:, sl]
        ss = jnp.sum(ah * ah, axis=-1, keepdims=True)
        outs.append(ah * lax.rsqrt(ss * (1.0 / HEAD_DIM) + EPS) * gain[:, sl])
    return outs[0] if len(outs) == 1 else jnp.concatenate(outs, axis=1)


def _inproj_kernel(x_ref, g1_ref, w_ref, gain_ref, o_ref, xn_ref):
    j = pl.program_id(1)

    @pl.when(j == 0)
    def _():
        def norm_rows(c, carry):
            rows = pl.ds(pl.multiple_of(c * NORM_ROWS, NORM_ROWS), NORM_ROWS)
            x = x_ref[rows, :]
            ms = jnp.mean(x * x, axis=-1, keepdims=True)
            xn_ref[rows, :] = (x * lax.rsqrt(ms + EPS) * g1_ref[...]).astype(BF16)
            return carry

        lax.fori_loop(0, PROJ_TM // NORM_ROWS, norm_rows, 0)

    acc = jnp.dot(xn_ref[...], w_ref[...], preferred_element_type=F32)
    gain = gain_ref[...]
    n_norm = COL_VB // PROJ_TN
    n_plain = COL_GA // PROJ_TN
    n_gate = COL_KA // PROJ_TN

    @pl.when(j < n_norm)
    def _():
        o_ref[...] = _head_norm(acc, gain).astype(o_ref.dtype)

    @pl.when(jnp.logical_and(j >= n_norm, j < n_plain))
    def _():
        o_ref[...] = acc.astype(o_ref.dtype)

    @pl.when(jnp.logical_and(j >= n_plain, j < n_gate))
    def _():
        o_ref[...] = jax.nn.sigmoid(acc).astype(o_ref.dtype)

    @pl.when(j >= n_gate)
    def _():
        o_ref[:, :A_KV_WIDTH] = _head_norm(acc[:, :A_KV_WIDTH], gain[:, :A_KV_WIDTH]).astype(o_ref.dtype)
        o_ref[:, A_KV_WIDTH:] = acc[:, A_KV_WIDTH:].astype(o_ref.dtype)


def _in_projection(x2, g1, w_cat, gain_cat):
    t = x2.shape[0]
    return pl.pallas_call(
        _inproj_kernel,
        grid=(t // PROJ_TM, IN_COLS // PROJ_TN),
        in_specs=[
            pl.BlockSpec((PROJ_TM, D_MODEL), lambda i, j: (i, 0)),
            pl.BlockSpec((1, D_MODEL), lambda i, j: (0, 0)),
            pl.BlockSpec((D_MODEL, PROJ_TN), lambda i, j: (0, j)),
            pl.BlockSpec((1, PROJ_TN), lambda i, j: (0, j)),
        ],
        out_specs=pl.BlockSpec((PROJ_TM, PROJ_TN), lambda i, j: (i, j)),
        out_shape=jax.ShapeDtypeStruct((t, IN_COLS), BF16),
        scratch_shapes=[pltpu.VMEM((PROJ_TM, D_MODEL), BF16)],
        compiler_params=pltpu.CompilerParams(
            dimension_semantics=("parallel", "arbitrary"), vmem_limit_bytes=VMEM_LIMIT),
        name="in_projection",
    )(x2, g1, w_cat, gain_cat)


def _t5_bucket(rel):
    nb = NUM_BUCKETS // 2
    ret = jnp.where(rel > 0, nb, 0)
    n = jnp.abs(rel)
    max_exact = nb // 2
    nf = jnp.maximum(n, 1).astype(jnp.float32)
    large = max_exact + (jnp.log(nf / max_exact) / math.log(MAX_DISTANCE / max_exact)
                         * (nb - max_exact)).astype(jnp.int32)
    large = jnp.minimum(large, nb - 1)
    return ret + jnp.where(n < max_exact, n, large)


def _window_bias(rel_table):
    q_loc = jnp.arange(A_BLOCK)
    k_loc = jnp.arange(3 * A_BLOCK) - WINDOW
    rel = k_loc[None, :] - q_loc[:, None]
    onehot = (_t5_bucket(rel)[..., None] == jnp.arange(NUM_BUCKETS)).astype(F32)
    bias = jnp.einsum('qkn,nh->hqk', onehot, rel_table.astype(F32), precision=lax.Precision.HIGHEST)
    in_window = jnp.abs(rel) <= WINDOW
    has_prev = jnp.array([False, True, True])[:, None, None]
    has_next = jnp.array([True, True, False])[:, None, None]
    k_blk = (jnp.arange(3 * A_BLOCK) // A_BLOCK)[None, None, :]
    valid = in_window[None] & ((k_blk != 0) | has_prev) & ((k_blk != 2) | has_next)
    full = jnp.where(valid[:, None], bias[None], NEG_INF)
    return full.reshape(3, A_KV_HEADS, A_GROUP * A_BLOCK, 3 * A_BLOCK)


def _window_kernel(sink_ref, q_ref, kp_ref, ko_ref, kn_ref, vp_ref, vo_ref, vn_ref, bias_ref, o_ref):
    for kvh in range(A_KV_HEADS):
        hs = slice(kvh * HEAD_DIM, (kvh + 1) * HEAD_DIM)
        heads = [kvh * A_GROUP + g for g in range(A_GROUP)]
        q4 = jnp.concatenate([q_ref[:, h * HEAD_DIM:(h + 1) * HEAD_DIM] for h in heads], axis=0)
        kc = jnp.concatenate([kp_ref[:, hs], ko_ref[:, hs], kn_ref[:, hs]], axis=0)
        vc = jnp.concatenate([vp_ref[:, hs], vo_ref[:, hs], vn_ref[:, hs]], axis=0)
        s = lax.dot_general(q4, kc, (((1,), (1,)), ((), ())), preferred_element_type=F32)
        s = s + bias_ref[0, kvh]
        m = jnp.max(s, axis=-1, keepdims=True)
        p = jnp.exp(s - m)
        sink = jnp.concatenate([jnp.full((A_BLOCK, 1), sink_ref[h], F32) for h in heads], axis=0)
        denom = jnp.sum(p, axis=-1, keepdims=True) + jnp.exp(sink - m)
        o = jnp.dot(p.astype(BF16), vc, preferred_element_type=F32) / denom
        for g, h in enumerate(heads):
            o_ref[:, h * HEAD_DIM:(h + 1) * HEAD_DIM] = o[g * A_BLOCK:(g + 1) * A_BLOCK].astype(o_ref.dtype)


def _window_attention(proj, bias_a, sink, batch, seq):
    nb = seq // A_BLOCK

    def cls(n):
        return jnp.where(n == 0, 0, jnp.where(n == nb - 1, 2, 1))

    def row(b, n):
        return b * nb + n

    kcol, vcol = COL_KA // A_KV_WIDTH, COL_VA // A_KV_WIDTH
    prev = lambda b, n: row(b, jnp.maximum(n - 1, 0))
    nxt = lambda b, n: row(b, jnp.minimum(n + 1, nb - 1))
    kv_spec = lambda rfn, col: pl.BlockSpec((A_BLOCK, A_KV_WIDTH), lambda b, n: (rfn(b, n), col))
    return pl.pallas_call(
        _window_kernel,
        grid=(batch, nb),
        in_specs=[
            pl.BlockSpec(memory_space=pltpu.SMEM),
            pl.BlockSpec((A_BLOCK, A_WIDTH), lambda b, n: (row(b, n), COL_QA // A_WIDTH)),
            kv_spec(prev, kcol), kv_spec(row, kcol), kv_spec(nxt, kcol),
            kv_spec(prev, vcol), kv_spec(row, vcol), kv_spec(nxt, vcol),
            pl.BlockSpec((1, A_KV_HEADS, A_GROUP * A_BLOCK, 3 * A_BLOCK), lambda b, n: (cls(n), 0, 0, 0)),
        ],
        out_specs=pl.BlockSpec((A_BLOCK, A_WIDTH), lambda b, n: (row(b, n), 0)),
        out_shape=jax.ShapeDtypeStruct((batch * seq, A_WIDTH), BF16),
        compiler_params=pltpu.CompilerParams(
            dimension_semantics=("parallel", "arbitrary"), vmem_limit_bytes=VMEM_LIMIT),
        name="window_attention",
    )(sink, proj, proj, proj, proj, proj, proj, proj, bias_a)


def _neighbourhood_bias(rpb, rows):
    kh = min(NB_ROWS, rows)
    nblk = rows // NA_ROWS
    r0 = jnp.array([0, NA_ROWS, (nblk - 1) * NA_ROWS])[:, None]
    r = r0 + jnp.arange(NA_ROWS)[None, :]
    rs = jnp.clip(r - kh // 2, 0, rows - kh)
    key_r = (r0 - NA_ROWS) + jnp.arange(3 * NA_ROWS)[None, :]
    row_ok = (key_r[:, None, :] >= rs[:, :, None]) & (key_r[:, None, :] < rs[:, :, None] + kh)
    dr = key_r[:, None, :] - r[:, :, None] + (NB_ROWS - 1)
    cols = jnp.arange(GRID_W)
    cs = jnp.clip(cols - NB_COLS // 2, 0, GRID_W - NB_COLS)
    col_ok = (cols[None, :] >= cs[:, None]) & (cols[None, :] < cs[:, None] + NB_COLS)
    dc = cols[None, :] - cols[:, None] + (NB_COLS - 1)
    row_sel = ((dr[..., None] == jnp.arange(2 * NB_ROWS - 1)) & row_ok[..., None]).astype(F32)
    col_sel = ((dc[..., None] == jnp.arange(2 * NB_COLS - 1)) & col_ok[..., None]).astype(F32)
    bias = jnp.einsum('hde,aikd,cje->ahickj', rpb.astype(F32), row_sel, col_sel,
                      precision=lax.Precision.HIGHEST)
    ok = row_ok[:, None, :, None, :, None] & col_ok[None, None, None, :, None, :]
    full = jnp.where(ok, bias, NEG_INF)
    return full.reshape(3, B_HEADS, NA_BLOCK, 3 * NA_BLOCK)


def _neighbourhood_kernel(q_ref, kp_ref, ko_ref, kn_ref, vp_ref, vo_ref, vn_ref, bias_ref, o_ref):
    for h in range(B_HEADS):
        hs = slice(h * HEAD_DIM, (h + 1) * HEAD_DIM)
        kc = jnp.concatenate([kp_ref[:, hs], ko_ref[:, hs], kn_ref[:, hs]], axis=0)
        vc = jnp.concatenate([vp_ref[:, hs], vo_ref[:, hs], vn_ref[:, hs]], axis=0)
        s = lax.dot_general(q_ref[:, hs], kc, (((1,), (1,)), ((), ())), preferred_element_type=F32)
        s = s + bias_ref[0, h]
        m = jnp.max(s, axis=-1, keepdims=True)
        p = jnp.exp(s - m)
        denom = jnp.sum(p, axis=-1, keepdims=True)
        o = jnp.dot(p.astype(BF16), vc, preferred_element_type=F32) / denom
        o_ref[:, hs] = o.astype(o_ref.dtype)


def _neighbourhood_attention(proj, bias_b, batch, seq):
    nb = seq // NA_BLOCK

    def cls(n):
        return jnp.where(n == 0, 0, jnp.where(n == nb - 1, 2, 1))

    def row(b, n):
        return b * nb + n

    prev = lambda b, n: row(b, jnp.maximum(n - 1, 0))
    nxt = lambda b, n: row(b, jnp.minimum(n + 1, nb - 1))
    spec = lambda rfn, col: pl.BlockSpec((NA_BLOCK, B_WIDTH), lambda b, n: (rfn(b, n), col))
    qcol, kcol, vcol = COL_QB // B_WIDTH, COL_KB // B_WIDTH, COL_VB // B_WIDTH
    return pl.pallas_call(
        _neighbourhood_kernel,
        grid=(batch, nb),
        in_specs=[
            spec(row, qcol),
            spec(prev, kcol), spec(row, kcol), spec(nxt, kcol),
            spec(prev, vcol), spec(row, vcol), spec(nxt, vcol),
            pl.BlockSpec((1, B_HEADS, NA_BLOCK, 3 * NA_BLOCK), lambda b, n: (cls(n), 0, 0, 0)),
        ],
        out_specs=pl.BlockSpec((NA_BLOCK, B_WIDTH), lambda b, n: (row(b, n), 0)),
        out_shape=jax.ShapeDtypeStruct((batch * seq, B_WIDTH), BF16),
        compiler_params=pltpu.CompilerParams(
            dimension_semantics=("parallel", "arbitrary"), vmem_limit_bytes=VMEM_LIMIT),
        name="neighbourhood_attention",
    )(proj, proj, proj, proj, proj, proj, proj, bias_b)


def _merge_kernel(oa_ref, ob_ref, ga_ref, gb_ref, x_ref, wa_ref, wb_ref, wo_ref, g2_ref, wr_ref, br_ref,
                  h_ref, hn_ref, ids_ref, wcol_ref):
    ua = jnp.dot(oa_ref[...], wa_ref[...], preferred_element_type=F32)
    ub = jnp.dot(ob_ref[...], wb_ref[...], preferred_element_type=F32)
    merged = ga_ref[...].astype(F32) * ua + gb_ref[...].astype(F32) * ub
    h = x_ref[...] + jnp.dot(merged.astype(BF16), wo_ref[...], preferred_element_type=F32)
    h_ref[...] = h
    ms = jnp.mean(h * h, axis=-1, keepdims=True)
    hn = h * lax.rsqrt(ms + EPS) * g2_ref[...]
    hn_ref[...] = hn

    hn_hi = hn.astype(BF16)
    hn_lo = (hn - hn_hi.astype(F32)).astype(BF16)
    lg = jnp.dot(jnp.concatenate([hn_hi, hn_lo], axis=0), wr_ref[...], preferred_element_type=F32)
    tm = hn.shape[0]
    lg_hi, lg_lo = lg[:tm], lg[tm:]
    logits = lg_hi + pltpu.roll(lg_hi, ROUTER_LANES // 2, axis=1) + lg_lo + br_ref[...]
    lt = logits.T

    gl = lt[N_EXPERTS:N_EXPERTS + N_GROUPS]
    gmax = jnp.max(gl, axis=0, keepdims=True)
    giota = lax.broadcasted_iota(jnp.int32, gl.shape, 0)
    g_idx = jnp.min(jnp.where(gl == gmax, giota, N_GROUPS), axis=0, keepdims=True)
    p_g = 1.0 / jnp.sum(jnp.exp(gl - gmax), axis=0, keepdims=True)
    e_sel = jnp.zeros((EXPERTS_PER_GROUP, tm), F32)
    for g in range(N_GROUPS):
        e_sel = jnp.where(g_idx == g, lt[g * EXPERTS_PER_GROUP:(g + 1) * EXPERTS_PER_GROUP], e_sel)
    eiota = lax.broadcasted_iota(jnp.int32, e_sel.shape, 0)
    m1 = jnp.max(e_sel, axis=0, keepdims=True)
    i1 = jnp.min(jnp.where(e_sel == m1, eiota, EXPERTS_PER_GROUP), axis=0, keepdims=True)
    rest = jnp.where(eiota == i1, -jnp.inf, e_sel)
    m2 = jnp.max(rest, axis=0, keepdims=True)
    i2 = jnp.min(jnp.where(rest == m2, eiota, EXPERTS_PER_GROUP), axis=0, keepdims=True)
    t = jnp.exp(m2 - m1)
    w1 = p_g / (1.0 + t)
    w2 = p_g * t / (1.0 + t)
    ids_ref[...] = jnp.concatenate([g_idx * EXPERTS_PER_GROUP + i1, g_idx * EXPERTS_PER_GROUP + i2], axis=0)
    riota = lax.broadcasted_iota(jnp.int32, (ROUTER_LANES, tm), 0)
    wrows = jnp.where(riota == 0, w1, jnp.where(riota == 1, w2, 0.0))
    wcol_ref[...] = wrows.T


def _merge_and_route(oa, ob, proj, x2, wa, wb, wo, g2, wr_cat, br):
    t = x2.shape[0]
    const = lambda shape: pl.BlockSpec(shape, lambda i: (0, 0), pipeline_mode=pl.Buffered(1))
    return pl.pallas_call(
        _merge_kernel,
        grid=(t // MERGE_TM,),
        in_specs=[
            pl.BlockSpec((MERGE_TM, A_WIDTH), lambda i: (i, 0)),
            pl.BlockSpec((MERGE_TM, B_WIDTH), lambda i: (i, 0)),
            pl.BlockSpec((MERGE_TM, D_MODEL), lambda i: (i, COL_GA // D_MODEL)),
            pl.BlockSpec((MERGE_TM, D_MODEL), lambda i: (i, COL_GB // D_MODEL)),
            pl.BlockSpec((MERGE_TM, D_MODEL), lambda i: (i, 0)),
            const((A_WIDTH, D_MODEL)), const((B_WIDTH, D_MODEL)), const((D_MODEL, D_MODEL)),
            const((1, D_MODEL)), const((D_MODEL, ROUTER_LANES)), const((1, ROUTER_LANES)),
        ],
        out_specs=[
            pl.BlockSpec((MERGE_TM, D_MODEL), lambda i: (i, 0)),
            pl.BlockSpec((MERGE_TM, D_MODEL), lambda i: (i, 0)),
            pl.BlockSpec((TOP_K, MERGE_TM), lambda i: (0, i)),
            pl.BlockSpec((MERGE_TM, ROUTER_LANES), lambda i: (i, 0)),
        ],
        out_shape=[
            jax.ShapeDtypeStruct((t, D_MODEL), F32),
            jax.ShapeDtypeStruct((t, D_MODEL), F32),
            jax.ShapeDtypeStruct((TOP_K, t), jnp.int32),
            jax.ShapeDtypeStruct((t, ROUTER_LANES), F32),
        ],
        compiler_params=pltpu.CompilerParams(
            dimension_semantics=("parallel",), vmem_limit_bytes=VMEM_LIMIT),
        name="merge_and_route",
    )(oa, ob, proj, proj, x2, wa, wb, wo, g2, wr_cat, br)


def _expert_kernel(tok_ref, texp_ref, nused_ref, hn_hbm, wg_ref, wu_ref, wd_ref, y_ref, xbuf, sem):
    i = pl.program_id(0)
    n_used = nused_ref[0]
    slot = lax.rem(i, 2)

    def row_copy(tok, dst_slot, r):
        return pltpu.make_async_copy(hn_hbm.at[pl.ds(tok, 1), :], xbuf.at[dst_slot, pl.ds(r, 1), :],
                                     sem.at[dst_slot])

    def issue(tile, dst_slot):
        base = tile * MOE_TM

        def body(r, carry):
            row_copy(tok_ref[base + r], dst_slot, r).start()
            return carry

        lax.fori_loop(0, MOE_TM, body, 0)

    @pl.when(jnp.logical_and(i == 0, n_used > 0))
    def _():
        issue(0, 0)

    @pl.when(i + 1 < n_used)
    def _():
        issue(i + 1, 1 - slot)

    @pl.when(i < n_used)
    def _():
        pltpu.make_async_copy(hn_hbm.at[pl.ds(0, MOE_TM), :], xbuf.at[slot], sem.at[slot]).wait()
        xb = xbuf[slot].astype(BF16)
        g = jnp.dot(xb, wg_ref[0], preferred_element_type=F32)
        u = jnp.dot(xb, wu_ref[0], preferred_element_type=F32)
        a = (g * jax.nn.sigmoid(g) * u).astype(BF16)
        y_ref[...] = jnp.dot(a, wd_ref[0], preferred_element_type=F32)

    @pl.when(i >= n_used)
    def _():
        y_ref[...] = jnp.zeros_like(y_ref)


def _expert_ffn(tok_sorted, tile_expert, n_used, hn, wg, wu, wd, n_tiles):
    grid_spec = pltpu.PrefetchScalarGridSpec(
        num_scalar_prefetch=3,
        grid=(n_tiles,),
        in_specs=[
            pl.BlockSpec(memory_space=pl.ANY),
            pl.BlockSpec((1, D_MODEL, D_EXPERT), lambda i, tok, te, nu: (te[i], 0, 0)),
            pl.BlockSpec((1, D_MODEL, D_EXPERT), lambda i, tok, te, nu: (te[i], 0, 0)),
            pl.BlockSpec((1, D_EXPERT, D_MODEL), lambda i, tok, te, nu: (te[i], 0, 0)),
        ],
        out_specs=pl.BlockSpec((MOE_TM, D_MODEL), lambda i, tok, te, nu: (i, 0)),
        scratch_shapes=[pltpu.VMEM((2, MOE_TM, D_MODEL), F32), pltpu.SemaphoreType.DMA((2,))],
    )
    return pl.pallas_call(
        _expert_kernel,
        grid_spec=grid_spec,
        out_shape=jax.ShapeDtypeStruct((n_tiles * MOE_TM, D_MODEL), F32),
        compiler_params=pltpu.CompilerParams(
            dimension_semantics=("arbitrary",), vmem_limit_bytes=VMEM_LIMIT),
        name="expert_ffn",
    )(tok_sorted, tile_expert, n_used, hn, wg, wu, wd)


def _combine_kernel(pos_ref, h_ref, wcol_ref, y_hbm, o_ref, ybuf, sem):
    i = pl.program_id(0)
    n = pl.num_programs(0)
    slot = lax.rem(i, 2)
    n_tok = n * COMB_TM

    def row_copy(src_row, dst_slot, k, r):
        return pltpu.make_async_copy(y_hbm.at[pl.ds(src_row, 1), :], ybuf.at[dst_slot, k, pl.ds(r, 1), :],
                                     sem.at[dst_slot])

    def issue(tile, dst_slot):
        base = tile * COMB_TM

        def body(r, carry):
            for k in range(TOP_K):
                row_copy(pos_ref[k * n_tok + base + r], dst_slot, k, r).start()
            return carry

        lax.fori_loop(0, COMB_TM, body, 0)

    @pl.when(i == 0)
    def _():
        issue(0, 0)

    @pl.when(i + 1 < n)
    def _():
        issue(i + 1, 1 - slot)

    for k in range(TOP_K):
        pltpu.make_async_copy(y_hbm.at[pl.ds(0, COMB_TM), :], ybuf.at[slot, k], sem.at[slot]).wait()
    w = wcol_ref[...]
    o_ref[...] = h_ref[...] + w[:, 0:1] * ybuf[slot, 0] + w[:, 1:2] * ybuf[slot, 1]


def _combine(pos, h, wcol, y):
    t = h.shape[0]
    grid_spec = pltpu.PrefetchScalarGridSpec(
        num_scalar_prefetch=1,
        grid=(t // COMB_TM,),
        in_specs=[
            pl.BlockSpec((COMB_TM, D_MODEL), lambda i, pos: (i, 0)),
            pl.BlockSpec((COMB_TM, ROUTER_LANES), lambda i, pos: (i, 0)),
            pl.BlockSpec(memory_space=pl.ANY),
        ],
        out_specs=pl.BlockSpec((COMB_TM, D_MODEL), lambda i, pos: (i, 0)),
        scratch_shapes=[pltpu.VMEM((2, TOP_K, COMB_TM, D_MODEL), F32), pltpu.SemaphoreType.DMA((2,))],
    )
    return pl.pallas_call(
        _combine_kernel,
        grid_spec=grid_spec,
        out_shape=jax.ShapeDtypeStruct((t, D_MODEL), F32),
        compiler_params=pltpu.CompilerParams(
            dimension_semantics=("arbitrary",), vmem_limit_bytes=VMEM_LIMIT),
        name="combine",
    )(pos, h, wcol, y)


def _dispatch_plan(ids, n_tok):
    n_assign = TOP_K * n_tok
    n_tiles = n_assign // MOE_TM + N_EXPERTS
    flat_e = ids.reshape(n_assign)
    order = jnp.argsort(flat_e, stable=True).astype(jnp.int32)
    se = flat_e[order]
    counts = jnp.bincount(flat_e, length=N_EXPERTS).astype(jnp.int32)
    padded = ((counts + MOE_TM - 1) // MOE_TM) * MOE_TM
    end_pad = jnp.cumsum(padded)
    start_pad = end_pad - padded
    start_orig = jnp.cumsum(counts) - counts
    dest = start_pad[se] + (jnp.arange(n_assign, dtype=jnp.int32) - start_orig[se])
    tok_sorted = jnp.zeros((n_tiles * MOE_TM,), jnp.int32).at[dest].set(order % n_tok)
    pos = jnp.zeros((n_assign,), jnp.int32).at[order].set(dest)
    tile_start = jnp.arange(n_tiles, dtype=jnp.int32) * MOE_TM
    tile_expert = jnp.minimum(jnp.sum(end_pad[None, :] <= tile_start[:, None], axis=1), N_EXPERTS - 1).astype(jnp.int32)
    n_used = (end_pad[-1] // MOE_TM).astype(jnp.int32).reshape(1)
    return tok_sorted, pos, tile_expert, n_used, n_tiles


def _layer(x, norm1_g, w_in, q_norm_a, k_norm_a, sink_a, q_norm_b, k_norm_b, rpb_b, w_up_a, w_up_b, w_out,
           rel_table, norm2_g, w_rg, b_rg, w_re, b_re, w_gate, w_up, w_down):
    batch, seq, d = x.shape
    t = batch * seq
    x2 = x.reshape(t, d)

    cuts = [0, A_WIDTH, A_WIDTH + A_KV_WIDTH, A_WIDTH + 2 * A_KV_WIDTH]
    cuts += [cuts[-1] + B_WIDTH, cuts[-1] + 2 * B_WIDTH, cuts[-1] + 3 * B_WIDTH]
    cuts += [cuts[-1] + D_MODEL, cuts[-1] + 2 * D_MODEL]
    qa, ka, va, qb, kb, vb, ga, gb = [w_in[:, cuts[n]:cuts[n + 1]] for n in range(8)]
    w_cat = jnp.concatenate([qa, qb, kb, vb, ga, gb, ka, va], axis=1).astype(BF16)
    scale = HEAD_DIM ** -0.5
    ones = lambda n: jnp.ones((n,), F32)
    gain_cat = jnp.concatenate([
        jnp.tile(q_norm_a.astype(F32) * scale, A_HEADS), jnp.tile(q_norm_b.astype(F32) * scale, B_HEADS),
        jnp.tile(k_norm_b.astype(F32), B_HEADS), ones(B_WIDTH + 2 * D_MODEL),
        jnp.tile(k_norm_a.astype(F32), A_KV_HEADS), ones(A_KV_WIDTH)]).reshape(1, IN_COLS)

    proj = _in_projection(x2, norm1_g.astype(F32).reshape(1, d), w_cat, gain_cat)
    oa = _window_attention(proj, _window_bias(rel_table), sink_a.astype(F32), batch, seq)
    ob = _neighbourhood_attention(proj, _neighbourhood_bias(rpb_b, seq // GRID_W), batch, seq)

    wr = jnp.concatenate([w_re.astype(F32), w_rg.astype(F32)], axis=1)
    wr_hi = wr.astype(BF16)
    wr_lo = (wr - wr_hi.astype(F32)).astype(BF16)
    half = ROUTER_LANES // 2
    padc = lambda a: jnp.pad(a, ((0, 0), (0, half - a.shape[1])))
    wr_cat = jnp.concatenate([padc(wr_hi), padc(wr_lo)], axis=1)
    br = jnp.pad(jnp.concatenate([b_re.astype(F32), b_rg.astype(F32)]), (0, ROUTER_LANES - N_EXPERTS - N_GROUPS))

    h, hn, ids, wcol = _merge_and_route(
        oa, ob, proj, x2, w_up_a.astype(BF16), w_up_b.astype(BF16), w_out.astype(BF16),
        norm2_g.astype(F32).reshape(1, d), wr_cat, br.reshape(1, ROUTER_LANES))

    tok_sorted, pos, tile_expert, n_used, n_tiles = _dispatch_plan(ids, t)
    y = _expert_ffn(tok_sorted, tile_expert, n_used, hn,
                    w_gate.astype(BF16), w_up.astype(BF16), w_down.astype(BF16), n_tiles)
    out = _combine(pos, h, wcol, y)
    return out.reshape(batch, seq, d)


def kernel(x, norm1_g, w_in, q_norm_a, k_norm_a, sink_a, q_norm_b, k_norm_b, rpb_b, w_up_a, w_up_b, w_out,
           rel_bias_table, norm2_g, w_router_group, b_router_group, w_router_expert, b_router_expert,
           w_gate, w_up, w_down):
    return _layer(x, norm1_g[0], w_in[0], q_norm_a[0], k_norm_a[0], sink_a[0], q_norm_b[0], k_norm_b[0],
                  rpb_b[0], w_up_a[0], w_up_b[0], w_out[0], rel_bias_table, norm2_g[0],
                  w_router_group[0], b_router_group[0], w_router_expert[0], b_router_expert[0],
                  w_gate[0], w_up[0], w_down[0])
```

```python
import functools
import math

import jax
import jax.numpy as jnp
from jax import lax
from jax.experimental import pallas as pl
from jax.experimental.pallas import tpu as pltpu

F32 = jnp.float32
BF16 = jnp.bfloat16

D_MODEL = 2048
HEAD_DIM = 128
A_HEADS = 8
A_KV_HEADS = 2
A_GROUP = A_HEADS // A_KV_HEADS
A_WIDTH = A_HEADS * HEAD_DIM
A_KV_WIDTH = A_KV_HEADS * HEAD_DIM
WINDOW = 128
A_BLOCK = 128
NUM_BUCKETS = 32
MAX_DISTANCE = 128
B_HEADS = 8
B_WIDTH = B_HEADS * HEAD_DIM
GRID_W = 64
NB_ROWS = 8
NB_COLS = 16
IN_COLS = A_WIDTH + 2 * A_KV_WIDTH + 3 * B_WIDTH + 2 * D_MODEL
N_GROUPS = 4
EXPERTS_PER_GROUP = 8
N_EXPERTS = N_GROUPS * EXPERTS_PER_GROUP
TOP_K = 2
D_EXPERT = 512
EPS = 1e-6
NEG_INF = -1e30

COL_QA = 0
COL_QB = COL_QA + A_WIDTH
COL_KB = COL_QB + B_WIDTH
COL_VB = COL_KB + B_WIDTH
COL_GA = COL_VB + B_WIDTH
COL_GB = COL_GA + D_MODEL
COL_KA = COL_GB + D_MODEL
COL_VA = COL_KA + A_KV_WIDTH

PROJ_TM = 1024
PROJ_TN = 512
PROJ_SUB = 128
NORM_ROWS = 128
NA_ROWS = 4
NA_BLOCK = NA_ROWS * GRID_W
MERGE_TM = 256
ROUTER_LANES = 128
LOCAL_BLOCK = 8
LOCAL_ROWS = 768
LOCAL_BLOCKS = LOCAL_ROWS // LOCAL_BLOCK
assert LOCAL_ROWS >= TOP_K * MERGE_TM + N_EXPERTS * (LOCAL_BLOCK - 1) and LOCAL_BLOCKS <= ROUTER_LANES
MOE_TM = 256
MOE_BLOCKS = MOE_TM // LOCAL_BLOCK
VMEM_LIMIT = 56 * 1024 * 1024


def _head_norm(a, gain):
    outs = []
    for hd in range(a.shape[1] // HEAD_DIM):
        sl = slice(hd * HEAD_DIM, (hd + 1) * HEAD_DIM)
        ah = a[:, sl]
        ss = jnp.sum(ah * ah, axis=-1, keepdims=True)
        outs.append(ah * lax.rsqrt(ss * (1.0 / HEAD_DIM) + EPS) * gain[:, sl])
    return outs[0] if len(outs) == 1 else jnp.concatenate(outs, axis=1)


def _inproj_kernel(x_ref, g1_ref, w_ref, gain_ref, kind_ref, o_ref, xn_ref):
    j = pl.program_id(1)

    @pl.when(j == 0)
    def _():
        def norm_rows(c, carry):
            rows = pl.ds(pl.multiple_of(c * NORM_ROWS, NORM_ROWS), NORM_ROWS)
            x = x_ref[rows, :]
            ms = jnp.mean(x * x, axis=-1, keepdims=True)
            xn_ref[rows, :] = (x * lax.rsqrt(ms + EPS) * g1_ref[...]).astype(BF16)
            return carry

        lax.fori_loop(0, PROJ_TM // NORM_ROWS, norm_rows, 0)

    kind = kind_ref[...]
    for r in range(PROJ_TM // PROJ_SUB):
        rows = slice(r * PROJ_SUB, (r + 1) * PROJ_SUB)
        acc = jnp.dot(xn_ref[rows, :], w_ref[...], preferred_element_type=F32)
        out = jnp.where(kind == 1.0, _head_norm(acc, gain_ref[...]),
                        jnp.where(kind == 2.0, jax.nn.sigmoid(acc), acc))
        o_ref[rows, :] = out.astype(o_ref.dtype)


def _in_projection(x2, g1, w_cat, gain_cat, kind_cat):
    t = x2.shape[0]
    return pl.pallas_call(
        _inproj_kernel,
        grid=(t // PROJ_TM, IN_COLS // PROJ_TN),
        in_specs=[
            pl.BlockSpec((PROJ_TM, D_MODEL), lambda i, j: (i, 0)),
            pl.BlockSpec((1, D_MODEL), lambda i, j: (0, 0)),
            pl.BlockSpec((D_MODEL, PROJ_TN), lambda i, j: (0, j)),
            pl.BlockSpec((1, PROJ_TN), lambda i, j: (0, j)),
            pl.BlockSpec((1, PROJ_TN), lambda i, j: (0, j)),
        ],
        out_specs=pl.BlockSpec((PROJ_TM, PROJ_TN), lambda i, j: (i, j)),
        out_shape=jax.ShapeDtypeStruct((t, IN_COLS), BF16),
        scratch_shapes=[pltpu.VMEM((PROJ_TM, D_MODEL), BF16)],
        compiler_params=pltpu.CompilerParams(
            dimension_semantics=("parallel", "arbitrary"), vmem_limit_bytes=VMEM_LIMIT),
        name="in_projection",
    )(x2, g1, w_cat, gain_cat, kind_cat)


def _t5_bucket(rel):
    nb = NUM_BUCKETS // 2
    ret = jnp.where(rel > 0, nb, 0)
    n = jnp.abs(rel)
    max_exact = nb // 2
    nf = jnp.maximum(n, 1).astype(jnp.float32)
    large = max_exact + (jnp.log(nf / max_exact) / math.log(MAX_DISTANCE / max_exact)
                         * (nb - max_exact)).astype(jnp.int32)
    large = jnp.minimum(large, nb - 1)
    return ret + jnp.where(n < max_exact, n, large)


def _window_bias(rel_table):
    q_loc = jnp.arange(A_BLOCK)
    k_loc = jnp.arange(3 * A_BLOCK) - WINDOW
    rel = k_loc[None, :] - q_loc[:, None]
    onehot = (_t5_bucket(rel)[..., None] == jnp.arange(NUM_BUCKETS)).astype(F32)
    bias = jnp.einsum('qkn,nh->hqk', onehot, rel_table.astype(F32), precision=lax.Precision.HIGHEST)
    in_window = jnp.abs(rel) <= WINDOW
    has_prev = jnp.array([False, True, True])[:, None, None]
    has_next = jnp.array([True, True, False])[:, None, None]
    k_blk = (jnp.arange(3 * A_BLOCK) // A_BLOCK)[None, None, :]
    valid = in_window[None] & ((k_blk != 0) | has_prev) & ((k_blk != 2) | has_next)
    full = jnp.where(valid[:, None], bias[None], NEG_INF)
    return full.reshape(3, A_KV_HEADS, A_GROUP * A_BLOCK, 3 * A_BLOCK)


def _window_kernel(sink_ref, q_ref, kp_ref, ko_ref, kn_ref, vp_ref, vo_ref, vn_ref, bias_ref, o_ref):
    for kvh in range(A_KV_HEADS):
        hs = slice(kvh * HEAD_DIM, (kvh + 1) * HEAD_DIM)
        heads = [kvh * A_GROUP + g for g in range(A_GROUP)]
        q4 = jnp.concatenate([q_ref[:, h * HEAD_DIM:(h + 1) * HEAD_DIM] for h in heads], axis=0)
        kc = jnp.concatenate([kp_ref[:, hs], ko_ref[:, hs], kn_ref[:, hs]], axis=0)
        vc = jnp.concatenate([vp_ref[:, hs], vo_ref[:, hs], vn_ref[:, hs]], axis=0)
        s = lax.dot_general(q4, kc, (((1,), (1,)), ((), ())), preferred_element_type=F32)
        s = s + bias_ref[0, kvh]
        m = jnp.max(s, axis=-1, keepdims=True)
        p = jnp.exp(s - m)
        sink = jnp.concatenate([jnp.full((A_BLOCK, 1), sink_ref[h], F32) for h in heads], axis=0)
        denom = jnp.sum(p, axis=-1, keepdims=True) + jnp.exp(sink - m)
        o = jnp.dot(p.astype(BF16), vc, preferred_element_type=F32) / denom
        for g, h in enumerate(heads):
            o_ref[:, h * HEAD_DIM:(h + 1) * HEAD_DIM] = o[g * A_BLOCK:(g + 1) * A_BLOCK].astype(o_ref.dtype)


def _window_attention(proj, bias_a, sink, batch, seq):
    nb = seq // A_BLOCK

    def cls(n):
        return jnp.where(n == 0, 0, jnp.where(n == nb - 1, 2, 1))

    def row(b, n):
        return b * nb + n

    kcol, vcol = COL_KA // A_KV_WIDTH, COL_VA // A_KV_WIDTH
    prev = lambda b, n: row(b, jnp.maximum(n - 1, 0))
    nxt = lambda b, n: row(b, jnp.minimum(n + 1, nb - 1))
    kv_spec = lambda rfn, col: pl.BlockSpec((A_BLOCK, A_KV_WIDTH), lambda b, n: (rfn(b, n), col))
    return pl.pallas_call(
        _window_kernel,
        grid=(batch, nb),
        in_specs=[
            pl.BlockSpec(memory_space=pltpu.SMEM),
            pl.BlockSpec((A_BLOCK, A_WIDTH), lambda b, n: (row(b, n), COL_QA // A_WIDTH)),
            kv_spec(prev, kcol), kv_spec(row, kcol), kv_spec(nxt, kcol),
            kv_spec(prev, vcol), kv_spec(row, vcol), kv_spec(nxt, vcol),
            pl.BlockSpec((1, A_KV_HEADS, A_GROUP * A_BLOCK, 3 * A_BLOCK), lambda b, n: (cls(n), 0, 0, 0)),
        ],
        out_specs=pl.BlockSpec((A_BLOCK, A_WIDTH), lambda b, n: (row(b, n), 0)),
        out_shape=jax.ShapeDtypeStruct((batch * seq, A_WIDTH), BF16),
        compiler_params=pltpu.CompilerParams(
            dimension_semantics=("parallel", "arbitrary"), vmem_limit_bytes=VMEM_LIMIT),
        name="window_attention",
    )(sink, proj, proj, proj, proj, proj, proj, proj, bias_a)


def _neighbourhood_bias(rpb, rows):
    kh = min(NB_ROWS, rows)
    nblk = rows // NA_ROWS
    r0 = jnp.array([0, NA_ROWS, (nblk - 1) * NA_ROWS])[:, None]
    r = r0 + jnp.arange(NA_ROWS)[None, :]
    rs = jnp.clip(r - kh // 2, 0, rows - kh)
    key_r = (r0 - NA_ROWS) + jnp.arange(3 * NA_ROWS)[None, :]
    row_ok = (key_r[:, None, :] >= rs[:, :, None]) & (key_r[:, None, :] < rs[:, :, None] + kh)
    dr = key_r[:, None, :] - r[:, :, None] + (NB_ROWS - 1)
    cols = jnp.arange(GRID_W)
    cs = jnp.clip(cols - NB_COLS // 2, 0, GRID_W - NB_COLS)
    col_ok = (cols[None, :] >= cs[:, None]) & (cols[None, :] < cs[:, None] + NB_COLS)
    dc = cols[None, :] - cols[:, None] + (NB_COLS - 1)
    row_sel = ((dr[..., None] == jnp.arange(2 * NB_ROWS - 1)) & row_ok[..., None]).astype(F32)
    col_sel = ((dc[..., None] == jnp.arange(2 * NB_COLS - 1)) & col_ok[..., None]).astype(F32)
    bias = jnp.einsum('hde,aikd,cje->ahickj', rpb.astype(F32), row_sel, col_sel,
                      precision=lax.Precision.HIGHEST)
    ok = row_ok[:, None, :, None, :, None] & col_ok[None, None, None, :, None, :]
    full = jnp.where(ok, bias, NEG_INF)
    return full.reshape(3, B_HEADS, NA_BLOCK, 3 * NA_BLOCK)


def _neighbourhood_kernel(q_ref, kp_ref, ko_ref, kn_ref, vp_ref, vo_ref, vn_ref, bias_ref, o_ref):
    for h in range(B_HEADS):
        hs = slice(h * HEAD_DIM, (h + 1) * HEAD_DIM)
        kc = jnp.concatenate([kp_ref[:, hs], ko_ref[:, hs], kn_ref[:, hs]], axis=0)
        vc = jnp.concatenate([vp_ref[:, hs], vo_ref[:, hs], vn_ref[:, hs]], axis=0)
        s = lax.dot_general(q_ref[:, hs], kc, (((1,), (1,)), ((), ())), preferred_element_type=F32)
        s = s + bias_ref[0, h]
        m = jnp.max(s, axis=-1, keepdims=True)
        p = jnp.exp(s - m)
        denom = jnp.sum(p, axis=-1, keepdims=True)
        o = jnp.dot(p.astype(BF16), vc, preferred_element_type=F32) / denom
        o_ref[:, hs] = o.astype(o_ref.dtype)


def _neighbourhood_attention(proj, bias_b, batch, seq):
    nb = seq // NA_BLOCK

    def cls(n):
        return jnp.where(n == 0, 0, jnp.where(n == nb - 1, 2, 1))

    def row(b, n):
        return b * nb + n

    prev = lambda b, n: row(b, jnp.maximum(n - 1, 0))
    nxt = lambda b, n: row(b, jnp.minimum(n + 1, nb - 1))
    spec = lambda rfn, col: pl.BlockSpec((NA_BLOCK, B_WIDTH), lambda b, n: (rfn(b, n), col))
    qcol, kcol, vcol = COL_QB // B_WIDTH, COL_KB // B_WIDTH, COL_VB // B_WIDTH
    return pl.pallas_call(
        _neighbourhood_kernel,
        grid=(batch, nb),
        in_specs=[
            spec(row, qcol),
            spec(prev, kcol), spec(row, kcol), spec(nxt, kcol),
            spec(prev, vcol), spec(row, vcol), spec(nxt, vcol),
            pl.BlockSpec((1, B_HEADS, NA_BLOCK, 3 * NA_BLOCK), lambda b, n: (cls(n), 0, 0, 0)),
        ],
        out_specs=pl.BlockSpec((NA_BLOCK, B_WIDTH), lambda b, n: (row(b, n), 0)),
        out_shape=jax.ShapeDtypeStruct((batch * seq, B_WIDTH), BF16),
        compiler_params=pltpu.CompilerParams(
            dimension_semantics=("parallel", "arbitrary"), vmem_limit_bytes=VMEM_LIMIT),
        name="neighbourhood_attention",
    )(proj, proj, proj, proj, proj, proj, proj, bias_b)


def _pack_bf16_pairs(a):
    n = a.shape[1] // 2
    bits = lax.bitcast_convert_type(a, jnp.uint32)
    return bits[:, n:] | (bits[:, :n] >> 16)


def _unpack_bf16_pairs(p):
    lo = lax.bitcast_convert_type(p << 16, F32).astype(BF16)
    hi = lax.bitcast_convert_type(p & jnp.uint32(0xFFFF0000), F32).astype(BF16)
    return lo, hi


def _merge_kernel(oa_ref, ob_ref, ga_ref, gb_ref, x_ref, wa_ref, wb_ref, wo_ref, g2_ref, wr_ref, br_ref,
                  upper_ref, lower_ref, h_ref, xs_ref, eb_ref, wcol_ref):
    ua = jnp.dot(oa_ref[...], wa_ref[...], preferred_element_type=F32)
    ub = jnp.dot(ob_ref[...], wb_ref[...], preferred_element_type=F32)
    merged = ga_ref[...].astype(F32) * ua + gb_ref[...].astype(F32) * ub
    h = x_ref[...] + jnp.dot(merged.astype(BF16), wo_ref[...], preferred_element_type=F32)
    h_ref[...] = h
    ms = jnp.mean(h * h, axis=-1, keepdims=True)
    hn = h * lax.rsqrt(ms + EPS) * g2_ref[...]

    hn_hi = hn.astype(BF16)
    hn_lo = (hn - hn_hi.astype(F32)).astype(BF16)
    lg = jnp.dot(jnp.concatenate([hn_hi, hn_lo], axis=0), wr_ref[...], preferred_element_type=F32)
    tm = hn.shape[0]
    lg_hi, lg_lo = lg[:tm], lg[tm:]
    logits = lg_hi + pltpu.roll(lg_hi, ROUTER_LANES // 2, axis=1) + lg_lo + br_ref[...]
    lt = logits.T

    gl = lt[N_EXPERTS:N_EXPERTS + N_GROUPS]
    gmax = jnp.max(gl, axis=0, keepdims=True)
    giota = lax.broadcasted_iota(jnp.int32, gl.shape, 0)
    g_idx = jnp.min(jnp.where(gl == gmax, giota, N_GROUPS), axis=0, keepdims=True)
    p_g = 1.0 / jnp.sum(jnp.exp(gl - gmax), axis=0, keepdims=True)
    e_sel = jnp.zeros((EXPERTS_PER_GROUP, tm), F32)
    for g in range(N_GROUPS):
        e_sel = jnp.where(g_idx == g, lt[g * EXPERTS_PER_GROUP:(g + 1) * EXPERTS_PER_GROUP], e_sel)
    eiota = lax.broadcasted_iota(jnp.int32, e_sel.shape, 0)
    m1 = jnp.max(e_sel, axis=0, keepdims=True)
    i1 = jnp.min(jnp.where(e_sel == m1, eiota, EXPERTS_PER_GROUP), axis=0, keepdims=True)
    rest = jnp.where(eiota == i1, -jnp.inf, e_sel)
    m2 = jnp.max(rest, axis=0, keepdims=True)
    i2 = jnp.min(jnp.where(rest == m2, eiota, EXPERTS_PER_GROUP), axis=0, keepdims=True)
    t = jnp.exp(m2 - m1)
    w1 = p_g / (1.0 + t)
    w2 = p_g * t / (1.0 + t)

    ids = jnp.concatenate([g_idx * EXPERTS_PER_GROUP + i1, g_idx * EXPERTS_PER_GROUP + i2], axis=1)
    eiota32 = lax.broadcasted_iota(jnp.int32, (N_EXPERTS, TOP_K * tm), 0)
    onehot = jnp.where(eiota32 == ids, 1.0, 0.0)
    count = jnp.sum(onehot, axis=1, keepdims=True)
    before = jnp.dot(onehot.astype(BF16), upper_ref[...], preferred_element_type=F32)
    nblk = jnp.floor((count + (LOCAL_BLOCK - 1)) * (1.0 / LOCAL_BLOCK))
    boff = jnp.dot(lower_ref[...], jnp.broadcast_to(nblk, (N_EXPERTS, ROUTER_LANES)).astype(BF16),
                   preferred_element_type=F32)[:, 0:1]
    lrow = jnp.sum(onehot * (before + LOCAL_BLOCK * boff), axis=0, keepdims=True)
    biota = lax.broadcasted_iota(jnp.int32, (N_EXPERTS, ROUTER_LANES), 1).astype(F32)
    owns = jnp.where(jnp.logical_and(biota >= boff, biota < boff + nblk), 1.0, 0.0)
    eiota_f = lax.broadcasted_iota(jnp.int32, (N_EXPERTS, ROUTER_LANES), 0).astype(F32)
    eb = jnp.sum(owns * eiota_f, axis=0, keepdims=True) + N_EXPERTS * (1.0 - jnp.sum(owns, axis=0, keepdims=True))
    eb_ref[0] = eb.astype(jnp.int32)

    jiota = lax.broadcasted_iota(jnp.int32, (LOCAL_ROWS, tm), 0).astype(F32)
    place = jnp.where(jiota == lrow[:, :tm], 1.0, jnp.where(jiota == lrow[:, tm:], 1.0, 0.0)).astype(BF16)
    xs_ref[...] = _pack_bf16_pairs(jnp.dot(place, hn_hi, preferred_element_type=F32))

    riota = lax.broadcasted_iota(jnp.int32, (ROUTER_LANES, tm), 0)
    wrows = jnp.where(riota == 0, w1, jnp.where(riota == 1, w2,
                      jnp.where(riota == 2, lrow[:, :tm], jnp.where(riota == 3, lrow[:, tm:], 0.0))))
    wcol_ref[...] = wrows.T


def _merge_and_route(oa, ob, proj, x2, wa, wb, wo, g2, wr_cat, br):
    t = x2.shape[0]
    n_assign = TOP_K * MERGE_TM
    upper = (jnp.arange(n_assign)[:, None] < jnp.arange(n_assign)[None, :]).astype(BF16)
    lower = (jnp.arange(N_EXPERTS)[:, None] > jnp.arange(N_EXPERTS)[None, :]).astype(BF16)
    const = lambda shape: pl.BlockSpec(shape, lambda i: (0, 0), pipeline_mode=pl.Buffered(1))
    return pl.pallas_call(
        _merge_kernel,
        grid=(t // MERGE_TM,),
        in_specs=[
            pl.BlockSpec((MERGE_TM, A_WIDTH), lambda i: (i, 0)),
            pl.BlockSpec((MERGE_TM, B_WIDTH), lambda i: (i, 0)),
            pl.BlockSpec((MERGE_TM, D_MODEL), lambda i: (i, COL_GA // D_MODEL)),
            pl.BlockSpec((MERGE_TM, D_MODEL), lambda i: (i, COL_GB // D_MODEL)),
            pl.BlockSpec((MERGE_TM, D_MODEL), lambda i: (i, 0)),
            const((A_WIDTH, D_MODEL)), const((B_WIDTH, D_MODEL)), const((D_MODEL, D_MODEL)),
            const((1, D_MODEL)), const((D_MODEL, ROUTER_LANES)), const((1, ROUTER_LANES)),
            const((n_assign, n_assign)), const((N_EXPERTS, N_EXPERTS)),
        ],
        out_specs=[
            pl.BlockSpec((MERGE_TM, D_MODEL), lambda i: (i, 0)),
            pl.BlockSpec((LOCAL_ROWS, D_MODEL // 2), lambda i: (i, 0)),
            pl.BlockSpec((1, 1, ROUTER_LANES), lambda i: (i, 0, 0)),
            pl.BlockSpec((MERGE_TM, ROUTER_LANES), lambda i: (i, 0)),
        ],
        out_shape=[
            jax.ShapeDtypeStruct((t, D_MODEL), F32),
            jax.ShapeDtypeStruct((t // MERGE_TM * LOCAL_ROWS, D_MODEL // 2), jnp.uint32),
            jax.ShapeDtypeStruct((t // MERGE_TM, 1, ROUTER_LANES), jnp.int32),
            jax.ShapeDtypeStruct((t, ROUTER_LANES), F32),
        ],
        compiler_params=pltpu.CompilerParams(
            dimension_semantics=("parallel",), vmem_limit_bytes=VMEM_LIMIT),
        name="merge_and_route",
    )(oa, ob, proj, proj, x2, wa, wb, wo, g2, wr_cat, br, upper, lower)


def _expert_kernel(blk_ref, texp_ref, tcb_ref, tnb_ref, nused_ref, xs_hbm, wg_ref, wu_ref, wd_ref, y_hbm,
                   xbuf, ybuf, wgb, wub, wdb, gsem, ssem):
    i = pl.program_id(0)
    n_used = nused_ref[0]
    slot = lax.rem(i, 2)
    expert = texp_ref[i]

    def gather_copy(blk, dst_slot, j):
        return pltpu.make_async_copy(xs_hbm.at[blk], xbuf.at[dst_slot, pl.ds(j * LOCAL_BLOCK, LOCAL_BLOCK)],
                                     gsem.at[dst_slot])

    def scatter_copy(blk, src_slot, j):
        return pltpu.make_async_copy(ybuf.at[src_slot, pl.ds(j * LOCAL_BLOCK, LOCAL_BLOCK)], y_hbm.at[blk],
                                     ssem.at[src_slot])

    def issue_gather(tile, dst_slot):
        base = tcb_ref[tile]
        for j in range(MOE_BLOCKS):
            gather_copy(blk_ref[base + j], dst_slot, j).start()

    def wait_gather(s):
        for j in range(MOE_BLOCKS):
            gather_copy(0, s, j).wait()

    def issue_scatter(tile, src_slot):
        base = tcb_ref[tile]
        n_valid = tnb_ref[tile]
        for j in range(MOE_BLOCKS):
            @pl.when(j < n_valid)
            def _():
                scatter_copy(blk_ref[base + j], src_slot, j).start()

    def wait_scatter(tile, s):
        n_valid = tnb_ref[tile]
        for j in range(MOE_BLOCKS):
            @pl.when(j < n_valid)
            def _():
                scatter_copy(0, s, j).wait()

    @pl.when(jnp.logical_and(i == 0, n_used > 0))
    def _():
        issue_gather(0, 0)

    @pl.when(i < n_used)
    def _():
        wait_gather(slot)

        @pl.when(i + 1 < n_used)
        def _():
            issue_gather(i + 1, 1 - slot)

        @pl.when(i >= 2)
        def _():
            wait_scatter(i - 2, slot)

        @pl.when(jnp.logical_and(expert < N_EXPERTS, jnp.logical_or(i == 0, expert != texp_ref[jnp.maximum(i - 1, 0)])))
        def _():
            wgb[...] = wg_ref[0].astype(BF16)
            wub[...] = wu_ref[0].astype(BF16)
            wdb[...] = wd_ref[0].astype(BF16)

        @pl.when(expert < N_EXPERTS)
        def _():
            x_lo, x_hi = _unpack_bf16_pairs(xbuf[slot])
            half = D_MODEL // 2
            g = (jnp.dot(x_lo, wgb[:half, :], preferred_element_type=F32)
                 + jnp.dot(x_hi, wgb[half:, :], preferred_element_type=F32))
            u = (jnp.dot(x_lo, wub[:half, :], preferred_element_type=F32)
                 + jnp.dot(x_hi, wub[half:, :], preferred_element_type=F32))
            act = (g * jax.nn.sigmoid(g) * u).astype(BF16)
            y = jnp.dot(act, wdb[...], preferred_element_type=F32)
            ybuf[slot] = _pack_bf16_pairs(y.astype(BF16).astype(F32))

        @pl.when(expert >= N_EXPERTS)
        def _():
            ybuf[slot] = jnp.zeros((MOE_TM, D_MODEL // 2), jnp.uint32)

        issue_scatter(i, slot)

        @pl.when(i == n_used - 1)
        def _():
            wait_scatter(i, slot)

            @pl.when(i >= 1)
            def _():
                wait_scatter(i - 1, 1 - slot)


def _expert_ffn(block_order, tile_expert, tile_base, tile_blocks, n_used, xs_local, wg, wu, wd, n_tiles):
    n_blocks = xs_local.shape[0] // LOCAL_BLOCK
    half = D_MODEL // 2
    widx = lambda i, blk, te, cb, nb, nu: (jnp.minimum(te[i], N_EXPERTS - 1), 0, 0)
    grid_spec = pltpu.PrefetchScalarGridSpec(
        num_scalar_prefetch=5,
        grid=(n_tiles,),
        in_specs=[
            pl.BlockSpec(memory_space=pl.ANY),
            pl.BlockSpec((1, D_MODEL, D_EXPERT), widx),
            pl.BlockSpec((1, D_MODEL, D_EXPERT), widx),
            pl.BlockSpec((1, D_EXPERT, D_MODEL), widx),
        ],
        out_specs=pl.BlockSpec(memory_space=pl.ANY),
        scratch_shapes=[
            pltpu.VMEM((2, MOE_TM, half), jnp.uint32), pltpu.VMEM((2, MOE_TM, half), jnp.uint32),
            pltpu.VMEM((D_MODEL, D_EXPERT), BF16), pltpu.VMEM((D_MODEL, D_EXPERT), BF16),
            pltpu.VMEM((D_EXPERT, D_MODEL), BF16),
            pltpu.SemaphoreType.DMA((2,)), pltpu.SemaphoreType.DMA((2,)),
        ],
    )
    return pl.pallas_call(
        _expert_kernel,
        grid_spec=grid_spec,
        out_shape=jax.ShapeDtypeStruct((n_blocks, LOCAL_BLOCK, half), jnp.uint32),
        compiler_params=pltpu.CompilerParams(
            dimension_semantics=("arbitrary",), vmem_limit_bytes=VMEM_LIMIT),
        name="expert_ffn",
    )(block_order, tile_expert, tile_base, tile_blocks, n_used,
      xs_local.reshape(n_blocks, LOCAL_BLOCK, half), wg, wu, wd)


def _combine_kernel(h_ref, wcol_ref, y_ref, o_ref):
    y_lo, y_hi = _unpack_bf16_pairs(y_ref[...])
    w = wcol_ref[...]
    jiota = lax.broadcasted_iota(jnp.int32, (MERGE_TM, LOCAL_ROWS), 1).astype(F32)
    pick0 = jnp.where(jiota == w[:, 2:3], 1.0, 0.0).astype(BF16)
    pick1 = jnp.where(jiota == w[:, 3:4], 1.0, 0.0).astype(BF16)
    half = D_MODEL // 2
    for cols, y in ((slice(0, half), y_lo), (slice(half, D_MODEL), y_hi)):
        o_ref[:, cols] = (h_ref[:, cols]
                          + w[:, 0:1] * jnp.dot(pick0, y, preferred_element_type=F32)
                          + w[:, 1:2] * jnp.dot(pick1, y, preferred_element_type=F32))


def _combine(h, wcol, y_local):
    t = h.shape[0]
    return pl.pallas_call(
        _combine_kernel,
        grid=(t // MERGE_TM,),
        in_specs=[
            pl.BlockSpec((MERGE_TM, D_MODEL), lambda i: (i, 0)),
            pl.BlockSpec((MERGE_TM, ROUTER_LANES), lambda i: (i, 0)),
            pl.BlockSpec((LOCAL_ROWS, D_MODEL // 2), lambda i: (i, 0)),
        ],
        out_specs=pl.BlockSpec((MERGE_TM, D_MODEL), lambda i: (i, 0)),
        out_shape=jax.ShapeDtypeStruct((t, D_MODEL), F32),
        compiler_params=pltpu.CompilerParams(
            dimension_semantics=("parallel",), vmem_limit_bytes=VMEM_LIMIT),
        name="combine",
    )(h, wcol, y_local)


def _dispatch_plan(block_expert):
    n_blocks = block_expert.size
    n_keys = N_EXPERTS + 1
    n_tiles = n_blocks // MOE_BLOCKS + n_keys
    keys = block_expert.reshape(n_blocks)
    order = jnp.argsort(keys, stable=True).astype(jnp.int32)
    counts = jnp.sum(keys[None, :] == jnp.arange(n_keys, dtype=jnp.int32)[:, None], axis=1).astype(jnp.int32)
    tiles_per = (counts + MOE_BLOCKS - 1) // MOE_BLOCKS
    tile_end = jnp.cumsum(tiles_per)
    tile_first = tile_end - tiles_per
    start = jnp.cumsum(counts) - counts
    tile = jnp.arange(n_tiles, dtype=jnp.int32)
    tile_expert = jnp.minimum(jnp.sum(tile_end[None, :] <= tile[:, None], axis=1), n_keys - 1).astype(jnp.int32)
    sel = (tile_expert[:, None] == jnp.arange(n_keys, dtype=jnp.int32)[None, :]).astype(jnp.int32)
    pick = lambda v: jnp.sum(sel * v[None, :], axis=1)
    within = (tile - pick(tile_first)) * MOE_BLOCKS
    tile_base = jnp.clip(pick(start) + within, 0, n_blocks).astype(jnp.int32)
    tile_blocks = jnp.clip(pick(counts) - within, 0, MOE_BLOCKS).astype(jnp.int32)
    n_used = tile_end[-1].astype(jnp.int32).reshape(1)
    order = jnp.concatenate([order, jnp.zeros((MOE_BLOCKS,), jnp.int32)])
    return order, tile_expert, tile_base, tile_blocks, n_used, n_tiles


def _layer(x, norm1_g, w_in, q_norm_a, k_norm_a, sink_a, q_norm_b, k_norm_b, rpb_b, w_up_a, w_up_b, w_out,
           rel_table, norm2_g, w_rg, b_rg, w_re, b_re, w_gate, w_up, w_down):
    batch, seq, d = x.shape
    t = batch * seq
    x2 = x.reshape(t, d)

    cuts = [0, A_WIDTH, A_WIDTH + A_KV_WIDTH, A_WIDTH + 2 * A_KV_WIDTH]
    cuts += [cuts[-1] + B_WIDTH, cuts[-1] + 2 * B_WIDTH, cuts[-1] + 3 * B_WIDTH]
    cuts += [cuts[-1] + D_MODEL, cuts[-1] + 2 * D_MODEL]
    qa, ka, va, qb, kb, vb, ga, gb = [w_in[:, cuts[n]:cuts[n + 1]] for n in range(8)]
    w_cat = jnp.concatenate([qa, qb, kb, vb, ga, gb, ka, va], axis=1).astype(BF16)
    scale = HEAD_DIM ** -0.5
    ones = lambda n: jnp.ones((n,), F32)
    gain_cat = jnp.concatenate([
        jnp.tile(q_norm_a.astype(F32) * scale, A_HEADS), jnp.tile(q_norm_b.astype(F32) * scale, B_HEADS),
        jnp.tile(k_norm_b.astype(F32), B_HEADS), ones(B_WIDTH + 2 * D_MODEL),
        jnp.tile(k_norm_a.astype(F32), A_KV_HEADS), ones(A_KV_WIDTH)]).reshape(1, IN_COLS)

    kind = lambda v, n: jnp.full((n,), v, F32)
    kind_cat = jnp.concatenate([kind(1.0, COL_VB), kind(0.0, B_WIDTH), kind(2.0, 2 * D_MODEL),
                                kind(1.0, A_KV_WIDTH), kind(0.0, A_KV_WIDTH)]).reshape(1, IN_COLS)

    proj = _in_projection(x2, norm1_g.astype(F32).reshape(1, d), w_cat, gain_cat, kind_cat)
    oa = _window_attention(proj, _window_bias(rel_table), sink_a.astype(F32), batch, seq)
    ob = _neighbourhood_attention(proj, _neighbourhood_bias(rpb_b, seq // GRID_W), batch, seq)

    wr = jnp.concatenate([w_re.astype(F32), w_rg.astype(F32)], axis=1)
    wr_hi = wr.astype(BF16)
    wr_lo = (wr - wr_hi.astype(F32)).astype(BF16)
    half = ROUTER_LANES // 2
    padc = lambda a: jnp.pad(a, ((0, 0), (0, half - a.shape[1])))
    wr_cat = jnp.concatenate([padc(wr_hi), padc(wr_lo)], axis=1)
    br = jnp.pad(jnp.concatenate([b_re.astype(F32), b_rg.astype(F32)]), (0, ROUTER_LANES - N_EXPERTS - N_GROUPS))

    h, xs_local, block_expert, wcol = _merge_and_route(
        oa, ob, proj, x2, w_up_a.astype(BF16), w_up_b.astype(BF16), w_out.astype(BF16),
        norm2_g.astype(F32).reshape(1, d), wr_cat, br.reshape(1, ROUTER_LANES))

    order, tile_expert, tile_base, tile_blocks, n_used, n_tiles = _dispatch_plan(
        block_expert[:, 0, :LOCAL_BLOCKS])
    y_local = _expert_ffn(order, tile_expert, tile_base, tile_blocks, n_used, xs_local,
                          w_gate.astype(F32), w_up.astype(F32), w_down.astype(F32), n_tiles)
    out = _combine(h, wcol, y_local.reshape(-1, d // 2))
    return out.reshape(batch, seq, d)


def kernel(x, norm1_g, w_in, q_norm_a, k_norm_a, sink_a, q_norm_b, k_norm_b, rpb_b, w_up_a, w_up_b, w_out,
           rel_bias_table, norm2_g, w_router_group, b_router_group, w_router_expert, b_router_expert,
           w_gate, w_up, w_down):
    return _layer(x, norm1_g[0], w_in[0], q_norm_a[0], k_norm_a[0], sink_a[0], q_norm_b[0], k_norm_b[0],
                  rpb_b[0], w_up_a[0], w_up_b[0], w_out[0], rel_bias_table, norm2_g[0],
                  w_router_group[0], b_router_group[0], w_router_expert[0], b_router_expert[0],
                  w_gate[0], w_up[0], w_down[0])
```

```python
import functools
import math

import jax
import jax.numpy as jnp
from jax import lax
from jax.experimental import pallas as pl
from jax.experimental.pallas import tpu as pltpu

F32 = jnp.float32
BF16 = jnp.bfloat16

D_MODEL = 2048
HEAD_DIM = 128
A_HEADS = 8
A_KV_HEADS = 2
A_GROUP = A_HEADS // A_KV_HEADS
A_WIDTH = A_HEADS * HEAD_DIM
A_KV_WIDTH = A_KV_HEADS * HEAD_DIM
WINDOW = 128
A_BLOCK = 128
NUM_BUCKETS = 32
MAX_DISTANCE = 128
B_HEADS = 8
B_WIDTH = B_HEADS * HEAD_DIM
GRID_W = 64
NB_ROWS = 8
NB_COLS = 16
IN_COLS = A_WIDTH + 2 * A_KV_WIDTH + 3 * B_WIDTH + 2 * D_MODEL
N_GROUPS = 4
EXPERTS_PER_GROUP = 8
N_EXPERTS = N_GROUPS * EXPERTS_PER_GROUP
TOP_K = 2
D_EXPERT = 512
EPS = 1e-6
NEG_INF = -1e30

COL_QA = 0
COL_QB = COL_QA + A_WIDTH
COL_KB = COL_QB + B_WIDTH
COL_VB = COL_KB + B_WIDTH
COL_GA = COL_VB + B_WIDTH
COL_GB = COL_GA + D_MODEL
COL_KA = COL_GB + D_MODEL
COL_VA = COL_KA + A_KV_WIDTH

PROJ_TM = 1024
PROJ_TN = 512
PROJ_SUB = 128
NORM_ROWS = 128
NA_ROWS = 4
NA_BLOCK = NA_ROWS * GRID_W
MERGE_TM = 256
ROUTER_LANES = 128
LOCAL_BLOCK = 8
LOCAL_ROWS = 768
LOCAL_BLOCKS = LOCAL_ROWS // LOCAL_BLOCK
assert LOCAL_ROWS >= TOP_K * MERGE_TM + N_EXPERTS * (LOCAL_BLOCK - 1) and LOCAL_BLOCKS <= ROUTER_LANES
MOE_TM = 256
MOE_BLOCKS = MOE_TM // LOCAL_BLOCK
VMEM_LIMIT = 56 * 1024 * 1024


def _head_norm(a, gain):
    outs = []
    for hd in range(a.shape[1] // HEAD_DIM):
        sl = slice(hd * HEAD_DIM, (hd + 1) * HEAD_DIM)
        ah = a[:, sl]
        ss = jnp.sum(ah * ah, axis=-1, keepdims=True)
        outs.append(ah * lax.rsqrt(ss * (1.0 / HEAD_DIM) + EPS) * gain[:, sl])
    return outs[0] if len(outs) == 1 else jnp.concatenate(outs, axis=1)


def _inproj_kernel(x_ref, g1_ref, w_ref, gain_ref, kind_ref, o_ref, xn_ref):
    j = pl.program_id(1)

    @pl.when(j == 0)
    def _():
        def norm_rows(c, carry):
            rows = pl.ds(pl.multiple_of(c * NORM_ROWS, NORM_ROWS), NORM_ROWS)
            x = x_ref[rows, :]
            ms = jnp.mean(x * x, axis=-1, keepdims=True)
            xn_ref[rows, :] = (x * lax.rsqrt(ms + EPS) * g1_ref[...]).astype(BF16)
            return carry

        lax.fori_loop(0, PROJ_TM // NORM_ROWS, norm_rows, 0)

    kind = kind_ref[...]
    for r in range(PROJ_TM // PROJ_SUB):
        rows = slice(r * PROJ_SUB, (r + 1) * PROJ_SUB)
        acc = jnp.dot(xn_ref[rows, :], w_ref[...], preferred_element_type=F32)
        out = jnp.where(kind == 1.0, _head_norm(acc, gain_ref[...]),
                        jnp.where(kind == 2.0, jax.nn.sigmoid(acc), acc))
        o_ref[rows, :] = out.astype(o_ref.dtype)


def _in_projection(x2, g1, w_cat, gain_cat, kind_cat):
    t = x2.shape[0]
    return pl.pallas_call(
        _inproj_kernel,
        grid=(t // PROJ_TM, IN_COLS // PROJ_TN),
        in_specs=[
            pl.BlockSpec((PROJ_TM, D_MODEL), lambda i, j: (i, 0)),
            pl.BlockSpec((1, D_MODEL), lambda i, j: (0, 0)),
            pl.BlockSpec((D_MODEL, PROJ_TN), lambda i, j: (0, j)),
            pl.BlockSpec((1, PROJ_TN), lambda i, j: (0, j)),
            pl.BlockSpec((1, PROJ_TN), lambda i, j: (0, j)),
        ],
        out_specs=pl.BlockSpec((PROJ_TM, PROJ_TN), lambda i, j: (i, j)),
        out_shape=jax.ShapeDtypeStruct((t, IN_COLS), BF16),
        scratch_shapes=[pltpu.VMEM((PROJ_TM, D_MODEL), BF16)],
        compiler_params=pltpu.CompilerParams(
            dimension_semantics=("parallel", "arbitrary"), vmem_limit_bytes=VMEM_LIMIT),
        name="in_projection",
    )(x2, g1, w_cat, gain_cat, kind_cat)


def _t5_bucket(rel):
    nb = NUM_BUCKETS // 2
    ret = jnp.where(rel > 0, nb, 0)
    n = jnp.abs(rel)
    max_exact = nb // 2
    nf = jnp.maximum(n, 1).astype(jnp.float32)
    large = max_exact + (jnp.log(nf / max_exact) / math.log(MAX_DISTANCE / max_exact)
                         * (nb - max_exact)).astype(jnp.int32)
    large = jnp.minimum(large, nb - 1)
    return ret + jnp.where(n < max_exact, n, large)


def _window_bias(rel_table):
    q_loc = jnp.arange(A_BLOCK)
    k_loc = jnp.arange(3 * A_BLOCK) - WINDOW
    rel = k_loc[None, :] - q_loc[:, None]
    onehot = (_t5_bucket(rel)[..., None] == jnp.arange(NUM_BUCKETS)).astype(F32)
    bias = jnp.einsum('qkn,nh->hqk', onehot, rel_table.astype(F32), precision=lax.Precision.HIGHEST)
    in_window = jnp.abs(rel) <= WINDOW
    has_prev = jnp.array([False, True, True])[:, None, None]
    has_next = jnp.array([True, True, False])[:, None, None]
    k_blk = (jnp.arange(3 * A_BLOCK) // A_BLOCK)[None, None, :]
    valid = in_window[None] & ((k_blk != 0) | has_prev) & ((k_blk != 2) | has_next)
    full = jnp.where(valid[:, None], bias[None], NEG_INF)
    return full.reshape(3, A_KV_HEADS, A_GROUP * A_BLOCK, 3 * A_BLOCK)


def _window_kernel(sink_ref, q_ref, kp_ref, ko_ref, kn_ref, vp_ref, vo_ref, vn_ref, bias_ref, o_ref):
    for kvh in range(A_KV_HEADS):
        hs = slice(kvh * HEAD_DIM, (kvh + 1) * HEAD_DIM)
        heads = [kvh * A_GROUP + g for g in range(A_GROUP)]
        q4 = jnp.concatenate([q_ref[:, h * HEAD_DIM:(h + 1) * HEAD_DIM] for h in heads], axis=0)
        kc = jnp.concatenate([kp_ref[:, hs], ko_ref[:, hs], kn_ref[:, hs]], axis=0)
        vc = jnp.concatenate([vp_ref[:, hs], vo_ref[:, hs], vn_ref[:, hs]], axis=0)
        s = lax.dot_general(q4, kc, (((1,), (1,)), ((), ())), preferred_element_type=F32)
        s = s + bias_ref[0, kvh]
        m = jnp.max(s, axis=-1, keepdims=True)
        p = jnp.exp((s - m).astype(BF16))
        sink = jnp.concatenate([jnp.full((A_BLOCK, 1), sink_ref[h], F32) for h in heads], axis=0)
        ov = jnp.dot(p, jnp.concatenate([vc, jnp.ones_like(vc)], axis=1), preferred_element_type=F32)
        o = ov[:, :HEAD_DIM] / (ov[:, HEAD_DIM:] + jnp.exp(sink - m))
        for g, h in enumerate(heads):
            o_ref[:, h * HEAD_DIM:(h + 1) * HEAD_DIM] = o[g * A_BLOCK:(g + 1) * A_BLOCK].astype(o_ref.dtype)


def _window_attention(proj, bias_a, sink, batch, seq):
    nb = seq // A_BLOCK

    def cls(n):
        return jnp.where(n == 0, 0, jnp.where(n == nb - 1, 2, 1))

    def row(b, n):
        return b * nb + n

    kcol, vcol = COL_KA // A_KV_WIDTH, COL_VA // A_KV_WIDTH
    prev = lambda b, n: row(b, jnp.maximum(n - 1, 0))
    nxt = lambda b, n: row(b, jnp.minimum(n + 1, nb - 1))
    kv_spec = lambda rfn, col: pl.BlockSpec((A_BLOCK, A_KV_WIDTH), lambda b, n: (rfn(b, n), col))
    return pl.pallas_call(
        _window_kernel,
        grid=(batch, nb),
        in_specs=[
            pl.BlockSpec(memory_space=pltpu.SMEM),
            pl.BlockSpec((A_BLOCK, A_WIDTH), lambda b, n: (row(b, n), COL_QA // A_WIDTH)),
            kv_spec(prev, kcol), kv_spec(row, kcol), kv_spec(nxt, kcol),
            kv_spec(prev, vcol), kv_spec(row, vcol), kv_spec(nxt, vcol),
            pl.BlockSpec((1, A_KV_HEADS, A_GROUP * A_BLOCK, 3 * A_BLOCK), lambda b, n: (cls(n), 0, 0, 0)),
        ],
        out_specs=pl.BlockSpec((A_BLOCK, A_WIDTH), lambda b, n: (row(b, n), 0)),
        out_shape=jax.ShapeDtypeStruct((batch * seq, A_WIDTH), BF16),
        compiler_params=pltpu.CompilerParams(
            dimension_semantics=("parallel", "arbitrary"), vmem_limit_bytes=VMEM_LIMIT),
        name="window_attention",
    )(sink, proj, proj, proj, proj, proj, proj, proj, bias_a)


def _neighbourhood_bias(rpb, rows):
    kh = min(NB_ROWS, rows)
    nblk = rows // NA_ROWS
    r0 = jnp.array([0, NA_ROWS, (nblk - 1) * NA_ROWS])[:, None]
    r = r0 + jnp.arange(NA_ROWS)[None, :]
    rs = jnp.clip(r - kh // 2, 0, rows - kh)
    key_r = (r0 - NA_ROWS) + jnp.arange(3 * NA_ROWS)[None, :]
    row_ok = (key_r[:, None, :] >= rs[:, :, None]) & (key_r[:, None, :] < rs[:, :, None] + kh)
    dr = key_r[:, None, :] - r[:, :, None] + (NB_ROWS - 1)
    cols = jnp.arange(GRID_W)
    cs = jnp.clip(cols - NB_COLS // 2, 0, GRID_W - NB_COLS)
    col_ok = (cols[None, :] >= cs[:, None]) & (cols[None, :] < cs[:, None] + NB_COLS)
    dc = cols[None, :] - cols[:, None] + (NB_COLS - 1)
    row_sel = ((dr[..., None] == jnp.arange(2 * NB_ROWS - 1)) & row_ok[..., None]).astype(F32)
    col_sel = ((dc[..., None] == jnp.arange(2 * NB_COLS - 1)) & col_ok[..., None]).astype(F32)
    bias = jnp.einsum('hde,aikd,cje->ahickj', rpb.astype(F32), row_sel, col_sel,
                      precision=lax.Precision.HIGHEST)
    ok = row_ok[:, None, :, None, :, None] & col_ok[None, None, None, :, None, :]
    full = jnp.where(ok, bias, NEG_INF)
    return full.reshape(3, B_HEADS, NA_BLOCK, 3 * NA_BLOCK)


def _neighbourhood_kernel(q_ref, kp_ref, ko_ref, kn_ref, vp_ref, vo_ref, vn_ref, bias_ref, o_ref):
    for h in range(B_HEADS):
        hs = slice(h * HEAD_DIM, (h + 1) * HEAD_DIM)
        kc = jnp.concatenate([kp_ref[:, hs], ko_ref[:, hs], kn_ref[:, hs]], axis=0)
        vc = jnp.concatenate([vp_ref[:, hs], vo_ref[:, hs], vn_ref[:, hs]], axis=0)
        s = lax.dot_general(q_ref[:, hs], kc, (((1,), (1,)), ((), ())), preferred_element_type=F32)
        s = s + bias_ref[0, h]
        m = jnp.max(s, axis=-1, keepdims=True)
        p = jnp.exp((s - m).astype(BF16))
        ov = jnp.dot(p, jnp.concatenate([vc, jnp.ones_like(vc)], axis=1), preferred_element_type=F32)
        o_ref[:, hs] = (ov[:, :HEAD_DIM] / ov[:, HEAD_DIM:]).astype(o_ref.dtype)


def _neighbourhood_attention(proj, bias_b, batch, seq):
    nb = seq // NA_BLOCK

    def cls(n):
        return jnp.where(n == 0, 0, jnp.where(n == nb - 1, 2, 1))

    def row(b, n):
        return b * nb + n

    prev = lambda b, n: row(b, jnp.maximum(n - 1, 0))
    nxt = lambda b, n: row(b, jnp.minimum(n + 1, nb - 1))
    spec = lambda rfn, col: pl.BlockSpec((NA_BLOCK, B_WIDTH), lambda b, n: (rfn(b, n), col))
    qcol, kcol, vcol = COL_QB // B_WIDTH, COL_KB // B_WIDTH, COL_VB // B_WIDTH
    return pl.pallas_call(
        _neighbourhood_kernel,
        grid=(batch, nb),
        in_specs=[
            spec(row, qcol),
            spec(prev, kcol), spec(row, kcol), spec(nxt, kcol),
            spec(prev, vcol), spec(row, vcol), spec(nxt, vcol),
            pl.BlockSpec((1, B_HEADS, NA_BLOCK, 3 * NA_BLOCK), lambda b, n: (cls(n), 0, 0, 0)),
        ],
        out_specs=pl.BlockSpec((NA_BLOCK, B_WIDTH), lambda b, n: (row(b, n), 0)),
        out_shape=jax.ShapeDtypeStruct((batch * seq, B_WIDTH), BF16),
        compiler_params=pltpu.CompilerParams(
            dimension_semantics=("parallel", "arbitrary"), vmem_limit_bytes=VMEM_LIMIT),
        name="neighbourhood_attention",
    )(proj, proj, proj, proj, proj, proj, proj, bias_b)


def _pack_bf16_pairs(a):
    n = a.shape[1] // 2
    bits = lax.bitcast_convert_type(a, jnp.uint32)
    return bits[:, n:] | (bits[:, :n] >> 16)


def _unpack_bf16_pairs(p):
    lo = lax.bitcast_convert_type(p << 16, F32).astype(BF16)
    hi = lax.bitcast_convert_type(p & jnp.uint32(0xFFFF0000), F32).astype(BF16)
    return lo, hi


def _merge_kernel(oa_ref, ob_ref, ga_ref, gb_ref, x_ref, wa_ref, wb_ref, wo_ref, g2_ref, wr_ref, br_ref,
                  upper_ref, lower_ref, h_ref, xs_ref, eb_ref, wcol_ref, hn_scr):
    s = pl.program_id(0)
    cur = lax.rem(s, 2)
    prev = 1 - cur
    tm = MERGE_TM

    @pl.when(s == 0)
    def _():
        hn_scr[...] = jnp.zeros_like(hn_scr)

    hn_split = hn_scr[prev]
    logits = _router_logits(hn_split, wr_ref, br_ref)
    ua = jnp.dot(oa_ref[...], wa_ref[...], preferred_element_type=F32)
    ub = jnp.dot(ob_ref[...], wb_ref[...], preferred_element_type=F32)
    ids, w1, w2 = _route(logits)
    lrow, eb = _local_sort(ids, upper_ref, lower_ref)
    merged = ga_ref[...].astype(F32) * ua + gb_ref[...].astype(F32) * ub
    acc = jnp.dot(merged.astype(BF16), wo_ref[...], preferred_element_type=F32)
    xs = _place(lrow, hn_split[:tm])
    h = x_ref[...] + acc
    h_ref[...] = h
    ms = jnp.mean(h * h, axis=-1, keepdims=True)
    hn = h * lax.rsqrt(ms + EPS) * g2_ref[...]
    hn_hi = hn.astype(BF16)
    hn_scr[cur, :tm, :] = hn_hi
    hn_scr[cur, tm:, :] = (hn - hn_hi.astype(F32)).astype(BF16)

    xs_ref[...] = _pack_bf16_pairs(xs)
    eb_ref[0] = eb
    riota = lax.broadcasted_iota(jnp.int32, (ROUTER_LANES, tm), 0)
    wrows = jnp.where(riota == 0, w1, jnp.where(riota == 1, w2,
                      jnp.where(riota == 2, lrow[:, :tm], jnp.where(riota == 3, lrow[:, tm:], 0.0))))
    wcol_ref[...] = wrows.T


def _router_logits(hn_split, wr_ref, br_ref):
    tm = hn_split.shape[0] // 2
    lg = jnp.dot(hn_split, wr_ref[...], preferred_element_type=F32)
    lg_hi, lg_lo = lg[:tm], lg[tm:]
    return lg_hi + pltpu.roll(lg_hi, ROUTER_LANES // 2, axis=1) + lg_lo + br_ref[...]


def _route(logits):
    tm = logits.shape[0]
    lt = logits.T
    gl = lt[N_EXPERTS:N_EXPERTS + N_GROUPS]
    gmax = jnp.max(gl, axis=0, keepdims=True)
    giota = lax.broadcasted_iota(jnp.int32, gl.shape, 0)
    g_idx = jnp.min(jnp.where(gl == gmax, giota, N_GROUPS), axis=0, keepdims=True)
    p_g = 1.0 / jnp.sum(jnp.exp(gl - gmax), axis=0, keepdims=True)
    e_sel = jnp.zeros((EXPERTS_PER_GROUP, tm), F32)
    for g in range(N_GROUPS):
        e_sel = jnp.where(g_idx == g, lt[g * EXPERTS_PER_GROUP:(g + 1) * EXPERTS_PER_GROUP], e_sel)
    eiota = lax.broadcasted_iota(jnp.int32, e_sel.shape, 0)
    m1 = jnp.max(e_sel, axis=0, keepdims=True)
    i1 = jnp.min(jnp.where(e_sel == m1, eiota, EXPERTS_PER_GROUP), axis=0, keepdims=True)
    rest = jnp.where(eiota == i1, -jnp.inf, e_sel)
    m2 = jnp.max(rest, axis=0, keepdims=True)
    i2 = jnp.min(jnp.where(rest == m2, eiota, EXPERTS_PER_GROUP), axis=0, keepdims=True)
    t = jnp.exp(m2 - m1)
    ids = jnp.concatenate([g_idx * EXPERTS_PER_GROUP + i1, g_idx * EXPERTS_PER_GROUP + i2], axis=1)
    return ids, p_g / (1.0 + t), p_g * t / (1.0 + t)


def _local_sort(ids, upper_ref, lower_ref):
    eiota32 = lax.broadcasted_iota(jnp.int32, (N_EXPERTS, ids.shape[1]), 0)
    onehot = jnp.where(eiota32 == ids, 1.0, 0.0)
    count = jnp.sum(onehot, axis=1, keepdims=True)
    before = jnp.dot(onehot.astype(BF16), upper_ref[...], preferred_element_type=F32)
    nblk = jnp.floor((count + (LOCAL_BLOCK - 1)) * (1.0 / LOCAL_BLOCK))
    boff = jnp.dot(lower_ref[...], jnp.broadcast_to(nblk, (N_EXPERTS, ROUTER_LANES)).astype(BF16),
                   preferred_element_type=F32)[:, 0:1]
    lrow = jnp.sum(onehot * (before + LOCAL_BLOCK * boff), axis=0, keepdims=True)
    biota = lax.broadcasted_iota(jnp.int32, (N_EXPERTS, ROUTER_LANES), 1).astype(F32)
    owns = jnp.where(jnp.logical_and(biota >= boff, biota < boff + nblk), 1.0, 0.0)
    eiota_f = lax.broadcasted_iota(jnp.int32, (N_EXPERTS, ROUTER_LANES), 0).astype(F32)
    eb = jnp.sum(owns * eiota_f, axis=0, keepdims=True) + N_EXPERTS * (1.0 - jnp.sum(owns, axis=0, keepdims=True))
    return lrow, eb.astype(jnp.int32)


def _place(lrow, hn_hi):
    tm = hn_hi.shape[0]
    jiota = lax.broadcasted_iota(jnp.int32, (LOCAL_ROWS, tm), 0).astype(F32)
    place = jnp.where(jiota == lrow[:, :tm], 1.0, jnp.where(jiota == lrow[:, tm:], 1.0, 0.0)).astype(BF16)
    return jnp.dot(place, hn_hi, preferred_element_type=F32)


def _merge_and_route(oa, ob, proj, x2, wa, wb, wo, g2, wr_cat, br):
    t = x2.shape[0]
    n_assign = TOP_K * MERGE_TM
    upper = (jnp.arange(n_assign)[:, None] < jnp.arange(n_assign)[None, :]).astype(BF16)
    lower = (jnp.arange(N_EXPERTS)[:, None] > jnp.arange(N_EXPERTS)[None, :]).astype(BF16)
    const = lambda shape: pl.BlockSpec(shape, lambda s: (0, 0), pipeline_mode=pl.Buffered(1))
    n_tiles = t // MERGE_TM
    cur = lambda s: jnp.minimum(s, n_tiles - 1)
    prev = lambda s: jnp.maximum(s - 1, 0)
    return pl.pallas_call(
        _merge_kernel,
        grid=(n_tiles + 1,),
        in_specs=[
            pl.BlockSpec((MERGE_TM, A_WIDTH), lambda s: (cur(s), 0)),
            pl.BlockSpec((MERGE_TM, B_WIDTH), lambda s: (cur(s), 0)),
            pl.BlockSpec((MERGE_TM, D_MODEL), lambda s: (cur(s), COL_GA // D_MODEL)),
            pl.BlockSpec((MERGE_TM, D_MODEL), lambda s: (cur(s), COL_GB // D_MODEL)),
            pl.BlockSpec((MERGE_TM, D_MODEL), lambda s: (cur(s), 0)),
            const((A_WIDTH, D_MODEL)), const((B_WIDTH, D_MODEL)), const((D_MODEL, D_MODEL)),
            const((1, D_MODEL)), const((D_MODEL, ROUTER_LANES)), const((1, ROUTER_LANES)),
            const((n_assign, n_assign)), const((N_EXPERTS, N_EXPERTS)),
        ],
        out_specs=[
            pl.BlockSpec((MERGE_TM, D_MODEL), lambda s: (cur(s), 0)),
            pl.BlockSpec((LOCAL_ROWS, D_MODEL // 2), lambda s: (prev(s), 0)),
            pl.BlockSpec((1, 1, ROUTER_LANES), lambda s: (prev(s), 0, 0)),
            pl.BlockSpec((MERGE_TM, ROUTER_LANES), lambda s: (prev(s), 0)),
        ],
        scratch_shapes=[pltpu.VMEM((2, 2 * MERGE_TM, D_MODEL), BF16)],
        out_shape=[
            jax.ShapeDtypeStruct((t, D_MODEL), F32),
            jax.ShapeDtypeStruct((t // MERGE_TM * LOCAL_ROWS, D_MODEL // 2), jnp.uint32),
            jax.ShapeDtypeStruct((t // MERGE_TM, 1, ROUTER_LANES), jnp.int32),
            jax.ShapeDtypeStruct((t, ROUTER_LANES), F32),
        ],
        compiler_params=pltpu.CompilerParams(
            dimension_semantics=("arbitrary",), vmem_limit_bytes=VMEM_LIMIT),
        name="merge_and_route",
    )(oa, ob, proj, proj, x2, wa, wb, wo, g2, wr_cat, br, upper, lower)


def _expert_kernel(blk_ref, texp_ref, tcb_ref, tnb_ref, nused_ref, xs_hbm, wg_ref, wu_ref, wd_ref, y_hbm,
                   xbuf, ybuf, wgb, wub, wdb, gsem, ssem):
    i = pl.program_id(0)
    n_used = nused_ref[0]
    slot = lax.rem(i, 2)
    expert = texp_ref[i]

    def gather_copy(blk, dst_slot, j):
        return pltpu.make_async_copy(xs_hbm.at[blk], xbuf.at[dst_slot, pl.ds(j * LOCAL_BLOCK, LOCAL_BLOCK)],
                                     gsem.at[dst_slot])

    def scatter_copy(blk, src_slot, j):
        return pltpu.make_async_copy(ybuf.at[src_slot, pl.ds(j * LOCAL_BLOCK, LOCAL_BLOCK)], y_hbm.at[blk],
                                     ssem.at[src_slot])

    def issue_gather(tile, dst_slot):
        base = tcb_ref[tile]
        for j in range(MOE_BLOCKS):
            gather_copy(blk_ref[base + j], dst_slot, j).start()

    def wait_gather(s):
        for j in range(MOE_BLOCKS):
            gather_copy(0, s, j).wait()

    def for_valid_blocks(tile, fn):
        n_valid = tnb_ref[tile]

        @pl.when(n_valid == MOE_BLOCKS)
        def _():
            for j in range(MOE_BLOCKS):
                fn(j)

        @pl.when(n_valid < MOE_BLOCKS)
        def _():
            for j in range(MOE_BLOCKS):
                pl.when(j < n_valid)(functools.partial(fn, j))

    def issue_scatter(tile, src_slot):
        base = tcb_ref[tile]
        for_valid_blocks(tile, lambda j: scatter_copy(blk_ref[base + j], src_slot, j).start())

    def wait_scatter(tile, s):
        for_valid_blocks(tile, lambda j: scatter_copy(0, s, j).wait())

    @pl.when(jnp.logical_and(i == 0, n_used > 0))
    def _():
        issue_gather(0, 0)

    @pl.when(i < n_used)
    def _():
        wait_gather(slot)

        @pl.when(i + 1 < n_used)
        def _():
            issue_gather(i + 1, 1 - slot)

        @pl.when(i >= 2)
        def _():
            wait_scatter(i - 2, slot)

        @pl.when(jnp.logical_and(expert < N_EXPERTS, jnp.logical_or(i == 0, expert != texp_ref[jnp.maximum(i - 1, 0)])))
        def _():
            wgb[...] = wg_ref[0].astype(BF16)
            wub[...] = wu_ref[0].astype(BF16)
            wdb[...] = wd_ref[0].astype(BF16)

        @pl.when(expert < N_EXPERTS)
        def _():
            x_lo, x_hi = _unpack_bf16_pairs(xbuf[slot])
            half = D_MODEL // 2
            g = (jnp.dot(x_lo, wgb[:half, :], preferred_element_type=F32)
                 + jnp.dot(x_hi, wgb[half:, :], preferred_element_type=F32))
            u = (jnp.dot(x_lo, wub[:half, :], preferred_element_type=F32)
                 + jnp.dot(x_hi, wub[half:, :], preferred_element_type=F32))
            act = (g * jax.nn.sigmoid(g) * u).astype(BF16)
            y = jnp.dot(act, wdb[...], preferred_element_type=F32)
            ybuf[slot] = _pack_bf16_pairs(y.astype(BF16).astype(F32))

        @pl.when(expert >= N_EXPERTS)
        def _():
            ybuf[slot] = jnp.zeros((MOE_TM, D_MODEL // 2), jnp.uint32)

        issue_scatter(i, slot)

        @pl.when(i == n_used - 1)
        def _():
            wait_scatter(i, slot)

            @pl.when(i >= 1)
            def _():
                wait_scatter(i - 1, 1 - slot)


def _expert_ffn(block_order, tile_expert, tile_base, tile_blocks, n_used, xs_local, wg, wu, wd, n_tiles):
    n_blocks = xs_local.shape[0] // LOCAL_BLOCK
    half = D_MODEL // 2
    widx = lambda i, blk, te, cb, nb, nu: (jnp.minimum(te[i], N_EXPERTS - 1), 0, 0)
    grid_spec = pltpu.PrefetchScalarGridSpec(
        num_scalar_prefetch=5,
        grid=(n_tiles,),
        in_specs=[
            pl.BlockSpec(memory_space=pl.ANY),
            pl.BlockSpec((1, D_MODEL, D_EXPERT), widx),
            pl.BlockSpec((1, D_MODEL, D_EXPERT), widx),
            pl.BlockSpec((1, D_EXPERT, D_MODEL), widx),
        ],
        out_specs=pl.BlockSpec(memory_space=pl.ANY),
        scratch_shapes=[
            pltpu.VMEM((2, MOE_TM, half), jnp.uint32), pltpu.VMEM((2, MOE_TM, half), jnp.uint32),
            pltpu.VMEM((D_MODEL, D_EXPERT), BF16), pltpu.VMEM((D_MODEL, D_EXPERT), BF16),
            pltpu.VMEM((D_EXPERT, D_MODEL), BF16),
            pltpu.SemaphoreType.DMA((2,)), pltpu.SemaphoreType.DMA((2,)),
        ],
    )
    return pl.pallas_call(
        _expert_kernel,
        grid_spec=grid_spec,
        out_shape=jax.ShapeDtypeStruct((n_blocks, LOCAL_BLOCK, half), jnp.uint32),
        compiler_params=pltpu.CompilerParams(
            dimension_semantics=("arbitrary",), vmem_limit_bytes=VMEM_LIMIT),
        name="expert_ffn",
    )(block_order, tile_expert, tile_base, tile_blocks, n_used,
      xs_local.reshape(n_blocks, LOCAL_BLOCK, half), wg, wu, wd)


def _combine_kernel(h_ref, wcol_ref, y_ref, o_ref):
    y_lo, y_hi = _unpack_bf16_pairs(y_ref[...])
    w = wcol_ref[...]
    jiota = lax.broadcasted_iota(jnp.int32, (MERGE_TM, LOCAL_ROWS), 1).astype(F32)
    pick0 = jnp.where(jiota == w[:, 2:3], 1.0, 0.0).astype(BF16)
    pick1 = jnp.where(jiota == w[:, 3:4], 1.0, 0.0).astype(BF16)
    half = D_MODEL // 2
    for cols, y in ((slice(0, half), y_lo), (slice(half, D_MODEL), y_hi)):
        o_ref[:, cols] = (h_ref[:, cols]
                          + w[:, 0:1] * jnp.dot(pick0, y, preferred_element_type=F32)
                          + w[:, 1:2] * jnp.dot(pick1, y, preferred_element_type=F32))


def _combine(h, wcol, y_local):
    t = h.shape[0]
    return pl.pallas_call(
        _combine_kernel,
        grid=(t // MERGE_TM,),
        in_specs=[
            pl.BlockSpec((MERGE_TM, D_MODEL), lambda i: (i, 0)),
            pl.BlockSpec((MERGE_TM, ROUTER_LANES), lambda i: (i, 0)),
            pl.BlockSpec((LOCAL_ROWS, D_MODEL // 2), lambda i: (i, 0)),
        ],
        out_specs=pl.BlockSpec((MERGE_TM, D_MODEL), lambda i: (i, 0)),
        out_shape=jax.ShapeDtypeStruct((t, D_MODEL), F32),
        compiler_params=pltpu.CompilerParams(
            dimension_semantics=("parallel",), vmem_limit_bytes=VMEM_LIMIT),
        name="combine",
    )(h, wcol, y_local)


def _dispatch_plan(block_expert):
    n_blocks = block_expert.size
    n_keys = N_EXPERTS + 1
    n_tiles = n_blocks // MOE_BLOCKS + n_keys
    keys = block_expert.reshape(n_blocks)
    order = jnp.argsort(keys, stable=True).astype(jnp.int32)
    counts = jnp.sum(keys[None, :] == jnp.arange(n_keys, dtype=jnp.int32)[:, None], axis=1).astype(jnp.int32)
    tiles_per = (counts + MOE_BLOCKS - 1) // MOE_BLOCKS
    tile_end = jnp.cumsum(tiles_per)
    tile_first = tile_end - tiles_per
    start = jnp.cumsum(counts) - counts
    tile = jnp.arange(n_tiles, dtype=jnp.int32)
    tile_expert = jnp.minimum(jnp.sum(tile_end[None, :] <= tile[:, None], axis=1), n_keys - 1).astype(jnp.int32)
    sel = (tile_expert[:, None] == jnp.arange(n_keys, dtype=jnp.int32)[None, :]).astype(jnp.int32)
    pick = lambda v: jnp.sum(sel * v[None, :], axis=1)
    within = (tile - pick(tile_first)) * MOE_BLOCKS
    tile_base = jnp.clip(pick(start) + within, 0, n_blocks).astype(jnp.int32)
    tile_blocks = jnp.clip(pick(counts) - within, 0, MOE_BLOCKS).astype(jnp.int32)
    n_used = tile_end[-1].astype(jnp.int32).reshape(1)
    order = jnp.concatenate([order, jnp.zeros((MOE_BLOCKS,), jnp.int32)])
    return order, tile_expert, tile_base, tile_blocks, n_used, n_tiles


def _layer(x, norm1_g, w_in, q_norm_a, k_norm_a, sink_a, q_norm_b, k_norm_b, rpb_b, w_up_a, w_up_b, w_out,
           rel_table, norm2_g, w_rg, b_rg, w_re, b_re, w_gate, w_up, w_down):
    batch, seq, d = x.shape
    t = batch * seq
    x2 = x.reshape(t, d)

    cuts = [0, A_WIDTH, A_WIDTH + A_KV_WIDTH, A_WIDTH + 2 * A_KV_WIDTH]
    cuts += [cuts[-1] + B_WIDTH, cuts[-1] + 2 * B_WIDTH, cuts[-1] + 3 * B_WIDTH]
    cuts += [cuts[-1] + D_MODEL, cuts[-1] + 2 * D_MODEL]
    qa, ka, va, qb, kb, vb, ga, gb = [w_in[:, cuts[n]:cuts[n + 1]] for n in range(8)]
    w_cat = jnp.concatenate([qa, qb, kb, vb, ga, gb, ka, va], axis=1).astype(BF16)
    scale = HEAD_DIM ** -0.5
    ones = lambda n: jnp.ones((n,), F32)
    gain_cat = jnp.concatenate([
        jnp.tile(q_norm_a.astype(F32) * scale, A_HEADS), jnp.tile(q_norm_b.astype(F32) * scale, B_HEADS),
        jnp.tile(k_norm_b.astype(F32), B_HEADS), ones(B_WIDTH + 2 * D_MODEL),
        jnp.tile(k_norm_a.astype(F32), A_KV_HEADS), ones(A_KV_WIDTH)]).reshape(1, IN_COLS)

    kind = lambda v, n: jnp.full((n,), v, F32)
    kind_cat = jnp.concatenate([kind(1.0, COL_VB), kind(0.0, B_WIDTH), kind(2.0, 2 * D_MODEL),
                                kind(1.0, A_KV_WIDTH), kind(0.0, A_KV_WIDTH)]).reshape(1, IN_COLS)

    proj = _in_projection(x2, norm1_g.astype(F32).reshape(1, d), w_cat, gain_cat, kind_cat)
    oa = _window_attention(proj, _window_bias(rel_table), sink_a.astype(F32), batch, seq)
    ob = _neighbourhood_attention(proj, _neighbourhood_bias(rpb_b, seq // GRID_W), batch, seq)

    wr = jnp.concatenate([w_re.astype(F32), w_rg.astype(F32)], axis=1)
    wr_hi = wr.astype(BF16)
    wr_lo = (wr - wr_hi.astype(F32)).astype(BF16)
    half = ROUTER_LANES // 2
    padc = lambda a: jnp.pad(a, ((0, 0), (0, half - a.shape[1])))
    wr_cat = jnp.concatenate([padc(wr_hi), padc(wr_lo)], axis=1)
    br = jnp.pad(jnp.concatenate([b_re.astype(F32), b_rg.astype(F32)]), (0, ROUTER_LANES - N_EXPERTS - N_GROUPS))

    h, xs_local, block_expert, wcol = _merge_and_route(
        oa, ob, proj, x2, w_up_a.astype(BF16), w_up_b.astype(BF16), w_out.astype(BF16),
        norm2_g.astype(F32).reshape(1, d), wr_cat, br.reshape(1, ROUTER_LANES))

    order, tile_expert, tile_base, tile_blocks, n_used, n_tiles = _dispatch_plan(
        block_expert[:, 0, :LOCAL_BLOCKS])
    y_local = _expert_ffn(order, tile_expert, tile_base, tile_blocks, n_used, xs_local,
                          w_gate.astype(F32), w_up.astype(F32), w_down.astype(F32), n_tiles)
    out = _combine(h, wcol, y_local.reshape(-1, d // 2))
    return out.reshape(batch, seq, d)


def kernel(x, norm1_g, w_in, q_norm_a, k_norm_a, sink_a, q_norm_b, k_norm_b, rpb_b, w_up_a, w_up_b, w_out,
           rel_bias_table, norm2_g, w_router_group, b_router_group, w_router_expert, b_router_expert,
           w_gate, w_up, w_down):
    return _layer(x, norm1_g[0], w_in[0], q_norm_a[0], k_norm_a[0], sink_a[0], q_norm_b[0], k_norm_b[0],
                  rpb_b[0], w_up_a[0], w_up_b[0], w_out[0], rel_bias_table, norm2_g[0],
                  w_router_group[0], b_router_group[0], w_router_expert[0], b_router_expert[0],
                  w_gate[0], w_up[0], w_down[0])
```

```python
import functools
import math

import jax
import jax.numpy as jnp
from jax import lax
from jax.experimental import pallas as pl
from jax.experimental.pallas import tpu as pltpu

F32 = jnp.float32
BF16 = jnp.bfloat16

D_MODEL = 2048
HEAD_DIM = 128
A_HEADS = 8
A_KV_HEADS = 2
A_GROUP = A_HEADS // A_KV_HEADS
A_WIDTH = A_HEADS * HEAD_DIM
A_KV_WIDTH = A_KV_HEADS * HEAD_DIM
WINDOW = 128
A_BLOCK = 128
NUM_BUCKETS = 32
MAX_DISTANCE = 128
B_HEADS = 8
B_WIDTH = B_HEADS * HEAD_DIM
GRID_W = 64
NB_ROWS = 8
NB_COLS = 16
IN_COLS = A_WIDTH + 2 * A_KV_WIDTH + 3 * B_WIDTH + 2 * D_MODEL
N_GROUPS = 4
EXPERTS_PER_GROUP = 8
N_EXPERTS = N_GROUPS * EXPERTS_PER_GROUP
TOP_K = 2
D_EXPERT = 512
EPS = 1e-6
NEG_INF = -1e30

COL_QA = 0
COL_QB = COL_QA + A_WIDTH
COL_KB = COL_QB + B_WIDTH
COL_VB = COL_KB + B_WIDTH
COL_GA = COL_VB + B_WIDTH
COL_GB = COL_GA + D_MODEL
COL_KA = COL_GB + D_MODEL
COL_VA = COL_KA + A_KV_WIDTH

PROJ_TM = 1024
PROJ_TN = 512
PROJ_SUB = 128
WIN_STEP = 4
NA_ROWS = 4
NA_BLOCK = NA_ROWS * GRID_W
MERGE_TM = 256
ROUTER_LANES = 128
LOCAL_BLOCK = 8
LOCAL_ROWS = 768
LOCAL_BLOCKS = LOCAL_ROWS // LOCAL_BLOCK
assert LOCAL_ROWS >= TOP_K * MERGE_TM + N_EXPERTS * (LOCAL_BLOCK - 1) and LOCAL_BLOCKS <= ROUTER_LANES
MOE_TM = 256
MOE_BLOCKS = MOE_TM // LOCAL_BLOCK
VMEM_LIMIT = 56 * 1024 * 1024


def _head_norm(a, gain):
    outs = []
    for hd in range(a.shape[1] // HEAD_DIM):
        sl = slice(hd * HEAD_DIM, (hd + 1) * HEAD_DIM)
        ah = a[:, sl]
        ss = jnp.sum(ah * ah, axis=-1, keepdims=True)
        outs.append(ah * lax.rsqrt(ss * (1.0 / HEAD_DIM) + EPS) * gain[:, sl])
    return outs[0] if len(outs) == 1 else jnp.concatenate(outs, axis=1)


def _inproj_kernel(x_ref, g1_ref, w_ref, gain_ref, kind_ref, o_ref, xn_ref):
    j = pl.program_id(1)

    def project(normalise_first):
        kind = kind_ref[...]
        for r in range(PROJ_TM // PROJ_SUB):
            rows = slice(r * PROJ_SUB, (r + 1) * PROJ_SUB)
            if normalise_first:
                x = x_ref[rows, :]
                ms = jnp.mean(x * x, axis=-1, keepdims=True)
                xn_ref[rows, :] = (x * lax.rsqrt(ms + EPS) * g1_ref[...]).astype(BF16)
            acc = jnp.dot(xn_ref[rows, :], w_ref[...], preferred_element_type=F32)
            out = jnp.where(kind == 1.0, _head_norm(acc, gain_ref[...]),
                            jnp.where(kind == 2.0, jax.nn.sigmoid(acc), acc))
            o_ref[rows, :] = out.astype(o_ref.dtype)

    pl.when(j == 0)(functools.partial(project, True))
    pl.when(j != 0)(functools.partial(project, False))


def _in_projection(x2, g1, w_cat, gain_cat, kind_cat):
    t = x2.shape[0]
    return pl.pallas_call(
        _inproj_kernel,
        grid=(t // PROJ_TM, IN_COLS // PROJ_TN),
        in_specs=[
            pl.BlockSpec((PROJ_TM, D_MODEL), lambda i, j: (i, 0)),
            pl.BlockSpec((1, D_MODEL), lambda i, j: (0, 0)),
            pl.BlockSpec((D_MODEL, PROJ_TN), lambda i, j: (0, j)),
            pl.BlockSpec((1, PROJ_TN), lambda i, j: (0, j)),
            pl.BlockSpec((1, PROJ_TN), lambda i, j: (0, j)),
        ],
        out_specs=pl.BlockSpec((PROJ_TM, PROJ_TN), lambda i, j: (i, j)),
        out_shape=jax.ShapeDtypeStruct((t, IN_COLS), BF16),
        scratch_shapes=[pltpu.VMEM((PROJ_TM, D_MODEL), BF16)],
        compiler_params=pltpu.CompilerParams(
            dimension_semantics=("parallel", "arbitrary"), vmem_limit_bytes=VMEM_LIMIT),
        name="in_projection",
    )(x2, g1, w_cat, gain_cat, kind_cat)


def _t5_bucket(rel):
    nb = NUM_BUCKETS // 2
    ret = jnp.where(rel > 0, nb, 0)
    n = jnp.abs(rel)
    max_exact = nb // 2
    nf = jnp.maximum(n, 1).astype(jnp.float32)
    large = max_exact + (jnp.log(nf / max_exact) / math.log(MAX_DISTANCE / max_exact)
                         * (nb - max_exact)).astype(jnp.int32)
    large = jnp.minimum(large, nb - 1)
    return ret + jnp.where(n < max_exact, n, large)


def _window_bias(rel_table):
    q_loc = jnp.arange(A_BLOCK)
    k_loc = jnp.arange(3 * A_BLOCK) - WINDOW
    rel = k_loc[None, :] - q_loc[:, None]
    onehot = (_t5_bucket(rel)[..., None] == jnp.arange(NUM_BUCKETS)).astype(F32)
    bias = jnp.einsum('qkn,nh->hqk', onehot, rel_table.astype(F32), precision=lax.Precision.HIGHEST)
    in_window = jnp.abs(rel) <= WINDOW
    has_prev = jnp.array([False, True, True])[:, None, None]
    has_next = jnp.array([True, True, False])[:, None, None]
    k_blk = (jnp.arange(3 * A_BLOCK) // A_BLOCK)[None, None, :]
    valid = in_window[None] & ((k_blk != 0) | has_prev) & ((k_blk != 2) | has_next)
    full = jnp.where(valid[:, None], bias[None], NEG_INF)
    return full.reshape(3, A_KV_HEADS, A_GROUP * A_BLOCK, 3 * A_BLOCK)


def _window_kernel(sink_ref, q_ref, kp_ref, ko_ref, kn_ref, vp_ref, vo_ref, vn_ref,
                   bias_lo_ref, bias_mid_ref, bias_hi_ref, o_ref):
    for kvh in range(A_KV_HEADS):
        hs = slice(kvh * HEAD_DIM, (kvh + 1) * HEAD_DIM)
        heads = [kvh * A_GROUP + g for g in range(A_GROUP)]
        kc = jnp.concatenate([kp_ref[:, hs], ko_ref[:, hs], kn_ref[:, hs]], axis=0)
        vc = jnp.concatenate([vp_ref[:, hs], vo_ref[:, hs], vn_ref[:, hs]], axis=0)
        vc1 = jnp.concatenate([vc, jnp.ones_like(vc)], axis=1)
        sink = jnp.concatenate([jnp.full((A_BLOCK, 1), sink_ref[h], F32) for h in heads], axis=0)
        for jq in range(WIN_STEP):
            rows = slice(jq * A_BLOCK, (jq + 1) * A_BLOCK)
            band = slice(jq * A_BLOCK, (jq + 3) * A_BLOCK)
            bias_ref = bias_lo_ref if jq == 0 else (bias_hi_ref if jq == WIN_STEP - 1 else bias_mid_ref)
            q4 = jnp.concatenate([q_ref[rows, h * HEAD_DIM:(h + 1) * HEAD_DIM] for h in heads], axis=0)
            s = lax.dot_general(q4, kc[band], (((1,), (1,)), ((), ())), preferred_element_type=F32)
            s = s + bias_ref[0, kvh]
            m = jnp.max(s, axis=-1, keepdims=True)
            p = jnp.exp((s - m).astype(BF16))
            ov = jnp.dot(p, vc1[band], preferred_element_type=F32)
            o = ov[:, :HEAD_DIM] / (ov[:, HEAD_DIM:] + jnp.exp(sink - m))
            for g, h in enumerate(heads):
                o_ref[rows, h * HEAD_DIM:(h + 1) * HEAD_DIM] = o[g * A_BLOCK:(g + 1) * A_BLOCK].astype(o_ref.dtype)


def _window_attention(proj, bias_a, sink, batch, seq):
    nb = seq // A_BLOCK
    ns = nb // WIN_STEP
    kcol, vcol = COL_KA // A_KV_WIDTH, COL_VA // A_KV_WIDTH
    prev = lambda b, n: b * nb + jnp.maximum(n * WIN_STEP - 1, 0)
    nxt = lambda b, n: b * nb + jnp.minimum((n + 1) * WIN_STEP, nb - 1)
    own = lambda b, n: b * ns + n
    edge = lambda rfn, col: pl.BlockSpec((A_BLOCK, A_KV_WIDTH), lambda b, n: (rfn(b, n), col))
    main = lambda col: pl.BlockSpec((WIN_STEP * A_BLOCK, A_KV_WIDTH), lambda b, n: (own(b, n), col))
    bias_block = (1, A_KV_HEADS, A_GROUP * A_BLOCK, 3 * A_BLOCK)
    return pl.pallas_call(
        _window_kernel,
        grid=(batch, ns),
        in_specs=[
            pl.BlockSpec(memory_space=pltpu.SMEM),
            pl.BlockSpec((WIN_STEP * A_BLOCK, A_WIDTH), lambda b, n: (own(b, n), COL_QA // A_WIDTH)),
            edge(prev, kcol), main(kcol), edge(nxt, kcol),
            edge(prev, vcol), main(vcol), edge(nxt, vcol),
            pl.BlockSpec(bias_block, lambda b, n: (jnp.where(n == 0, 0, 1), 0, 0, 0)),
            pl.BlockSpec(bias_block, lambda b, n: (1, 0, 0, 0)),
            pl.BlockSpec(bias_block, lambda b, n: (jnp.where(n == ns - 1, 2, 1), 0, 0, 0)),
        ],
        out_specs=pl.BlockSpec((WIN_STEP * A_BLOCK, A_WIDTH), lambda b, n: (own(b, n), 0)),
        out_shape=jax.ShapeDtypeStruct((batch * seq, A_WIDTH), BF16),
        compiler_params=pltpu.CompilerParams(
            dimension_semantics=("parallel", "arbitrary"), vmem_limit_bytes=VMEM_LIMIT),
        name="window_attention",
    )(sink, proj, proj, proj, proj, proj, proj, proj, bias_a, bias_a, bias_a)


def _neighbourhood_bias(rpb, rows):
    kh = min(NB_ROWS, rows)
    nblk = rows // NA_ROWS
    r0 = jnp.array([0, NA_ROWS, (nblk - 1) * NA_ROWS])[:, None]
    r = r0 + jnp.arange(NA_ROWS)[None, :]
    rs = jnp.clip(r - kh // 2, 0, rows - kh)
    key_r = (r0 - NA_ROWS) + jnp.arange(3 * NA_ROWS)[None, :]
    row_ok = (key_r[:, None, :] >= rs[:, :, None]) & (key_r[:, None, :] < rs[:, :, None] + kh)
    dr = key_r[:, None, :] - r[:, :, None] + (NB_ROWS - 1)
    cols = jnp.arange(GRID_W)
    cs = jnp.clip(cols - NB_COLS // 2, 0, GRID_W - NB_COLS)
    col_ok = (cols[None, :] >= cs[:, None]) & (cols[None, :] < cs[:, None] + NB_COLS)
    dc = cols[None, :] - cols[:, None] + (NB_COLS - 1)
    row_sel = ((dr[..., None] == jnp.arange(2 * NB_ROWS - 1)) & row_ok[..., None]).astype(F32)
    col_sel = ((dc[..., None] == jnp.arange(2 * NB_COLS - 1)) & col_ok[..., None]).astype(F32)
    bias = jnp.einsum('hde,aikd,cje->ahickj', rpb.astype(F32), row_sel, col_sel,
                      precision=lax.Precision.HIGHEST)
    ok = row_ok[:, None, :, None, :, None] & col_ok[None, None, None, :, None, :]
    full = jnp.where(ok, bias, NEG_INF)
    return full.reshape(3, B_HEADS, NA_BLOCK, 3 * NA_BLOCK)


def _neighbourhood_kernel(q_ref, kp_ref, ko_ref, kn_ref, vp_ref, vo_ref, vn_ref, bias_ref, o_ref):
    for h in range(B_HEADS):
        hs = slice(h * HEAD_DIM, (h + 1) * HEAD_DIM)
        kc = jnp.concatenate([kp_ref[:, hs], ko_ref[:, hs], kn_ref[:, hs]], axis=0)
        vc = jnp.concatenate([vp_ref[:, hs], vo_ref[:, hs], vn_ref[:, hs]], axis=0)
        s = lax.dot_general(q_ref[:, hs], kc, (((1,), (1,)), ((), ())), preferred_element_type=F32)
        s = s + bias_ref[0, h]
        m = jnp.max(s, axis=-1, keepdims=True)
        p = jnp.exp((s - m).astype(BF16))
        ov = jnp.dot(p, jnp.concatenate([vc, jnp.ones_like(vc)], axis=1), preferred_element_type=F32)
        o_ref[:, hs] = (ov[:, :HEAD_DIM] / ov[:, HEAD_DIM:]).astype(o_ref.dtype)


def _neighbourhood_attention(proj, bias_b, batch, seq):
    nb = seq // NA_BLOCK

    def cls(n):
        return jnp.where(n == 0, 0, jnp.where(n == nb - 1, 2, 1))

    def row(b, n):
        return b * nb + n

    prev = lambda b, n: row(b, jnp.maximum(n - 1, 0))
    nxt = lambda b, n: row(b, jnp.minimum(n + 1, nb - 1))
    spec = lambda rfn, col: pl.BlockSpec((NA_BLOCK, B_WIDTH), lambda b, n: (rfn(b, n), col))
    qcol, kcol, vcol = COL_QB // B_WIDTH, COL_KB // B_WIDTH, COL_VB // B_WIDTH
    return pl.pallas_call(
        _neighbourhood_kernel,
        grid=(batch, nb),
        in_specs=[
            spec(row, qcol),
            spec(prev, kcol), spec(row, kcol), spec(nxt, kcol),
            spec(prev, vcol), spec(row, vcol), spec(nxt, vcol),
            pl.BlockSpec((1, B_HEADS, NA_BLOCK, 3 * NA_BLOCK), lambda b, n: (cls(n), 0, 0, 0)),
        ],
        out_specs=pl.BlockSpec((NA_BLOCK, B_WIDTH), lambda b, n: (row(b, n), 0)),
        out_shape=jax.ShapeDtypeStruct((batch * seq, B_WIDTH), BF16),
        compiler_params=pltpu.CompilerParams(
            dimension_semantics=("parallel", "arbitrary"), vmem_limit_bytes=VMEM_LIMIT),
        name="neighbourhood_attention",
    )(proj, proj, proj, proj, proj, proj, proj, bias_b)


def _pack_bf16_pairs(a):
    n = a.shape[1] // 2
    bits = lax.bitcast_convert_type(a, jnp.uint32)
    return bits[:, n:] | (bits[:, :n] >> 16)


def _unpack_bf16_pairs(p):
    lo = lax.bitcast_convert_type(p << 16, F32).astype(BF16)
    hi = lax.bitcast_convert_type(p & jnp.uint32(0xFFFF0000), F32).astype(BF16)
    return lo, hi


def _merge_kernel(oa_ref, ob_ref, ga_ref, gb_ref, x_ref, wa_ref, wb_ref, wo_ref, g2_ref, wr_ref, br_ref,
                  upper_ref, lower_ref, h_ref, xs_ref, eb_ref, wcol_ref, hn_scr):
    s = pl.program_id(0)
    cur = lax.rem(s, 2)
    prev = 1 - cur
    tm = MERGE_TM

    @pl.when(s == 0)
    def _():
        hn_scr[...] = jnp.zeros_like(hn_scr)

    hn_split = hn_scr[prev]
    logits = _router_logits(hn_split, wr_ref, br_ref)
    ua = jnp.dot(oa_ref[...], wa_ref[...], preferred_element_type=F32)
    ub = jnp.dot(ob_ref[...], wb_ref[...], preferred_element_type=F32)
    ids, w1, w2 = _route(logits)
    lrow, eb = _local_sort(ids, upper_ref, lower_ref)
    merged = ga_ref[...].astype(F32) * ua + gb_ref[...].astype(F32) * ub
    acc = jnp.dot(merged.astype(BF16), wo_ref[...], preferred_element_type=F32)
    xs = _place(lrow, hn_split[:tm])
    h = x_ref[...] + acc
    h_ref[...] = h
    ms = jnp.mean(h * h, axis=-1, keepdims=True)
    hn = h * lax.rsqrt(ms + EPS) * g2_ref[...]
    hn_hi = hn.astype(BF16)
    hn_scr[cur, :tm, :] = hn_hi
    hn_scr[cur, tm:, :] = (hn - hn_hi.astype(F32)).astype(BF16)

    xs_ref[...] = _pack_bf16_pairs(xs)
    eb_ref[0] = eb
    riota = lax.broadcasted_iota(jnp.int32, (ROUTER_LANES, tm), 0)
    wrows = jnp.where(riota == 0, w1, jnp.where(riota == 1, w2,
                      jnp.where(riota == 2, lrow[:, :tm], jnp.where(riota == 3, lrow[:, tm:], 0.0))))
    wcol_ref[...] = wrows.T


def _router_logits(hn_split, wr_ref, br_ref):
    tm = hn_split.shape[0] // 2
    lg = jnp.dot(hn_split, wr_ref[...], preferred_element_type=F32)
    lg_hi, lg_lo = lg[:tm], lg[tm:]
    return lg_hi + pltpu.roll(lg_hi, ROUTER_LANES // 2, axis=1) + lg_lo + br_ref[...]


def _route(logits):
    tm = logits.shape[0]
    lt = logits.T
    gl = lt[N_EXPERTS:N_EXPERTS + N_GROUPS]
    gmax = jnp.max(gl, axis=0, keepdims=True)
    giota = lax.broadcasted_iota(jnp.int32, gl.shape, 0)
    g_idx = jnp.min(jnp.where(gl == gmax, giota, N_GROUPS), axis=0, keepdims=True)
    p_g = 1.0 / jnp.sum(jnp.exp(gl - gmax), axis=0, keepdims=True)
    e_sel = jnp.zeros((EXPERTS_PER_GROUP, tm), F32)
    for g in range(N_GROUPS):
        e_sel = jnp.where(g_idx == g, lt[g * EXPERTS_PER_GROUP:(g + 1) * EXPERTS_PER_GROUP], e_sel)
    eiota = lax.broadcasted_iota(jnp.int32, e_sel.shape, 0)
    m1 = jnp.max(e_sel, axis=0, keepdims=True)
    i1 = jnp.min(jnp.where(e_sel == m1, eiota, EXPERTS_PER_GROUP), axis=0, keepdims=True)
    rest = jnp.where(eiota == i1, -jnp.inf, e_sel)
    m2 = jnp.max(rest, axis=0, keepdims=True)
    i2 = jnp.min(jnp.where(rest == m2, eiota, EXPERTS_PER_GROUP), axis=0, keepdims=True)
    t = jnp.exp(m2 - m1)
    ids = jnp.concatenate([g_idx * EXPERTS_PER_GROUP + i1, g_idx * EXPERTS_PER_GROUP + i2], axis=1)
    return ids, p_g / (1.0 + t), p_g * t / (1.0 + t)


def _local_sort(ids, upper_ref, lower_ref):
    eiota32 = lax.broadcasted_iota(jnp.int32, (N_EXPERTS, ids.shape[1]), 0)
    onehot = jnp.where(eiota32 == ids, 1.0, 0.0)
    count = jnp.sum(onehot, axis=1, keepdims=True)
    before = jnp.dot(onehot.astype(BF16), upper_ref[...], preferred_element_type=F32)
    nblk = jnp.floor((count + (LOCAL_BLOCK - 1)) * (1.0 / LOCAL_BLOCK))
    boff = jnp.dot(lower_ref[...], jnp.broadcast_to(nblk, (N_EXPERTS, ROUTER_LANES)).astype(BF16),
                   preferred_element_type=F32)[:, 0:1]
    lrow = jnp.sum(onehot * (before + LOCAL_BLOCK * boff), axis=0, keepdims=True)
    biota = lax.broadcasted_iota(jnp.int32, (N_EXPERTS, ROUTER_LANES), 1).astype(F32)
    owns = jnp.where(jnp.logical_and(biota >= boff, biota < boff + nblk), 1.0, 0.0)
    eiota_f = lax.broadcasted_iota(jnp.int32, (N_EXPERTS, ROUTER_LANES), 0).astype(F32)
    eb = jnp.sum(owns * eiota_f, axis=0, keepdims=True) + N_EXPERTS * (1.0 - jnp.sum(owns, axis=0, keepdims=True))
    return lrow, eb.astype(jnp.int32)


def _place(lrow, hn_hi):
    tm = hn_hi.shape[0]
    jiota = lax.broadcasted_iota(jnp.int32, (LOCAL_ROWS, tm), 0).astype(F32)
    place = jnp.where(jiota == lrow[:, :tm], 1.0, jnp.where(jiota == lrow[:, tm:], 1.0, 0.0)).astype(BF16)
    return jnp.dot(place, hn_hi, preferred_element_type=F32)


def _merge_and_route(oa, ob, proj, x2, wa, wb, wo, g2, wr_cat, br):
    t = x2.shape[0]
    n_assign = TOP_K * MERGE_TM
    upper = (jnp.arange(n_assign)[:, None] < jnp.arange(n_assign)[None, :]).astype(BF16)
    lower = (jnp.arange(N_EXPERTS)[:, None] > jnp.arange(N_EXPERTS)[None, :]).astype(BF16)
    const = lambda shape: pl.BlockSpec(shape, lambda s: (0, 0), pipeline_mode=pl.Buffered(1))
    n_tiles = t // MERGE_TM
    cur = lambda s: jnp.minimum(s, n_tiles - 1)
    prev = lambda s: jnp.maximum(s - 1, 0)
    return pl.pallas_call(
        _merge_kernel,
        grid=(n_tiles + 1,),
        in_specs=[
            pl.BlockSpec((MERGE_TM, A_WIDTH), lambda s: (cur(s), 0)),
            pl.BlockSpec((MERGE_TM, B_WIDTH), lambda s: (cur(s), 0)),
            pl.BlockSpec((MERGE_TM, D_MODEL), lambda s: (cur(s), COL_GA // D_MODEL)),
            pl.BlockSpec((MERGE_TM, D_MODEL), lambda s: (cur(s), COL_GB // D_MODEL)),
            pl.BlockSpec((MERGE_TM, D_MODEL), lambda s: (cur(s), 0)),
            const((A_WIDTH, D_MODEL)), const((B_WIDTH, D_MODEL)), const((D_MODEL, D_MODEL)),
            const((1, D_MODEL)), const((D_MODEL, ROUTER_LANES)), const((1, ROUTER_LANES)),
            const((n_assign, n_assign)), const((N_EXPERTS, N_EXPERTS)),
        ],
        out_specs=[
            pl.BlockSpec((MERGE_TM, D_MODEL), lambda s: (cur(s), 0)),
            pl.BlockSpec((LOCAL_ROWS, D_MODEL // 2), lambda s: (prev(s), 0)),
            pl.BlockSpec((1, 1, ROUTER_LANES), lambda s: (prev(s), 0, 0)),
            pl.BlockSpec((MERGE_TM, ROUTER_LANES), lambda s: (prev(s), 0)),
        ],
        scratch_shapes=[pltpu.VMEM((2, 2 * MERGE_TM, D_MODEL), BF16)],
        out_shape=[
            jax.ShapeDtypeStruct((t, D_MODEL), F32),
            jax.ShapeDtypeStruct((t // MERGE_TM * LOCAL_ROWS, D_MODEL // 2), jnp.uint32),
            jax.ShapeDtypeStruct((t // MERGE_TM, 1, ROUTER_LANES), jnp.int32),
            jax.ShapeDtypeStruct((t, ROUTER_LANES), F32),
        ],
        compiler_params=pltpu.CompilerParams(
            dimension_semantics=("arbitrary",), vmem_limit_bytes=VMEM_LIMIT),
        name="merge_and_route",
    )(oa, ob, proj, proj, x2, wa, wb, wo, g2, wr_cat, br, upper, lower)


def _expert_kernel(blk_ref, texp_ref, tcb_ref, tnb_ref, nused_ref, xs_hbm, wg_ref, wu_ref, wd_ref, y_hbm,
                   xbuf, ybuf, wgb, wub, wdb, gsem, ssem):
    i = pl.program_id(0)
    n_used = nused_ref[0]
    slot = lax.rem(i, 2)
    expert = texp_ref[i]

    def gather_copy(blk, dst_slot, j):
        return pltpu.make_async_copy(xs_hbm.at[blk], xbuf.at[dst_slot, pl.ds(j * LOCAL_BLOCK, LOCAL_BLOCK)],
                                     gsem.at[dst_slot])

    def scatter_copy(blk, src_slot, j):
        return pltpu.make_async_copy(ybuf.at[src_slot, pl.ds(j * LOCAL_BLOCK, LOCAL_BLOCK)], y_hbm.at[blk],
                                     ssem.at[src_slot])

    def issue_gather(tile, dst_slot):
        base = tcb_ref[tile]
        for j in range(MOE_BLOCKS):
            gather_copy(blk_ref[base + j], dst_slot, j).start()

    def wait_gather(s):
        for j in range(MOE_BLOCKS):
            gather_copy(0, s, j).wait()

    def for_valid_blocks(tile, fn):
        n_valid = tnb_ref[tile]

        @pl.when(n_valid == MOE_BLOCKS)
        def _():
            for j in range(MOE_BLOCKS):
                fn(j)

        @pl.when(n_valid < MOE_BLOCKS)
        def _():
            for j in range(MOE_BLOCKS):
                pl.when(j < n_valid)(functools.partial(fn, j))

    def issue_scatter(tile, src_slot):
        base = tcb_ref[tile]
        for_valid_blocks(tile, lambda j: scatter_copy(blk_ref[base + j], src_slot, j).start())

    def wait_scatter(tile, s):
        for_valid_blocks(tile, lambda j: scatter_copy(0, s, j).wait())

    @pl.when(jnp.logical_and(i == 0, n_used > 0))
    def _():
        issue_gather(0, 0)

    @pl.when(i < n_used)
    def _():
        wait_gather(slot)

        @pl.when(i + 1 < n_used)
        def _():
            issue_gather(i + 1, 1 - slot)

        @pl.when(i >= 2)
        def _():
            wait_scatter(i - 2, slot)

        @pl.when(jnp.logical_and(expert < N_EXPERTS, jnp.logical_or(i == 0, expert != texp_ref[jnp.maximum(i - 1, 0)])))
        def _():
            wgb[...] = wg_ref[0].astype(BF16)
            wub[...] = wu_ref[0].astype(BF16)
            wdb[...] = wd_ref[0].astype(BF16)

        @pl.when(expert < N_EXPERTS)
        def _():
            x_lo, x_hi = _unpack_bf16_pairs(xbuf[slot])
            half = D_MODEL // 2
            g = (jnp.dot(x_lo, wgb[:half, :], preferred_element_type=F32)
                 + jnp.dot(x_hi, wgb[half:, :], preferred_element_type=F32))
            u = (jnp.dot(x_lo, wub[:half, :], preferred_element_type=F32)
                 + jnp.dot(x_hi, wub[half:, :], preferred_element_type=F32))
            act = (g * jax.nn.sigmoid(g) * u).astype(BF16)
            y = jnp.dot(act, wdb[...], preferred_element_type=F32)
            ybuf[slot] = _pack_bf16_pairs(y.astype(BF16).astype(F32))

        @pl.when(expert >= N_EXPERTS)
        def _():
            ybuf[slot] = jnp.zeros((MOE_TM, D_MODEL // 2), jnp.uint32)

        issue_scatter(i, slot)

        @pl.when(i == n_used - 1)
        def _():
            wait_scatter(i, slot)

            @pl.when(i >= 1)
            def _():
                wait_scatter(i - 1, 1 - slot)


def _expert_ffn(block_order, tile_expert, tile_base, tile_blocks, n_used, xs_local, wg, wu, wd, n_tiles):
    n_blocks = xs_local.shape[0] // LOCAL_BLOCK
    half = D_MODEL // 2
    widx = lambda i, blk, te, cb, nb, nu: (jnp.minimum(te[i], N_EXPERTS - 1), 0, 0)
    grid_spec = pltpu.PrefetchScalarGridSpec(
        num_scalar_prefetch=5,
        grid=(n_tiles,),
        in_specs=[
            pl.BlockSpec(memory_space=pl.ANY),
            pl.BlockSpec((1, D_MODEL, D_EXPERT), widx),
            pl.BlockSpec((1, D_MODEL, D_EXPERT), widx),
            pl.BlockSpec((1, D_EXPERT, D_MODEL), widx),
        ],
        out_specs=pl.BlockSpec(memory_space=pl.ANY),
        scratch_shapes=[
            pltpu.VMEM((2, MOE_TM, half), jnp.uint32), pltpu.VMEM((2, MOE_TM, half), jnp.uint32),
            pltpu.VMEM((D_MODEL, D_EXPERT), BF16), pltpu.VMEM((D_MODEL, D_EXPERT), BF16),
            pltpu.VMEM((D_EXPERT, D_MODEL), BF16),
            pltpu.SemaphoreType.DMA((2,)), pltpu.SemaphoreType.DMA((2,)),
        ],
    )
    return pl.pallas_call(
        _expert_kernel,
        grid_spec=grid_spec,
        out_shape=jax.ShapeDtypeStruct((n_blocks, LOCAL_BLOCK, half), jnp.uint32),
        compiler_params=pltpu.CompilerParams(
            dimension_semantics=("arbitrary",), vmem_limit_bytes=VMEM_LIMIT),
        name="expert_ffn",
    )(block_order, tile_expert, tile_base, tile_blocks, n_used,
      xs_local.reshape(n_blocks, LOCAL_BLOCK, half), wg, wu, wd)


def _combine_kernel(h_ref, wcol_ref, y_ref, o_ref):
    y_lo, y_hi = _unpack_bf16_pairs(y_ref[...])
    w = wcol_ref[...]
    jiota = lax.broadcasted_iota(jnp.int32, (MERGE_TM, LOCAL_ROWS), 1).astype(F32)
    pick = (jnp.where(jiota == w[:, 2:3], w[:, 0:1], 0.0) + jnp.where(jiota == w[:, 3:4], w[:, 1:2], 0.0)).astype(BF16)
    half = D_MODEL // 2
    for cols, y in ((slice(0, half), y_lo), (slice(half, D_MODEL), y_hi)):
        o_ref[:, cols] = h_ref[:, cols] + jnp.dot(pick, y, preferred_element_type=F32)


def _combine(h, wcol, y_local):
    t = h.shape[0]
    return pl.pallas_call(
        _combine_kernel,
        grid=(t // MERGE_TM,),
        in_specs=[
            pl.BlockSpec((MERGE_TM, D_MODEL), lambda i: (i, 0)),
            pl.BlockSpec((MERGE_TM, ROUTER_LANES), lambda i: (i, 0)),
            pl.BlockSpec((LOCAL_ROWS, D_MODEL // 2), lambda i: (i, 0)),
        ],
        out_specs=pl.BlockSpec((MERGE_TM, D_MODEL), lambda i: (i, 0)),
        out_shape=jax.ShapeDtypeStruct((t, D_MODEL), F32),
        compiler_params=pltpu.CompilerParams(
            dimension_semantics=("parallel",), vmem_limit_bytes=VMEM_LIMIT),
        name="combine",
    )(h, wcol, y_local)


def _dispatch_plan(block_expert):
    n_blocks = block_expert.size
    n_keys = N_EXPERTS + 1
    n_tiles = n_blocks // MOE_BLOCKS + n_keys
    keys = block_expert.reshape(n_blocks)
    order = jnp.argsort(keys, stable=True).astype(jnp.int32)
    counts = jnp.sum(keys[None, :] == jnp.arange(n_keys, dtype=jnp.int32)[:, None], axis=1).astype(jnp.int32)
    tiles_per = (counts + MOE_BLOCKS - 1) // MOE_BLOCKS
    tile_end = jnp.cumsum(tiles_per)
    tile_first = tile_end - tiles_per
    start = jnp.cumsum(counts) - counts
    tile = jnp.arange(n_tiles, dtype=jnp.int32)
    tile_expert = jnp.minimum(jnp.sum(tile_end[None, :] <= tile[:, None], axis=1), n_keys - 1).astype(jnp.int32)
    sel = (tile_expert[:, None] == jnp.arange(n_keys, dtype=jnp.int32)[None, :]).astype(jnp.int32)
    pick = lambda v: jnp.sum(sel * v[None, :], axis=1)
    within = (tile - pick(tile_first)) * MOE_BLOCKS
    tile_base = jnp.clip(pick(start) + within, 0, n_blocks).astype(jnp.int32)
    tile_blocks = jnp.clip(pick(counts) - within, 0, MOE_BLOCKS).astype(jnp.int32)
    n_used = tile_end[-1].astype(jnp.int32).reshape(1)
    order = jnp.concatenate([order, jnp.zeros((MOE_BLOCKS,), jnp.int32)])
    return order, tile_expert, tile_base, tile_blocks, n_used, n_tiles


def _layer(x, norm1_g, w_in, q_norm_a, k_norm_a, sink_a, q_norm_b, k_norm_b, rpb_b, w_up_a, w_up_b, w_out,
           rel_table, norm2_g, w_rg, b_rg, w_re, b_re, w_gate, w_up, w_down):
    batch, seq, d = x.shape
    t = batch * seq
    x2 = x.reshape(t, d)

    cuts = [0, A_WIDTH, A_WIDTH + A_KV_WIDTH, A_WIDTH + 2 * A_KV_WIDTH]
    cuts += [cuts[-1] + B_WIDTH, cuts[-1] + 2 * B_WIDTH, cuts[-1] + 3 * B_WIDTH]
    cuts += [cuts[-1] + D_MODEL, cuts[-1] + 2 * D_MODEL]
    qa, ka, va, qb, kb, vb, ga, gb = [w_in[:, cuts[n]:cuts[n + 1]] for n in range(8)]
    w_cat = jnp.concatenate([qa, qb, kb, vb, ga, gb, ka, va], axis=1).astype(BF16)
    scale = HEAD_DIM ** -0.5
    ones = lambda n: jnp.ones((n,), F32)
    gain_cat = jnp.concatenate([
        jnp.tile(q_norm_a.astype(F32) * scale, A_HEADS), jnp.tile(q_norm_b.astype(F32) * scale, B_HEADS),
        jnp.tile(k_norm_b.astype(F32), B_HEADS), ones(B_WIDTH + 2 * D_MODEL),
        jnp.tile(k_norm_a.astype(F32), A_KV_HEADS), ones(A_KV_WIDTH)]).reshape(1, IN_COLS)

    kind = lambda v, n: jnp.full((n,), v, F32)
    kind_cat = jnp.concatenate([kind(1.0, COL_VB), kind(0.0, B_WIDTH), kind(2.0, 2 * D_MODEL),
                                kind(1.0, A_KV_WIDTH), kind(0.0, A_KV_WIDTH)]).reshape(1, IN_COLS)

    proj = _in_projection(x2, norm1_g.astype(F32).reshape(1, d), w_cat, gain_cat, kind_cat)
    oa = _window_attention(proj, _window_bias(rel_table), sink_a.astype(F32), batch, seq)
    ob = _neighbourhood_attention(proj, _neighbourhood_bias(rpb_b, seq // GRID_W), batch, seq)

    wr = jnp.concatenate([w_re.astype(F32), w_rg.astype(F32)], axis=1)
    wr_hi = wr.astype(BF16)
    wr_lo = (wr - wr_hi.astype(F32)).astype(BF16)
    half = ROUTER_LANES // 2
    padc = lambda a: jnp.pad(a, ((0, 0), (0, half - a.shape[1])))
    wr_cat = jnp.concatenate([padc(wr_hi), padc(wr_lo)], axis=1)
    br = jnp.pad(jnp.concatenate([b_re.astype(F32), b_rg.astype(F32)]), (0, ROUTER_LANES - N_EXPERTS - N_GROUPS))

    h, xs_local, block_expert, wcol = _merge_and_route(
        oa, ob, proj, x2, w_up_a.astype(BF16), w_up_b.astype(BF16), w_out.astype(BF16),
        norm2_g.astype(F32).reshape(1, d), wr_cat, br.reshape(1, ROUTER_LANES))

    order, tile_expert, tile_base, tile_blocks, n_used, n_tiles = _dispatch_plan(
        block_expert[:, 0, :LOCAL_BLOCKS])
    y_local = _expert_ffn(order, tile_expert, tile_base, tile_blocks, n_used, xs_local,
                          w_gate.astype(F32), w_up.astype(F32), w_down.astype(F32), n_tiles)
    out = _combine(h, wcol, y_local.reshape(-1, d // 2))
    return out.reshape(batch, seq, d)


def kernel(x, norm1_g, w_in, q_norm_a, k_norm_a, sink_a, q_norm_b, k_norm_b, rpb_b, w_up_a, w_up_b, w_out,
           rel_bias_table, norm2_g, w_router_group, b_router_group, w_router_expert, b_router_expert,
           w_gate, w_up, w_down):
    return _layer(x, norm1_g[0], w_in[0], q_norm_a[0], k_norm_a[0], sink_a[0], q_norm_b[0], k_norm_b[0],
                  rpb_b[0], w_up_a[0], w_up_b[0], w_out[0], rel_bias_table, norm2_g[0],
                  w_router_group[0], b_router_group[0], w_router_expert[0], b_router_expert[0],
                  w_gate[0], w_up[0], w_down[0])
```

```python
import functools
import math

import jax
import jax.numpy as jnp
from jax import lax
from jax.experimental import pallas as pl
from jax.experimental.pallas import tpu as pltpu

F32 = jnp.float32
BF16 = jnp.bfloat16

D_MODEL = 2048
HEAD_DIM = 128
A_HEADS = 8
A_KV_HEADS = 2
A_GROUP = A_HEADS // A_KV_HEADS
A_WIDTH = A_HEADS * HEAD_DIM
A_KV_WIDTH = A_KV_HEADS * HEAD_DIM
WINDOW = 128
A_BLOCK = 128
NUM_BUCKETS = 32
MAX_DISTANCE = 128
B_HEADS = 8
B_WIDTH = B_HEADS * HEAD_DIM
GRID_W = 64
NB_ROWS = 8
NB_COLS = 16
IN_COLS = A_WIDTH + 2 * A_KV_WIDTH + 3 * B_WIDTH + 2 * D_MODEL
N_GROUPS = 4
EXPERTS_PER_GROUP = 8
N_EXPERTS = N_GROUPS * EXPERTS_PER_GROUP
TOP_K = 2
D_EXPERT = 512
EPS = 1e-6
NEG_INF = -1e30

COL_QA = 0
COL_QB = COL_QA + A_WIDTH
COL_KB = COL_QB + B_WIDTH
COL_VB = COL_KB + B_WIDTH
COL_GA = COL_VB + B_WIDTH
COL_GB = COL_GA + D_MODEL
COL_KA = COL_GB + D_MODEL
COL_VA = COL_KA + A_KV_WIDTH

PROJ_TM = 1024
PROJ_TN = 512
PROJ_SUB = 128
WIN_STEP = 4
NA_ROWS = 4
NA_BLOCK = NA_ROWS * GRID_W
MERGE_TM = 256
ROUTER_LANES = 128
LOCAL_BLOCK = 8
LOCAL_ROWS = 768
LOCAL_BLOCKS = LOCAL_ROWS // LOCAL_BLOCK
assert LOCAL_ROWS >= TOP_K * MERGE_TM + N_EXPERTS * (LOCAL_BLOCK - 1) and LOCAL_BLOCKS <= ROUTER_LANES
MOE_TM = 256
MOE_BLOCKS = MOE_TM // LOCAL_BLOCK
VMEM_LIMIT = 56 * 1024 * 1024


def _head_norm(a, gain):
    outs = []
    for hd in range(a.shape[1] // HEAD_DIM):
        sl = slice(hd * HEAD_DIM, (hd + 1) * HEAD_DIM)
        ah = a[:, sl]
        ss = jnp.sum(ah * ah, axis=-1, keepdims=True)
        outs.append(ah * lax.rsqrt(ss * (1.0 / HEAD_DIM) + EPS) * gain[:, sl])
    return outs[0] if len(outs) == 1 else jnp.concatenate(outs, axis=1)


def _inproj_kernel(x_ref, g1_ref, w_ref, gain_ref, kind_ref, o_ref, xn_ref):
    j = pl.program_id(1)

    def project(normalise_first):
        kind = kind_ref[...]
        for r in range(PROJ_TM // PROJ_SUB):
            rows = slice(r * PROJ_SUB, (r + 1) * PROJ_SUB)
            if normalise_first:
                x = x_ref[rows, :]
                ms = jnp.mean(x * x, axis=-1, keepdims=True)
                xn_ref[rows, :] = (x * lax.rsqrt(ms + EPS) * g1_ref[...]).astype(BF16)
            acc = jnp.dot(xn_ref[rows, :], w_ref[0], preferred_element_type=F32)
            out = jnp.where(kind == 1.0, _head_norm(acc, gain_ref[...]),
                            jnp.where(kind == 2.0, jax.nn.sigmoid(acc), acc))
            o_ref[rows, :] = out.astype(o_ref.dtype)

    pl.when(j == 0)(functools.partial(project, True))
    pl.when(j != 0)(functools.partial(project, False))


def _in_projection(x2, g1, w_cat, gain_cat, kind_cat):
    t = x2.shape[0]
    return pl.pallas_call(
        _inproj_kernel,
        grid=(t // PROJ_TM, IN_COLS // PROJ_TN),
        in_specs=[
            pl.BlockSpec((PROJ_TM, D_MODEL), lambda i, j: (i, 0)),
            pl.BlockSpec((1, D_MODEL), lambda i, j: (0, 0)),
            pl.BlockSpec((1, D_MODEL, PROJ_TN), lambda i, j: (j, 0, 0)),
            pl.BlockSpec((1, PROJ_TN), lambda i, j: (0, j)),
            pl.BlockSpec((1, PROJ_TN), lambda i, j: (0, j)),
        ],
        out_specs=pl.BlockSpec((PROJ_TM, PROJ_TN), lambda i, j: (i, j)),
        out_shape=jax.ShapeDtypeStruct((t, IN_COLS), BF16),
        scratch_shapes=[pltpu.VMEM((PROJ_TM, D_MODEL), BF16)],
        compiler_params=pltpu.CompilerParams(
            dimension_semantics=("parallel", "arbitrary"), vmem_limit_bytes=VMEM_LIMIT),
        name="in_projection",
    )(x2, g1, w_cat, gain_cat, kind_cat)


def _t5_bucket(rel):
    nb = NUM_BUCKETS // 2
    ret = jnp.where(rel > 0, nb, 0)
    n = jnp.abs(rel)
    max_exact = nb // 2
    nf = jnp.maximum(n, 1).astype(jnp.float32)
    large = max_exact + (jnp.log(nf / max_exact) / math.log(MAX_DISTANCE / max_exact)
                         * (nb - max_exact)).astype(jnp.int32)
    large = jnp.minimum(large, nb - 1)
    return ret + jnp.where(n < max_exact, n, large)


def _window_bias(rel_table):
    q_loc = jnp.arange(A_BLOCK)
    k_loc = jnp.arange(3 * A_BLOCK) - WINDOW
    rel = k_loc[None, :] - q_loc[:, None]
    onehot = (_t5_bucket(rel)[..., None] == jnp.arange(NUM_BUCKETS)).astype(F32)
    bias = jnp.einsum('qkn,nh->hqk', onehot, rel_table.astype(F32), precision=lax.Precision.HIGHEST)
    in_window = jnp.abs(rel) <= WINDOW
    has_prev = jnp.array([False, True, True])[:, None, None]
    has_next = jnp.array([True, True, False])[:, None, None]
    k_blk = (jnp.arange(3 * A_BLOCK) // A_BLOCK)[None, None, :]
    valid = in_window[None] & ((k_blk != 0) | has_prev) & ((k_blk != 2) | has_next)
    full = jnp.where(valid[:, None], bias[None], NEG_INF)
    return full.reshape(3, A_KV_HEADS, A_GROUP * A_BLOCK, 3 * A_BLOCK)


def _window_kernel(sink_ref, q_ref, kp_ref, ko_ref, kn_ref, vp_ref, vo_ref, vn_ref,
                   bias_lo_ref, bias_mid_ref, bias_hi_ref, o_ref):
    for kvh in range(A_KV_HEADS):
        hs = slice(kvh * HEAD_DIM, (kvh + 1) * HEAD_DIM)
        heads = [kvh * A_GROUP + g for g in range(A_GROUP)]
        kc = jnp.concatenate([kp_ref[:, hs], ko_ref[:, hs], kn_ref[:, hs]], axis=0)
        vc = jnp.concatenate([vp_ref[:, hs], vo_ref[:, hs], vn_ref[:, hs]], axis=0)
        vc1 = jnp.concatenate([vc, jnp.ones_like(vc)], axis=1)
        sink = jnp.concatenate([jnp.full((A_BLOCK, 1), sink_ref[h], F32) for h in heads], axis=0)
        for jq in range(WIN_STEP):
            rows = slice(jq * A_BLOCK, (jq + 1) * A_BLOCK)
            band = slice(jq * A_BLOCK, (jq + 3) * A_BLOCK)
            bias_ref = bias_lo_ref if jq == 0 else (bias_hi_ref if jq == WIN_STEP - 1 else bias_mid_ref)
            q4 = jnp.concatenate([q_ref[rows, h * HEAD_DIM:(h + 1) * HEAD_DIM] for h in heads], axis=0)
            s = lax.dot_general(q4, kc[band], (((1,), (1,)), ((), ())), preferred_element_type=F32)
            s = s + bias_ref[0, kvh]
            m = jnp.max(s, axis=-1, keepdims=True)
            p = jnp.exp((s - m).astype(BF16))
            ov = jnp.dot(p, vc1[band], preferred_element_type=F32)
            o = ov[:, :HEAD_DIM] / (ov[:, HEAD_DIM:] + jnp.exp(sink - m))
            for g, h in enumerate(heads):
                o_ref[rows, h * HEAD_DIM:(h + 1) * HEAD_DIM] = o[g * A_BLOCK:(g + 1) * A_BLOCK].astype(o_ref.dtype)


def _window_attention(proj, bias_a, sink, batch, seq):
    nb = seq // A_BLOCK
    ns = nb // WIN_STEP
    kcol, vcol = COL_KA // A_KV_WIDTH, COL_VA // A_KV_WIDTH
    prev = lambda b, n: b * nb + jnp.maximum(n * WIN_STEP - 1, 0)
    nxt = lambda b, n: b * nb + jnp.minimum((n + 1) * WIN_STEP, nb - 1)
    own = lambda b, n: b * ns + n
    edge = lambda rfn, col: pl.BlockSpec((A_BLOCK, A_KV_WIDTH), lambda b, n: (rfn(b, n), col))
    main = lambda col: pl.BlockSpec((WIN_STEP * A_BLOCK, A_KV_WIDTH), lambda b, n: (own(b, n), col))
    bias_block = (1, A_KV_HEADS, A_GROUP * A_BLOCK, 3 * A_BLOCK)
    return pl.pallas_call(
        _window_kernel,
        grid=(batch, ns),
        in_specs=[
            pl.BlockSpec(memory_space=pltpu.SMEM),
            pl.BlockSpec((WIN_STEP * A_BLOCK, A_WIDTH), lambda b, n: (own(b, n), COL_QA // A_WIDTH)),
            edge(prev, kcol), main(kcol), edge(nxt, kcol),
            edge(prev, vcol), main(vcol), edge(nxt, vcol),
            pl.BlockSpec(bias_block, lambda b, n: (jnp.where(n == 0, 0, 1), 0, 0, 0)),
            pl.BlockSpec(bias_block, lambda b, n: (1, 0, 0, 0)),
            pl.BlockSpec(bias_block, lambda b, n: (jnp.where(n == ns - 1, 2, 1), 0, 0, 0)),
        ],
        out_specs=pl.BlockSpec((WIN_STEP * A_BLOCK, A_WIDTH), lambda b, n: (own(b, n), 0)),
        out_shape=jax.ShapeDtypeStruct((batch * seq, A_WIDTH), BF16),
        compiler_params=pltpu.CompilerParams(
            dimension_semantics=("parallel", "arbitrary"), vmem_limit_bytes=VMEM_LIMIT),
        name="window_attention",
    )(sink, proj, proj, proj, proj, proj, proj, proj, bias_a, bias_a, bias_a)


def _neighbourhood_bias(rpb, rows):
    kh = min(NB_ROWS, rows)
    nblk = rows // NA_ROWS
    r0 = jnp.array([0, NA_ROWS, (nblk - 1) * NA_ROWS])[:, None]
    r = r0 + jnp.arange(NA_ROWS)[None, :]
    rs = jnp.clip(r - kh // 2, 0, rows - kh)
    key_r = (r0 - NA_ROWS) + jnp.arange(3 * NA_ROWS)[None, :]
    row_ok = (key_r[:, None, :] >= rs[:, :, None]) & (key_r[:, None, :] < rs[:, :, None] + kh)
    dr = key_r[:, None, :] - r[:, :, None] + (NB_ROWS - 1)
    cols = jnp.arange(GRID_W)
    cs = jnp.clip(cols - NB_COLS // 2, 0, GRID_W - NB_COLS)
    col_ok = (cols[None, :] >= cs[:, None]) & (cols[None, :] < cs[:, None] + NB_COLS)
    dc = cols[None, :] - cols[:, None] + (NB_COLS - 1)
    row_sel = ((dr[..., None] == jnp.arange(2 * NB_ROWS - 1)) & row_ok[..., None]).astype(F32)
    col_sel = ((dc[..., None] == jnp.arange(2 * NB_COLS - 1)) & col_ok[..., None]).astype(F32)
    by_col = jnp.einsum('hde,cje->hdcj', rpb.astype(F32), col_sel, precision=lax.Precision.HIGHEST)
    bias = sum(row_sel[:, None, :, None, :, None, dd] * by_col[None, :, dd, None, :, None, :]
               for dd in range(2 * NB_ROWS - 1))
    ok = row_ok[:, None, :, None, :, None] & col_ok[None, None, None, :, None, :]
    full = jnp.where(ok, bias, NEG_INF)
    return full.reshape(3, B_HEADS, NA_BLOCK, 3 * NA_BLOCK)


def _neighbourhood_kernel(q_ref, kp_ref, ko_ref, kn_ref, vp_ref, vo_ref, vn_ref, bias_ref, o_ref):
    for h in range(B_HEADS):
        hs = slice(h * HEAD_DIM, (h + 1) * HEAD_DIM)
        kc = jnp.concatenate([kp_ref[:, hs], ko_ref[:, hs], kn_ref[:, hs]], axis=0)
        vc = jnp.concatenate([vp_ref[:, hs], vo_ref[:, hs], vn_ref[:, hs]], axis=0)
        s = lax.dot_general(q_ref[:, hs], kc, (((1,), (1,)), ((), ())), preferred_element_type=F32)
        s = s + bias_ref[0, h]
        m = jnp.max(s, axis=-1, keepdims=True)
        p = jnp.exp((s - m).astype(BF16))
        ov = jnp.dot(p, jnp.concatenate([vc, jnp.ones_like(vc)], axis=1), preferred_element_type=F32)
        o_ref[:, hs] = (ov[:, :HEAD_DIM] / ov[:, HEAD_DIM:]).astype(o_ref.dtype)


def _neighbourhood_attention(proj, bias_b, batch, seq):
    nb = seq // NA_BLOCK

    def cls(n):
        return jnp.where(n == 0, 0, jnp.where(n == nb - 1, 2, 1))

    def row(b, n):
        return b * nb + n

    prev = lambda b, n: row(b, jnp.maximum(n - 1, 0))
    nxt = lambda b, n: row(b, jnp.minimum(n + 1, nb - 1))
    spec = lambda rfn, col: pl.BlockSpec((NA_BLOCK, B_WIDTH), lambda b, n: (rfn(b, n), col))
    qcol, kcol, vcol = COL_QB // B_WIDTH, COL_KB // B_WIDTH, COL_VB // B_WIDTH
    return pl.pallas_call(
        _neighbourhood_kernel,
        grid=(batch, nb),
        in_specs=[
            spec(row, qcol),
            spec(prev, kcol), spec(row, kcol), spec(nxt, kcol),
            spec(prev, vcol), spec(row, vcol), spec(nxt, vcol),
            pl.BlockSpec((1, B_HEADS, NA_BLOCK, 3 * NA_BLOCK), lambda b, n: (cls(n), 0, 0, 0)),
        ],
        out_specs=pl.BlockSpec((NA_BLOCK, B_WIDTH), lambda b, n: (row(b, n), 0)),
        out_shape=jax.ShapeDtypeStruct((batch * seq, B_WIDTH), BF16),
        compiler_params=pltpu.CompilerParams(
            dimension_semantics=("parallel", "arbitrary"), vmem_limit_bytes=VMEM_LIMIT),
        name="neighbourhood_attention",
    )(proj, proj, proj, proj, proj, proj, proj, bias_b)


def _pack_bf16_pairs(a):
    n = a.shape[1] // 2
    bits = lax.bitcast_convert_type(a, jnp.uint32)
    return bits[:, n:] | (bits[:, :n] >> 16)


def _unpack_bf16_pairs(p):
    lo = lax.bitcast_convert_type(p << 16, F32).astype(BF16)
    hi = lax.bitcast_convert_type(p & jnp.uint32(0xFFFF0000), F32).astype(BF16)
    return lo, hi


def _merge_kernel(oa_ref, ob_ref, ga_ref, gb_ref, x_ref, wa_ref, wb_ref, wo_ref, g2_ref, wr_ref, br_ref,
                  upper_ref, lower_ref, h_ref, xs_ref, eb_ref, wcol_ref, hn_scr):
    s = pl.program_id(0)
    cur = lax.rem(s, 2)
    prev = 1 - cur
    tm = MERGE_TM

    @pl.when(s == 0)
    def _():
        hn_scr[...] = jnp.zeros_like(hn_scr)

    hn_split = hn_scr[prev]
    logits = _router_logits(hn_split, wr_ref, br_ref)
    ua = jnp.dot(oa_ref[...], wa_ref[...], preferred_element_type=F32)
    ub = jnp.dot(ob_ref[...], wb_ref[...], preferred_element_type=F32)
    ids, w1, w2 = _route(logits)
    lrow, eb = _local_sort(ids, upper_ref, lower_ref)
    merged = ga_ref[...].astype(F32) * ua + gb_ref[...].astype(F32) * ub
    acc = jnp.dot(merged.astype(BF16), wo_ref[...], preferred_element_type=F32)
    xs = _place(lrow, hn_split[:tm])
    h = x_ref[...] + acc
    h_ref[...] = h
    ms = jnp.mean(h * h, axis=-1, keepdims=True)
    hn = h * lax.rsqrt(ms + EPS) * g2_ref[...]
    hn_hi = hn.astype(BF16)
    hn_scr[cur, :tm, :] = hn_hi
    hn_scr[cur, tm:, :] = (hn - hn_hi.astype(F32)).astype(BF16)

    xs_ref[...] = _pack_bf16_pairs(xs)
    eb_ref[0] = eb
    riota = lax.broadcasted_iota(jnp.int32, (ROUTER_LANES, tm), 0)
    wrows = jnp.where(riota == 0, w1, jnp.where(riota == 1, w2,
                      jnp.where(riota == 2, lrow[:, :tm], jnp.where(riota == 3, lrow[:, tm:], 0.0))))
    wcol_ref[...] = wrows.T


def _router_logits(hn_split, wr_ref, br_ref):
    tm = hn_split.shape[0] // 2
    lg = jnp.dot(hn_split, wr_ref[...], preferred_element_type=F32)
    lg_hi, lg_lo = lg[:tm], lg[tm:]
    return lg_hi + pltpu.roll(lg_hi, ROUTER_LANES // 2, axis=1) + lg_lo + br_ref[...]


def _route(logits):
    tm = logits.shape[0]
    lt = logits.T
    gl = lt[N_EXPERTS:N_EXPERTS + N_GROUPS]
    gmax = jnp.max(gl, axis=0, keepdims=True)
    giota = lax.broadcasted_iota(jnp.int32, gl.shape, 0)
    g_idx = jnp.min(jnp.where(gl == gmax, giota, N_GROUPS), axis=0, keepdims=True)
    p_g = 1.0 / jnp.sum(jnp.exp(gl - gmax), axis=0, keepdims=True)
    e_sel = jnp.zeros((EXPERTS_PER_GROUP, tm), F32)
    for g in range(N_GROUPS):
        e_sel = jnp.where(g_idx == g, lt[g * EXPERTS_PER_GROUP:(g + 1) * EXPERTS_PER_GROUP], e_sel)
    eiota = lax.broadcasted_iota(jnp.int32, e_sel.shape, 0)
    m1 = jnp.max(e_sel, axis=0, keepdims=True)
    i1 = jnp.min(jnp.where(e_sel == m1, eiota, EXPERTS_PER_GROUP), axis=0, keepdims=True)
    rest = jnp.where(eiota == i1, -jnp.inf, e_sel)
    m2 = jnp.max(rest, axis=0, keepdims=True)
    i2 = jnp.min(jnp.where(rest == m2, eiota, EXPERTS_PER_GROUP), axis=0, keepdims=True)
    t = jnp.exp(m2 - m1)
    ids = jnp.concatenate([g_idx * EXPERTS_PER_GROUP + i1, g_idx * EXPERTS_PER_GROUP + i2], axis=1)
    return ids, p_g / (1.0 + t), p_g * t / (1.0 + t)


def _local_sort(ids, upper_ref, lower_ref):
    eiota32 = lax.broadcasted_iota(jnp.int32, (N_EXPERTS, ids.shape[1]), 0)
    onehot = jnp.where(eiota32 == ids, 1.0, 0.0)
    count = jnp.sum(onehot, axis=1, keepdims=True)
    before = jnp.dot(onehot.astype(BF16), upper_ref[...], preferred_element_type=F32)
    nblk = jnp.floor((count + (LOCAL_BLOCK - 1)) * (1.0 / LOCAL_BLOCK))
    boff = jnp.dot(lower_ref[...], jnp.broadcast_to(nblk, (N_EXPERTS, ROUTER_LANES)).astype(BF16),
                   preferred_element_type=F32)[:, 0:1]
    lrow = jnp.sum(onehot * (before + LOCAL_BLOCK * boff), axis=0, keepdims=True)
    biota = lax.broadcasted_iota(jnp.int32, (N_EXPERTS, ROUTER_LANES), 1).astype(F32)
    owns = jnp.where(jnp.logical_and(biota >= boff, biota < boff + nblk), 1.0, 0.0)
    eiota_f = lax.broadcasted_iota(jnp.int32, (N_EXPERTS, ROUTER_LANES), 0).astype(F32)
    eb = jnp.sum(owns * eiota_f, axis=0, keepdims=True) + N_EXPERTS * (1.0 - jnp.sum(owns, axis=0, keepdims=True))
    return lrow, eb.astype(jnp.int32)


def _place(lrow, hn_hi):
    tm = hn_hi.shape[0]
    jiota = lax.broadcasted_iota(jnp.int32, (LOCAL_ROWS, tm), 0).astype(F32)
    place = jnp.where(jiota == lrow[:, :tm], 1.0, jnp.where(jiota == lrow[:, tm:], 1.0, 0.0)).astype(BF16)
    return jnp.dot(place, hn_hi, preferred_element_type=F32)


def _merge_and_route(oa, ob, proj, x2, wa, wb, wo, g2, wr_cat, br):
    t = x2.shape[0]
    n_assign = TOP_K * MERGE_TM
    upper = (jnp.arange(n_assign)[:, None] < jnp.arange(n_assign)[None, :]).astype(BF16)
    lower = (jnp.arange(N_EXPERTS)[:, None] > jnp.arange(N_EXPERTS)[None, :]).astype(BF16)
    const = lambda shape: pl.BlockSpec(shape, lambda s: (0, 0), pipeline_mode=pl.Buffered(1))
    n_tiles = t // MERGE_TM
    cur = lambda s: jnp.minimum(s, n_tiles - 1)
    prev = lambda s: jnp.maximum(s - 1, 0)
    return pl.pallas_call(
        _merge_kernel,
        grid=(n_tiles + 1,),
        in_specs=[
            pl.BlockSpec((MERGE_TM, A_WIDTH), lambda s: (cur(s), 0)),
            pl.BlockSpec((MERGE_TM, B_WIDTH), lambda s: (cur(s), 0)),
            pl.BlockSpec((MERGE_TM, D_MODEL), lambda s: (cur(s), COL_GA // D_MODEL)),
            pl.BlockSpec((MERGE_TM, D_MODEL), lambda s: (cur(s), COL_GB // D_MODEL)),
            pl.BlockSpec((MERGE_TM, D_MODEL), lambda s: (cur(s), 0)),
            const((A_WIDTH, D_MODEL)), const((B_WIDTH, D_MODEL)), const((D_MODEL, D_MODEL)),
            const((1, D_MODEL)), const((D_MODEL, ROUTER_LANES)), const((1, ROUTER_LANES)),
            const((n_assign, n_assign)), const((N_EXPERTS, N_EXPERTS)),
        ],
        out_specs=[
            pl.BlockSpec((MERGE_TM, D_MODEL), lambda s: (cur(s), 0)),
            pl.BlockSpec((LOCAL_ROWS, D_MODEL // 2), lambda s: (prev(s), 0)),
            pl.BlockSpec((1, 1, ROUTER_LANES), lambda s: (prev(s), 0, 0)),
            pl.BlockSpec((MERGE_TM, ROUTER_LANES), lambda s: (prev(s), 0)),
        ],
        scratch_shapes=[pltpu.VMEM((2, 2 * MERGE_TM, D_MODEL), BF16)],
        out_shape=[
            jax.ShapeDtypeStruct((t, D_MODEL), F32),
            jax.ShapeDtypeStruct((t // MERGE_TM * LOCAL_ROWS, D_MODEL // 2), jnp.uint32),
            jax.ShapeDtypeStruct((t // MERGE_TM, 1, ROUTER_LANES), jnp.int32),
            jax.ShapeDtypeStruct((t, ROUTER_LANES), F32),
        ],
        compiler_params=pltpu.CompilerParams(
            dimension_semantics=("arbitrary",), vmem_limit_bytes=VMEM_LIMIT),
        name="merge_and_route",
    )(oa, ob, proj, proj, x2, wa, wb, wo, g2, wr_cat, br, upper, lower)


def _expert_kernel(blk_ref, texp_ref, tcb_ref, tnb_ref, wslot_ref, wnext_ref, nused_ref,
                   xs_hbm, wg_hbm, wu_hbm, wd_hbm, y_hbm,
                   xbuf, ybuf, wgf, wuf, wdf, wgb, wub, wdb, gsem, ssem, wsem):
    n_used = nused_ref[0]

    def weight_copies(e, ws):
        return [pltpu.make_async_copy(src.at[e], dst.at[ws], wsem.at[ws])
                for src, dst in ((wg_hbm, wgf), (wu_hbm, wuf), (wd_hbm, wdf))]

    def gather_copy(blk, dst_slot, j):
        return pltpu.make_async_copy(xs_hbm.at[blk], xbuf.at[dst_slot, pl.ds(j * LOCAL_BLOCK, LOCAL_BLOCK)],
                                     gsem.at[dst_slot])

    def scatter_copy(blk, src_slot, j):
        return pltpu.make_async_copy(ybuf.at[src_slot, pl.ds(j * LOCAL_BLOCK, LOCAL_BLOCK)], y_hbm.at[blk],
                                     ssem.at[src_slot])

    def issue_gather(tile, dst_slot):
        @pl.when(texp_ref[tile] < N_EXPERTS)
        def _():
            base = tcb_ref[tile]
            for j in range(MOE_BLOCKS):
                gather_copy(blk_ref[base + j], dst_slot, j).start()

    def wait_gather(tile, s):
        @pl.when(texp_ref[tile] < N_EXPERTS)
        def _():
            for j in range(MOE_BLOCKS):
                gather_copy(0, s, j).wait()

    def for_valid_blocks(tile, fn):
        n_valid = tnb_ref[tile]

        @pl.when(n_valid == MOE_BLOCKS)
        def _():
            for j in range(MOE_BLOCKS):
                fn(j)

        @pl.when(n_valid < MOE_BLOCKS)
        def _():
            for j in range(MOE_BLOCKS):
                pl.when(j < n_valid)(functools.partial(fn, j))

    def issue_scatter(tile, src_slot):
        base = tcb_ref[tile]
        for_valid_blocks(tile, lambda j: scatter_copy(blk_ref[base + j], src_slot, j).start())

    def wait_scatter(tile, s):
        for_valid_blocks(tile, lambda j: scatter_copy(0, s, j).wait())

    @pl.when(n_used > 0)
    def _():
        issue_gather(0, 0)

        @pl.when(texp_ref[0] < N_EXPERTS)
        def _():
            for cp in weight_copies(texp_ref[0], wslot_ref[0]):
                cp.start()

    def tile_step(i, carry):
        slot = lax.rem(i, 2)
        expert = texp_ref[i]
        wait_gather(i, slot)

        @pl.when(i + 1 < n_used)
        def _():
            issue_gather(i + 1, 1 - slot)

        @pl.when(i >= 2)
        def _():
            wait_scatter(i - 2, slot)

        @pl.when(jnp.logical_and(expert < N_EXPERTS, jnp.logical_or(i == 0, expert != texp_ref[jnp.maximum(i - 1, 0)])))
        def _():
            ws = wslot_ref[i]
            for cp in weight_copies(expert, ws):
                cp.wait()
            for static_ws in (0, 1):
                @pl.when(ws == static_ws)
                def _():
                    wgb[...] = wgf[static_ws].astype(BF16)
                    wub[...] = wuf[static_ws].astype(BF16)
                    wdb[...] = wdf[static_ws].astype(BF16)
            nxt = wnext_ref[i]

            @pl.when(nxt < N_EXPERTS)
            def _():
                for cp in weight_copies(nxt, 1 - ws):
                    cp.start()

        @pl.when(expert < N_EXPERTS)
        def _():
            x_lo, x_hi = _unpack_bf16_pairs(xbuf[slot])
            half = D_MODEL // 2
            g = (jnp.dot(x_lo, wgb[:half, :], preferred_element_type=F32)
                 + jnp.dot(x_hi, wgb[half:, :], preferred_element_type=F32))
            u = (jnp.dot(x_lo, wub[:half, :], preferred_element_type=F32)
                 + jnp.dot(x_hi, wub[half:, :], preferred_element_type=F32))
            act = (g * jax.nn.sigmoid(g) * u).astype(BF16)
            y = jnp.dot(act, wdb[...], preferred_element_type=F32)
            ybuf[slot] = _pack_bf16_pairs(y.astype(BF16).astype(F32))

        @pl.when(expert >= N_EXPERTS)
        def _():
            ybuf[slot] = jnp.zeros((MOE_TM, D_MODEL // 2), jnp.uint32)

        issue_scatter(i, slot)
        return carry

    lax.fori_loop(0, n_used, tile_step, 0)

    for back in (1, 2):
        @pl.when(n_used >= back)
        def _():
            last = jnp.maximum(n_used - back, 0)
            wait_scatter(last, lax.rem(last, 2))


def _expert_ffn(plan, xs_local, wg, wu, wd):
    n_blocks = xs_local.shape[0] // LOCAL_BLOCK
    half = D_MODEL // 2
    any_spec = pl.BlockSpec(memory_space=pl.ANY)
    grid_spec = pltpu.PrefetchScalarGridSpec(
        num_scalar_prefetch=len(plan),
        grid=(1,),
        in_specs=[any_spec, any_spec, any_spec, any_spec],
        out_specs=any_spec,
        scratch_shapes=[
            pltpu.VMEM((2, MOE_TM, half), jnp.uint32), pltpu.VMEM((2, MOE_TM, half), jnp.uint32),
            pltpu.VMEM((2, D_MODEL, D_EXPERT), F32), pltpu.VMEM((2, D_MODEL, D_EXPERT), F32),
            pltpu.VMEM((2, D_EXPERT, D_MODEL), F32),
            pltpu.VMEM((D_MODEL, D_EXPERT), BF16), pltpu.VMEM((D_MODEL, D_EXPERT), BF16),
            pltpu.VMEM((D_EXPERT, D_MODEL), BF16),
            pltpu.SemaphoreType.DMA((2,)), pltpu.SemaphoreType.DMA((2,)), pltpu.SemaphoreType.DMA((2,)),
        ],
    )
    return pl.pallas_call(
        _expert_kernel,
        grid_spec=grid_spec,
        out_shape=jax.ShapeDtypeStruct((n_blocks, LOCAL_BLOCK, half), jnp.uint32),
        compiler_params=pltpu.CompilerParams(
            dimension_semantics=("arbitrary",), vmem_limit_bytes=VMEM_LIMIT),
        name="expert_ffn",
    )(*plan, xs_local.reshape(n_blocks, LOCAL_BLOCK, half), wg, wu, wd)


def _combine_kernel(h_ref, wcol_ref, y_ref, o_ref):
    y_lo, y_hi = _unpack_bf16_pairs(y_ref[...])
    w = wcol_ref[...]
    jiota = lax.broadcasted_iota(jnp.int32, (MERGE_TM, LOCAL_ROWS), 1).astype(F32)
    pick = (jnp.where(jiota == w[:, 2:3], w[:, 0:1], 0.0) + jnp.where(jiota == w[:, 3:4], w[:, 1:2], 0.0)).astype(BF16)
    half = D_MODEL // 2
    for cols, y in ((slice(0, half), y_lo), (slice(half, D_MODEL), y_hi)):
        o_ref[:, cols] = h_ref[:, cols] + jnp.dot(pick, y, preferred_element_type=F32)


def _combine(h, wcol, y_local):
    t = h.shape[0]
    return pl.pallas_call(
        _combine_kernel,
        grid=(t // MERGE_TM,),
        in_specs=[
            pl.BlockSpec((MERGE_TM, D_MODEL), lambda i: (i, 0)),
            pl.BlockSpec((MERGE_TM, ROUTER_LANES), lambda i: (i, 0)),
            pl.BlockSpec((LOCAL_ROWS, D_MODEL // 2), lambda i: (i, 0)),
        ],
        out_specs=pl.BlockSpec((MERGE_TM, D_MODEL), lambda i: (i, 0)),
        out_shape=jax.ShapeDtypeStruct((t, D_MODEL), F32),
        compiler_params=pltpu.CompilerParams(
            dimension_semantics=("parallel",), vmem_limit_bytes=VMEM_LIMIT),
        name="combine",
    )(h, wcol, y_local)


def _dispatch_plan(block_expert):
    n_blocks = block_expert.size
    n_keys = N_EXPERTS + 1
    n_tiles = n_blocks // MOE_BLOCKS + n_keys
    keys = block_expert.reshape(n_blocks)
    order = jnp.argsort(keys, stable=True).astype(jnp.int32)
    counts = jnp.sum(keys[None, :] == jnp.arange(n_keys, dtype=jnp.int32)[:, None], axis=1).astype(jnp.int32)
    tiles_per = (counts + MOE_BLOCKS - 1) // MOE_BLOCKS
    tile_end = jnp.cumsum(tiles_per)
    tile_first = tile_end - tiles_per
    start = jnp.cumsum(counts) - counts
    tile = jnp.arange(n_tiles, dtype=jnp.int32)
    tile_expert = jnp.minimum(jnp.sum(tile_end[None, :] <= tile[:, None], axis=1), n_keys - 1).astype(jnp.int32)
    sel = (tile_expert[:, None] == jnp.arange(n_keys, dtype=jnp.int32)[None, :]).astype(jnp.int32)
    pick = lambda v: jnp.sum(sel * v[None, :], axis=1)
    within = (tile - pick(tile_first)) * MOE_BLOCKS
    tile_base = jnp.clip(pick(start) + within, 0, n_blocks).astype(jnp.int32)
    tile_blocks = jnp.clip(pick(counts) - within, 0, MOE_BLOCKS).astype(jnp.int32)
    n_used = tile_end[-1].astype(jnp.int32).reshape(1)
    order = jnp.concatenate([order, jnp.zeros((MOE_BLOCKS,), jnp.int32)])
    key = jnp.arange(n_keys, dtype=jnp.int32)
    present = jnp.logical_and(counts > 0, key < N_EXPERTS)
    slot_of = (jnp.cumsum(present) - present) % 2
    later = jnp.logical_and(key[None, :] > key[:, None], present[None, :])
    next_of = jnp.min(jnp.where(later, key[None, :], N_EXPERTS), axis=1)
    return (order, tile_expert, tile_base, tile_blocks, pick(slot_of).astype(jnp.int32),
            pick(next_of).astype(jnp.int32), n_used)


def _layer(x, norm1_g, w_in, q_norm_a, k_norm_a, sink_a, q_norm_b, k_norm_b, rpb_b, w_up_a, w_up_b, w_out,
           rel_table, norm2_g, w_rg, b_rg, w_re, b_re, w_gate, w_up, w_down):
    batch, seq, d = x.shape
    t = batch * seq
    x2 = x.reshape(t, d)

    cuts = [0, A_WIDTH, A_WIDTH + A_KV_WIDTH, A_WIDTH + 2 * A_KV_WIDTH]
    cuts += [cuts[-1] + B_WIDTH, cuts[-1] + 2 * B_WIDTH, cuts[-1] + 3 * B_WIDTH]
    cuts += [cuts[-1] + D_MODEL, cuts[-1] + 2 * D_MODEL]
    qa, ka, va, qb, kb, vb, ga, gb = [w_in[:, cuts[n]:cuts[n + 1]] for n in range(8)]
    w_cat = jnp.concatenate([qa, qb, kb, vb, ga, gb, ka, va], axis=1).astype(BF16)
    w_cat = w_cat.reshape(d, IN_COLS // PROJ_TN, PROJ_TN).transpose(1, 0, 2)
    scale = HEAD_DIM ** -0.5
    ones = lambda n: jnp.ones((n,), F32)
    gain_cat = jnp.concatenate([
        jnp.tile(q_norm_a.astype(F32) * scale, A_HEADS), jnp.tile(q_norm_b.astype(F32) * scale, B_HEADS),
        jnp.tile(k_norm_b.astype(F32), B_HEADS), ones(B_WIDTH + 2 * D_MODEL),
        jnp.tile(k_norm_a.astype(F32), A_KV_HEADS), ones(A_KV_WIDTH)]).reshape(1, IN_COLS)

    kind = lambda v, n: jnp.full((n,), v, F32)
    kind_cat = jnp.concatenate([kind(1.0, COL_VB), kind(0.0, B_WIDTH), kind(2.0, 2 * D_MODEL),
                                kind(1.0, A_KV_WIDTH), kind(0.0, A_KV_WIDTH)]).reshape(1, IN_COLS)

    proj = _in_projection(x2, norm1_g.astype(F32).reshape(1, d), w_cat, gain_cat, kind_cat)
    oa = _window_attention(proj, _window_bias(rel_table), sink_a.astype(F32), batch, seq)
    ob = _neighbourhood_attention(proj, _neighbourhood_bias(rpb_b, seq // GRID_W), batch, seq)

    wr = jnp.concatenate([w_re.astype(F32), w_rg.astype(F32)], axis=1)
    wr_hi = wr.astype(BF16)
    wr_lo = (wr - wr_hi.astype(F32)).astype(BF16)
    half = ROUTER_LANES // 2
    padc = lambda a: jnp.pad(a, ((0, 0), (0, half - a.shape[1])))
    wr_cat = jnp.concatenate([padc(wr_hi), padc(wr_lo)], axis=1)
    br = jnp.pad(jnp.concatenate([b_re.astype(F32), b_rg.astype(F32)]), (0, ROUTER_LANES - N_EXPERTS - N_GROUPS))

    h, xs_local, block_expert, wcol = _merge_and_route(
        oa, ob, proj, x2, w_up_a.astype(BF16), w_up_b.astype(BF16), w_out.astype(BF16),
        norm2_g.astype(F32).reshape(1, d), wr_cat, br.reshape(1, ROUTER_LANES))

    plan = _dispatch_plan(block_expert[:, 0, :LOCAL_BLOCKS])
    y_local = _expert_ffn(plan, xs_local, w_gate.astype(F32), w_up.astype(F32), w_down.astype(F32))
    out = _combine(h, wcol, y_local.reshape(-1, d // 2))
    return out.reshape(batch, seq, d)


def kernel(x, norm1_g, w_in, q_norm_a, k_norm_a, sink_a, q_norm_b, k_norm_b, rpb_b, w_up_a, w_up_b, w_out,
           rel_bias_table, norm2_g, w_router_group, b_router_group, w_router_expert, b_router_expert,
           w_gate, w_up, w_down):
    return _layer(x, norm1_g[0], w_in[0], q_norm_a[0], k_norm_a[0], sink_a[0], q_norm_b[0], k_norm_b[0],
                  rpb_b[0], w_up_a[0], w_up_b[0], w_out[0], rel_bias_table, norm2_g[0],
                  w_router_group[0], b_router_group[0], w_router_expert[0], b_router_expert[0],
                  w_gate[0], w_up[0], w_down[0])
```

```python
import functools
import math

import jax
import jax.numpy as jnp
from jax import lax
from jax.experimental import pallas as pl
from jax.experimental.pallas import tpu as pltpu

F32 = jnp.float32
BF16 = jnp.bfloat16

D_MODEL = 2048
HEAD_DIM = 128
A_HEADS = 8
A_KV_HEADS = 2
A_GROUP = A_HEADS // A_KV_HEADS
A_WIDTH = A_HEADS * HEAD_DIM
A_KV_WIDTH = A_KV_HEADS * HEAD_DIM
WINDOW = 128
A_BLOCK = 128
NUM_BUCKETS = 32
MAX_DISTANCE = 128
B_HEADS = 8
B_WIDTH = B_HEADS * HEAD_DIM
GRID_W = 64
NB_ROWS = 8
NB_COLS = 16
IN_COLS = A_WIDTH + 2 * A_KV_WIDTH + 3 * B_WIDTH + 2 * D_MODEL
N_GROUPS = 4
EXPERTS_PER_GROUP = 8
N_EXPERTS = N_GROUPS * EXPERTS_PER_GROUP
TOP_K = 2
D_EXPERT = 512
EPS = 1e-6
NEG_INF = -1e30

COL_QA = 0
COL_QB = COL_QA + A_WIDTH
COL_KB = COL_QB + B_WIDTH
COL_VB = COL_KB + B_WIDTH
COL_GA = COL_VB + B_WIDTH
COL_GB = COL_GA + D_MODEL
COL_KA = COL_GB + D_MODEL
COL_VA = COL_KA + A_KV_WIDTH

PROJ_TM = 1024
PROJ_TN = 512
PROJ_SUB = 128
WIN_STEP = 4
NA_ROWS = 4
NA_BLOCK = NA_ROWS * GRID_W
MERGE_TM = 256
ROUTER_LANES = 128
LOCAL_BLOCK = 8
LOCAL_ROWS = 768
LOCAL_BLOCKS = LOCAL_ROWS // LOCAL_BLOCK
assert LOCAL_ROWS >= TOP_K * MERGE_TM + N_EXPERTS * (LOCAL_BLOCK - 1) and LOCAL_BLOCKS <= ROUTER_LANES
MOE_TM = 256
MOE_BLOCKS = MOE_TM // LOCAL_BLOCK
VMEM_LIMIT = 56 * 1024 * 1024


def _head_norm(a, gain):
    outs = []
    for hd in range(a.shape[1] // HEAD_DIM):
        sl = slice(hd * HEAD_DIM, (hd + 1) * HEAD_DIM)
        ah = a[:, sl]
        ss = jnp.sum(ah * ah, axis=-1, keepdims=True)
        outs.append(ah * lax.rsqrt(ss * (1.0 / HEAD_DIM) + EPS) * gain[:, sl])
    return outs[0] if len(outs) == 1 else jnp.concatenate(outs, axis=1)


def _inproj_kernel(x_ref, g1_ref, w_ref, gain_ref, kind_ref, o_ref, xn_ref):
    j = pl.program_id(1)

    def project(normalise_first):
        kind = kind_ref[...]
        for r in range(PROJ_TM // PROJ_SUB):
            rows = slice(r * PROJ_SUB, (r + 1) * PROJ_SUB)
            if normalise_first:
                x = x_ref[rows, :]
                ms = jnp.mean(x * x, axis=-1, keepdims=True)
                xn_ref[rows, :] = (x * lax.rsqrt(ms + EPS) * g1_ref[...]).astype(BF16)
            acc = jnp.dot(xn_ref[rows, :], w_ref[...], preferred_element_type=F32)
            out = jnp.where(kind == 1.0, _head_norm(acc, gain_ref[...]),
                            jnp.where(kind == 2.0, jax.nn.sigmoid(acc), acc))
            o_ref[rows, :] = out.astype(o_ref.dtype)

    pl.when(j == 0)(functools.partial(project, True))
    pl.when(j != 0)(functools.partial(project, False))


def _in_projection(x2, g1, w_cat, gain_cat, kind_cat):
    t = x2.shape[0]
    return pl.pallas_call(
        _inproj_kernel,
        grid=(t // PROJ_TM, IN_COLS // PROJ_TN),
        in_specs=[
            pl.BlockSpec((PROJ_TM, D_MODEL), lambda i, j: (i, 0)),
            pl.BlockSpec((1, D_MODEL), lambda i, j: (0, 0)),
            pl.BlockSpec((D_MODEL, PROJ_TN), lambda i, j: (0, j)),
            pl.BlockSpec((1, PROJ_TN), lambda i, j: (0, j)),
            pl.BlockSpec((1, PROJ_TN), lambda i, j: (0, j)),
        ],
        out_specs=pl.BlockSpec((PROJ_TM, PROJ_TN), lambda i, j: (i, j)),
        out_shape=jax.ShapeDtypeStruct((t, IN_COLS), BF16),
        scratch_shapes=[pltpu.VMEM((PROJ_TM, D_MODEL), BF16)],
        compiler_params=pltpu.CompilerParams(
            dimension_semantics=("parallel", "arbitrary"), vmem_limit_bytes=VMEM_LIMIT),
        name="in_projection",
    )(x2, g1, w_cat, gain_cat, kind_cat)


def _t5_bucket(rel):
    nb = NUM_BUCKETS // 2
    ret = jnp.where(rel > 0, nb, 0)
    n = jnp.abs(rel)
    max_exact = nb // 2
    nf = jnp.maximum(n, 1).astype(jnp.float32)
    large = max_exact + (jnp.log(nf / max_exact) / math.log(MAX_DISTANCE / max_exact)
                         * (nb - max_exact)).astype(jnp.int32)
    large = jnp.minimum(large, nb - 1)
    return ret + jnp.where(n < max_exact, n, large)


def _window_bias(rel_table):
    q_loc = jnp.arange(A_BLOCK)
    k_loc = jnp.arange(3 * A_BLOCK) - WINDOW
    rel = k_loc[None, :] - q_loc[:, None]
    onehot = (_t5_bucket(rel)[..., None] == jnp.arange(NUM_BUCKETS)).astype(F32)
    bias = jnp.einsum('qkn,nh->hqk', onehot, rel_table.astype(F32), precision=lax.Precision.HIGHEST)
    in_window = jnp.abs(rel) <= WINDOW
    has_prev = jnp.array([False, True, True])[:, None, None]
    has_next = jnp.array([True, True, False])[:, None, None]
    k_blk = (jnp.arange(3 * A_BLOCK) // A_BLOCK)[None, None, :]
    valid = in_window[None] & ((k_blk != 0) | has_prev) & ((k_blk != 2) | has_next)
    full = jnp.where(valid[:, None], bias[None], NEG_INF)
    return full.reshape(3, A_KV_HEADS, A_GROUP * A_BLOCK, 3 * A_BLOCK)


def _window_kernel(sink_ref, q_ref, kp_ref, ko_ref, kn_ref, vp_ref, vo_ref, vn_ref,
                   bias_lo_ref, bias_mid_ref, bias_hi_ref, o_ref):
    for kvh in range(A_KV_HEADS):
        hs = slice(kvh * HEAD_DIM, (kvh + 1) * HEAD_DIM)
        heads = [kvh * A_GROUP + g for g in range(A_GROUP)]
        kc = jnp.concatenate([kp_ref[:, hs], ko_ref[:, hs], kn_ref[:, hs]], axis=0)
        vc = jnp.concatenate([vp_ref[:, hs], vo_ref[:, hs], vn_ref[:, hs]], axis=0)
        vc1 = jnp.concatenate([vc, jnp.ones_like(vc)], axis=1)
        sink = jnp.concatenate([jnp.full((A_BLOCK, 1), sink_ref[h], F32) for h in heads], axis=0)
        for jq in range(WIN_STEP):
            rows = slice(jq * A_BLOCK, (jq + 1) * A_BLOCK)
            band = slice(jq * A_BLOCK, (jq + 3) * A_BLOCK)
            bias_ref = bias_lo_ref if jq == 0 else (bias_hi_ref if jq == WIN_STEP - 1 else bias_mid_ref)
            q4 = jnp.concatenate([q_ref[rows, h * HEAD_DIM:(h + 1) * HEAD_DIM] for h in heads], axis=0)
            s = lax.dot_general(q4, kc[band], (((1,), (1,)), ((), ())), preferred_element_type=F32)
            s = s + bias_ref[0, kvh]
            m = jnp.max(s, axis=-1, keepdims=True)
            p = jnp.exp((s - m).astype(BF16))
            ov = jnp.dot(p, vc1[band], preferred_element_type=F32)
            o = ov[:, :HEAD_DIM] / (ov[:, HEAD_DIM:] + jnp.exp(sink - m))
            for g, h in enumerate(heads):
                o_ref[rows, h * HEAD_DIM:(h + 1) * HEAD_DIM] = o[g * A_BLOCK:(g + 1) * A_BLOCK].astype(o_ref.dtype)


def _window_attention(proj, bias_a, sink, batch, seq):
    nb = seq // A_BLOCK
    ns = nb // WIN_STEP
    kcol, vcol = COL_KA // A_KV_WIDTH, COL_VA // A_KV_WIDTH
    prev = lambda b, n: b * nb + jnp.maximum(n * WIN_STEP - 1, 0)
    nxt = lambda b, n: b * nb + jnp.minimum((n + 1) * WIN_STEP, nb - 1)
    own = lambda b, n: b * ns + n
    edge = lambda rfn, col: pl.BlockSpec((A_BLOCK, A_KV_WIDTH), lambda b, n: (rfn(b, n), col))
    main = lambda col: pl.BlockSpec((WIN_STEP * A_BLOCK, A_KV_WIDTH), lambda b, n: (own(b, n), col))
    bias_block = (1, A_KV_HEADS, A_GROUP * A_BLOCK, 3 * A_BLOCK)
    return pl.pallas_call(
        _window_kernel,
        grid=(batch, ns),
        in_specs=[
            pl.BlockSpec(memory_space=pltpu.SMEM),
            pl.BlockSpec((WIN_STEP * A_BLOCK, A_WIDTH), lambda b, n: (own(b, n), COL_QA // A_WIDTH)),
            edge(prev, kcol), main(kcol), edge(nxt, kcol),
            edge(prev, vcol), main(vcol), edge(nxt, vcol),
            pl.BlockSpec(bias_block, lambda b, n: (jnp.where(n == 0, 0, 1), 0, 0, 0)),
            pl.BlockSpec(bias_block, lambda b, n: (1, 0, 0, 0)),
            pl.BlockSpec(bias_block, lambda b, n: (jnp.where(n == ns - 1, 2, 1), 0, 0, 0)),
        ],
        out_specs=pl.BlockSpec((WIN_STEP * A_BLOCK, A_WIDTH), lambda b, n: (own(b, n), 0)),
        out_shape=jax.ShapeDtypeStruct((batch * seq, A_WIDTH), BF16),
        compiler_params=pltpu.CompilerParams(
            dimension_semantics=("parallel", "arbitrary"), vmem_limit_bytes=VMEM_LIMIT),
        name="window_attention",
    )(sink, proj, proj, proj, proj, proj, proj, proj, bias_a, bias_a, bias_a)


def _neighbourhood_bias(rpb, rows):
    kh = min(NB_ROWS, rows)
    nblk = rows // NA_ROWS
    r0 = jnp.array([0, NA_ROWS, (nblk - 1) * NA_ROWS])[:, None]
    r = r0 + jnp.arange(NA_ROWS)[None, :]
    rs = jnp.clip(r - kh // 2, 0, rows - kh)
    key_r = (r0 - NA_ROWS) + jnp.arange(3 * NA_ROWS)[None, :]
    row_ok = (key_r[:, None, :] >= rs[:, :, None]) & (key_r[:, None, :] < rs[:, :, None] + kh)
    dr = key_r[:, None, :] - r[:, :, None] + (NB_ROWS - 1)
    cols = jnp.arange(GRID_W)
    cs = jnp.clip(cols - NB_COLS // 2, 0, GRID_W - NB_COLS)
    col_ok = (cols[None, :] >= cs[:, None]) & (cols[None, :] < cs[:, None] + NB_COLS)
    dc = cols[None, :] - cols[:, None] + (NB_COLS - 1)
    row_sel = ((dr[..., None] == jnp.arange(2 * NB_ROWS - 1)) & row_ok[..., None]).astype(F32)
    col_sel = ((dc[..., None] == jnp.arange(2 * NB_COLS - 1)) & col_ok[..., None]).astype(F32)
    bias = jnp.einsum('hde,aikd,cje->ahickj', rpb.astype(F32), row_sel, col_sel,
                      precision=lax.Precision.HIGHEST)
    ok = row_ok[:, None, :, None, :, None] & col_ok[None, None, None, :, None, :]
    full = jnp.where(ok, bias, NEG_INF)
    return full.reshape(3, B_HEADS, NA_BLOCK, 3 * NA_BLOCK)


def _neighbourhood_kernel(q_ref, kp_ref, ko_ref, kn_ref, vp_ref, vo_ref, vn_ref, bias_ref, o_ref):
    for h in range(B_HEADS):
        hs = slice(h * HEAD_DIM, (h + 1) * HEAD_DIM)
        kc = jnp.concatenate([kp_ref[:, hs], ko_ref[:, hs], kn_ref[:, hs]], axis=0)
        vc = jnp.concatenate([vp_ref[:, hs], vo_ref[:, hs], vn_ref[:, hs]], axis=0)
        s = lax.dot_general(q_ref[:, hs], kc, (((1,), (1,)), ((), ())), preferred_element_type=F32)
        s = s + bias_ref[0, h]
        m = jnp.max(s, axis=-1, keepdims=True)
        p = jnp.exp((s - m).astype(BF16))
        ov = jnp.dot(p, jnp.concatenate([vc, jnp.ones_like(vc)], axis=1), preferred_element_type=F32)
        o_ref[:, hs] = (ov[:, :HEAD_DIM] / ov[:, HEAD_DIM:]).astype(o_ref.dtype)


def _neighbourhood_attention(proj, bias_b, batch, seq):
    nb = seq // NA_BLOCK

    def cls(n):
        return jnp.where(n == 0, 0, jnp.where(n == nb - 1, 2, 1))

    def row(b, n):
        return b * nb + n

    prev = lambda b, n: row(b, jnp.maximum(n - 1, 0))
    nxt = lambda b, n: row(b, jnp.minimum(n + 1, nb - 1))
    spec = lambda rfn, col: pl.BlockSpec((NA_BLOCK, B_WIDTH), lambda b, n: (rfn(b, n), col))
    qcol, kcol, vcol = COL_QB // B_WIDTH, COL_KB // B_WIDTH, COL_VB // B_WIDTH
    return pl.pallas_call(
        _neighbourhood_kernel,
        grid=(batch, nb),
        in_specs=[
            spec(row, qcol),
            spec(prev, kcol), spec(row, kcol), spec(nxt, kcol),
            spec(prev, vcol), spec(row, vcol), spec(nxt, vcol),
            pl.BlockSpec((1, B_HEADS, NA_BLOCK, 3 * NA_BLOCK), lambda b, n: (cls(n), 0, 0, 0)),
        ],
        out_specs=pl.BlockSpec((NA_BLOCK, B_WIDTH), lambda b, n: (row(b, n), 0)),
        out_shape=jax.ShapeDtypeStruct((batch * seq, B_WIDTH), BF16),
        compiler_params=pltpu.CompilerParams(
            dimension_semantics=("parallel", "arbitrary"), vmem_limit_bytes=VMEM_LIMIT),
        name="neighbourhood_attention",
    )(proj, proj, proj, proj, proj, proj, proj, bias_b)


def _pack_bf16_pairs(a):
    n = a.shape[1] // 2
    bits = lax.bitcast_convert_type(a, jnp.uint32)
    return bits[:, n:] | (bits[:, :n] >> 16)


def _unpack_bf16_pairs(p):
    lo = lax.bitcast_convert_type(p << 16, F32).astype(BF16)
    hi = lax.bitcast_convert_type(p & jnp.uint32(0xFFFF0000), F32).astype(BF16)
    return lo, hi


def _merge_kernel(oa_ref, ob_ref, ga_ref, gb_ref, x_ref, wa_ref, wb_ref, wo_ref, g2_ref, wr_ref, br_ref,
                  upper_ref, lower_ref, h_ref, xs_ref, eb_ref, wcol_ref, hn_scr):
    s = pl.program_id(0)
    cur = lax.rem(s, 2)
    prev = 1 - cur
    tm = MERGE_TM

    @pl.when(s == 0)
    def _():
        hn_scr[...] = jnp.zeros_like(hn_scr)

    hn_split = hn_scr[prev]
    logits = _router_logits(hn_split, wr_ref, br_ref)
    ua = jnp.dot(oa_ref[...], wa_ref[...], preferred_element_type=F32)
    ub = jnp.dot(ob_ref[...], wb_ref[...], preferred_element_type=F32)
    ids, w1, w2 = _route(logits)
    lrow, eb = _local_sort(ids, upper_ref, lower_ref)
    merged = ga_ref[...].astype(F32) * ua + gb_ref[...].astype(F32) * ub
    acc = jnp.dot(merged.astype(BF16), wo_ref[...], preferred_element_type=F32)
    xs = _place(lrow, hn_split[:tm])
    h = x_ref[...] + acc
    h_ref[...] = h
    ms = jnp.mean(h * h, axis=-1, keepdims=True)
    hn = h * lax.rsqrt(ms + EPS) * g2_ref[...]
    hn_hi = hn.astype(BF16)
    hn_scr[cur, :tm, :] = hn_hi
    hn_scr[cur, tm:, :] = (hn - hn_hi.astype(F32)).astype(BF16)

    xs_ref[...] = _pack_bf16_pairs(xs)
    eb_ref[0] = eb
    riota = lax.broadcasted_iota(jnp.int32, (ROUTER_LANES, tm), 0)
    wrows = jnp.where(riota == 0, w1, jnp.where(riota == 1, w2,
                      jnp.where(riota == 2, lrow[:, :tm], jnp.where(riota == 3, lrow[:, tm:], 0.0))))
    wcol_ref[...] = wrows.T


def _router_logits(hn_split, wr_ref, br_ref):
    tm = hn_split.shape[0] // 2
    lg = jnp.dot(hn_split, wr_ref[...], preferred_element_type=F32)
    lg_hi, lg_lo = lg[:tm], lg[tm:]
    return lg_hi + pltpu.roll(lg_hi, ROUTER_LANES // 2, axis=1) + lg_lo + br_ref[...]


def _route(logits):
    tm = logits.shape[0]
    lt = logits.T
    gl = lt[N_EXPERTS:N_EXPERTS + N_GROUPS]
    gmax = jnp.max(gl, axis=0, keepdims=True)
    giota = lax.broadcasted_iota(jnp.int32, gl.shape, 0)
    g_idx = jnp.min(jnp.where(gl == gmax, giota, N_GROUPS), axis=0, keepdims=True)
    p_g = 1.0 / jnp.sum(jnp.exp(gl - gmax), axis=0, keepdims=True)
    e_sel = jnp.zeros((EXPERTS_PER_GROUP, tm), F32)
    for g in range(N_GROUPS):
        e_sel = jnp.where(g_idx == g, lt[g * EXPERTS_PER_GROUP:(g + 1) * EXPERTS_PER_GROUP], e_sel)
    eiota = lax.broadcasted_iota(jnp.int32, e_sel.shape, 0)
    m1 = jnp.max(e_sel, axis=0, keepdims=True)
    i1 = jnp.min(jnp.where(e_sel == m1, eiota, EXPERTS_PER_GROUP), axis=0, keepdims=True)
    rest = jnp.where(eiota == i1, -jnp.inf, e_sel)
    m2 = jnp.max(rest, axis=0, keepdims=True)
    i2 = jnp.min(jnp.where(rest == m2, eiota, EXPERTS_PER_GROUP), axis=0, keepdims=True)
    t = jnp.exp(m2 - m1)
    ids = jnp.concatenate([g_idx * EXPERTS_PER_GROUP + i1, g_idx * EXPERTS_PER_GROUP + i2], axis=1)
    return ids, p_g / (1.0 + t), p_g * t / (1.0 + t)


def _local_sort(ids, upper_ref, lower_ref):
    eiota32 = lax.broadcasted_iota(jnp.int32, (N_EXPERTS, ids.shape[1]), 0)
    onehot = jnp.where(eiota32 == ids, 1.0, 0.0)
    count = jnp.sum(onehot, axis=1, keepdims=True)
    before = jnp.dot(onehot.astype(BF16), upper_ref[...], preferred_element_type=F32)
    nblk = jnp.floor((count + (LOCAL_BLOCK - 1)) * (1.0 / LOCAL_BLOCK))
    boff = jnp.dot(lower_ref[...], jnp.broadcast_to(nblk, (N_EXPERTS, ROUTER_LANES)).astype(BF16),
                   preferred_element_type=F32)[:, 0:1]
    lrow = jnp.sum(onehot * (before + LOCAL_BLOCK * boff), axis=0, keepdims=True)
    biota = lax.broadcasted_iota(jnp.int32, (N_EXPERTS, ROUTER_LANES), 1).astype(F32)
    owns = jnp.where(jnp.logical_and(biota >= boff, biota < boff + nblk), 1.0, 0.0)
    eiota_f = lax.broadcasted_iota(jnp.int32, (N_EXPERTS, ROUTER_LANES), 0).astype(F32)
    eb = jnp.sum(owns * eiota_f, axis=0, keepdims=True) + N_EXPERTS * (1.0 - jnp.sum(owns, axis=0, keepdims=True))
    return lrow, eb.astype(jnp.int32)


def _place(lrow, hn_hi):
    tm = hn_hi.shape[0]
    jiota = lax.broadcasted_iota(jnp.int32, (LOCAL_ROWS, tm), 0).astype(F32)
    place = jnp.where(jiota == lrow[:, :tm], 1.0, jnp.where(jiota == lrow[:, tm:], 1.0, 0.0)).astype(BF16)
    return jnp.dot(place, hn_hi, preferred_element_type=F32)


def _merge_and_route(oa, ob, proj, x2, wa, wb, wo, g2, wr_cat, br):
    t = x2.shape[0]
    n_assign = TOP_K * MERGE_TM
    upper = (jnp.arange(n_assign)[:, None] < jnp.arange(n_assign)[None, :]).astype(BF16)
    lower = (jnp.arange(N_EXPERTS)[:, None] > jnp.arange(N_EXPERTS)[None, :]).astype(BF16)
    const = lambda shape: pl.BlockSpec(shape, lambda s: (0, 0), pipeline_mode=pl.Buffered(1))
    n_tiles = t // MERGE_TM
    cur = lambda s: jnp.minimum(s, n_tiles - 1)
    prev = lambda s: jnp.maximum(s - 1, 0)
    return pl.pallas_call(
        _merge_kernel,
        grid=(n_tiles + 1,),
        in_specs=[
            pl.BlockSpec((MERGE_TM, A_WIDTH), lambda s: (cur(s), 0)),
            pl.BlockSpec((MERGE_TM, B_WIDTH), lambda s: (cur(s), 0)),
            pl.BlockSpec((MERGE_TM, D_MODEL), lambda s: (cur(s), COL_GA // D_MODEL)),
            pl.BlockSpec((MERGE_TM, D_MODEL), lambda s: (cur(s), COL_GB // D_MODEL)),
            pl.BlockSpec((MERGE_TM, D_MODEL), lambda s: (cur(s), 0)),
            const((A_WIDTH, D_MODEL)), const((B_WIDTH, D_MODEL)), const((D_MODEL, D_MODEL)),
            const((1, D_MODEL)), const((D_MODEL, ROUTER_LANES)), const((1, ROUTER_LANES)),
            const((n_assign, n_assign)), const((N_EXPERTS, N_EXPERTS)),
        ],
        out_specs=[
            pl.BlockSpec((MERGE_TM, D_MODEL), lambda s: (cur(s), 0)),
            pl.BlockSpec((LOCAL_ROWS, D_MODEL // 2), lambda s: (prev(s), 0)),
            pl.BlockSpec((1, 1, ROUTER_LANES), lambda s: (prev(s), 0, 0)),
            pl.BlockSpec((MERGE_TM, ROUTER_LANES), lambda s: (prev(s), 0)),
        ],
        scratch_shapes=[pltpu.VMEM((2, 2 * MERGE_TM, D_MODEL), BF16)],
        out_shape=[
            jax.ShapeDtypeStruct((t, D_MODEL), F32),
            jax.ShapeDtypeStruct((t // MERGE_TM * LOCAL_ROWS, D_MODEL // 2), jnp.uint32),
            jax.ShapeDtypeStruct((t // MERGE_TM, 1, ROUTER_LANES), jnp.int32),
            jax.ShapeDtypeStruct((t, ROUTER_LANES), F32),
        ],
        compiler_params=pltpu.CompilerParams(
            dimension_semantics=("arbitrary",), vmem_limit_bytes=VMEM_LIMIT),
        name="merge_and_route",
    )(oa, ob, proj, proj, x2, wa, wb, wo, g2, wr_cat, br, upper, lower)


def _expert_kernel(blk_ref, texp_ref, tcb_ref, tnb_ref, wslot_ref, wnext_ref, nused_ref,
                   xs_hbm, wg_hbm, wu_hbm, wd_hbm, y_hbm,
                   xbuf, ybuf, wgf, wuf, wdf, wgb, wub, wdb, gsem, ssem, wsem):
    n_used = nused_ref[0]

    def weight_copies(e, ws):
        return [pltpu.make_async_copy(src.at[e], dst.at[ws], wsem.at[ws])
                for src, dst in ((wg_hbm, wgf), (wu_hbm, wuf), (wd_hbm, wdf))]

    def gather_copy(blk, dst_slot, j):
        return pltpu.make_async_copy(xs_hbm.at[blk], xbuf.at[dst_slot, pl.ds(j * LOCAL_BLOCK, LOCAL_BLOCK)],
                                     gsem.at[dst_slot])

    def scatter_copy(blk, src_slot, j):
        return pltpu.make_async_copy(ybuf.at[src_slot, pl.ds(j * LOCAL_BLOCK, LOCAL_BLOCK)], y_hbm.at[blk],
                                     ssem.at[src_slot])

    def issue_gather(tile, dst_slot):
        @pl.when(texp_ref[tile] < N_EXPERTS)
        def _():
            base = tcb_ref[tile]
            for j in range(MOE_BLOCKS):
                gather_copy(blk_ref[base + j], dst_slot, j).start()

    def wait_gather(tile, s):
        @pl.when(texp_ref[tile] < N_EXPERTS)
        def _():
            for j in range(MOE_BLOCKS):
                gather_copy(0, s, j).wait()

    def for_valid_blocks(tile, fn):
        n_valid = tnb_ref[tile]

        @pl.when(n_valid == MOE_BLOCKS)
        def _():
            for j in range(MOE_BLOCKS):
                fn(j)

        @pl.when(n_valid < MOE_BLOCKS)
        def _():
            for j in range(MOE_BLOCKS):
                pl.when(j < n_valid)(functools.partial(fn, j))

    def issue_scatter(tile, src_slot):
        base = tcb_ref[tile]
        for_valid_blocks(tile, lambda j: scatter_copy(blk_ref[base + j], src_slot, j).start())

    def wait_scatter(tile, s):
        for_valid_blocks(tile, lambda j: scatter_copy(0, s, j).wait())

    @pl.when(n_used > 0)
    def _():
        issue_gather(0, 0)

        @pl.when(texp_ref[0] < N_EXPERTS)
        def _():
            for cp in weight_copies(texp_ref[0], wslot_ref[0]):
                cp.start()

    def tile_step(i, carry):
        slot = lax.rem(i, 2)
        expert = texp_ref[i]
        wait_gather(i, slot)

        @pl.when(i + 1 < n_used)
        def _():
            issue_gather(i + 1, 1 - slot)

        @pl.when(i >= 2)
        def _():
            wait_scatter(i - 2, slot)

        @pl.when(jnp.logical_and(expert < N_EXPERTS, jnp.logical_or(i == 0, expert != texp_ref[jnp.maximum(i - 1, 0)])))
        def _():
            ws = wslot_ref[i]
            for cp in weight_copies(expert, ws):
                cp.wait()
            for static_ws in (0, 1):
                @pl.when(ws == static_ws)
                def _():
                    wgb[...] = wgf[static_ws].astype(BF16)
                    wub[...] = wuf[static_ws].astype(BF16)
                    wdb[...] = wdf[static_ws].astype(BF16)
            nxt = wnext_ref[i]

            @pl.when(nxt < N_EXPERTS)
            def _():
                for cp in weight_copies(nxt, 1 - ws):
                    cp.start()

        @pl.when(expert < N_EXPERTS)
        def _():
            x_lo, x_hi = _unpack_bf16_pairs(xbuf[slot])
            half = D_MODEL // 2
            g = (jnp.dot(x_lo, wgb[:half, :], preferred_element_type=F32)
                 + jnp.dot(x_hi, wgb[half:, :], preferred_element_type=F32))
            u = (jnp.dot(x_lo, wub[:half, :], preferred_element_type=F32)
                 + jnp.dot(x_hi, wub[half:, :], preferred_element_type=F32))
            act = (g * jax.nn.sigmoid(g) * u).astype(BF16)
            y = jnp.dot(act, wdb[...], preferred_element_type=F32)
            ybuf[slot] = _pack_bf16_pairs(y.astype(BF16).astype(F32))

        @pl.when(expert >= N_EXPERTS)
        def _():
            ybuf[slot] = jnp.zeros((MOE_TM, D_MODEL // 2), jnp.uint32)

        issue_scatter(i, slot)
        return carry

    lax.fori_loop(0, n_used, tile_step, 0)

    for back in (1, 2):
        @pl.when(n_used >= back)
        def _():
            last = jnp.maximum(n_used - back, 0)
            wait_scatter(last, lax.rem(last, 2))


def _expert_ffn(plan, xs_local, wg, wu, wd):
    n_blocks = xs_local.shape[0] // LOCAL_BLOCK
    half = D_MODEL // 2
    any_spec = pl.BlockSpec(memory_space=pl.ANY)
    grid_spec = pltpu.PrefetchScalarGridSpec(
        num_scalar_prefetch=len(plan),
        grid=(1,),
        in_specs=[any_spec, any_spec, any_spec, any_spec],
        out_specs=any_spec,
        scratch_shapes=[
            pltpu.VMEM((2, MOE_TM, half), jnp.uint32), pltpu.VMEM((2, MOE_TM, half), jnp.uint32),
            pltpu.VMEM((2, D_MODEL, D_EXPERT), F32), pltpu.VMEM((2, D_MODEL, D_EXPERT), F32),
            pltpu.VMEM((2, D_EXPERT, D_MODEL), F32),
            pltpu.VMEM((D_MODEL, D_EXPERT), BF16), pltpu.VMEM((D_MODEL, D_EXPERT), BF16),
            pltpu.VMEM((D_EXPERT, D_MODEL), BF16),
            pltpu.SemaphoreType.DMA((2,)), pltpu.SemaphoreType.DMA((2,)), pltpu.SemaphoreType.DMA((2,)),
        ],
    )
    return pl.pallas_call(
        _expert_kernel,
        grid_spec=grid_spec,
        out_shape=jax.ShapeDtypeStruct((n_blocks, LOCAL_BLOCK, half), jnp.uint32),
        compiler_params=pltpu.CompilerParams(
            dimension_semantics=("arbitrary",), vmem_limit_bytes=VMEM_LIMIT),
        name="expert_ffn",
    )(*plan, xs_local.reshape(n_blocks, LOCAL_BLOCK, half), wg, wu, wd)


def _combine_kernel(h_ref, wcol_ref, y_ref, o_ref):
    y_lo, y_hi = _unpack_bf16_pairs(y_ref[...])
    w = wcol_ref[...]
    jiota = lax.broadcasted_iota(jnp.int32, (MERGE_TM, LOCAL_ROWS), 1).astype(F32)
    pick = (jnp.where(jiota == w[:, 2:3], w[:, 0:1], 0.0) + jnp.where(jiota == w[:, 3:4], w[:, 1:2], 0.0)).astype(BF16)
    half = D_MODEL // 2
    for cols, y in ((slice(0, half), y_lo), (slice(half, D_MODEL), y_hi)):
        o_ref[:, cols] = h_ref[:, cols] + jnp.dot(pick, y, preferred_element_type=F32)


def _combine(h, wcol, y_local):
    t = h.shape[0]
    return pl.pallas_call(
        _combine_kernel,
        grid=(t // MERGE_TM,),
        in_specs=[
            pl.BlockSpec((MERGE_TM, D_MODEL), lambda i: (i, 0)),
            pl.BlockSpec((MERGE_TM, ROUTER_LANES), lambda i: (i, 0)),
            pl.BlockSpec((LOCAL_ROWS, D_MODEL // 2), lambda i: (i, 0)),
        ],
        out_specs=pl.BlockSpec((MERGE_TM, D_MODEL), lambda i: (i, 0)),
        out_shape=jax.ShapeDtypeStruct((t, D_MODEL), F32),
        compiler_params=pltpu.CompilerParams(
            dimension_semantics=("parallel",), vmem_limit_bytes=VMEM_LIMIT),
        name="combine",
    )(h, wcol, y_local)


def _dispatch_plan(block_expert):
    n_blocks = block_expert.size
    n_keys = N_EXPERTS + 1
    n_tiles = n_blocks // MOE_BLOCKS + n_keys
    keys = block_expert.reshape(n_blocks)
    order = jnp.argsort(keys, stable=True).astype(jnp.int32)
    counts = jnp.sum(keys[None, :] == jnp.arange(n_keys, dtype=jnp.int32)[:, None], axis=1).astype(jnp.int32)
    tiles_per = (counts + MOE_BLOCKS - 1) // MOE_BLOCKS
    tile_end = jnp.cumsum(tiles_per)
    tile_first = tile_end - tiles_per
    start = jnp.cumsum(counts) - counts
    tile = jnp.arange(n_tiles, dtype=jnp.int32)
    tile_expert = jnp.minimum(jnp.sum(tile_end[None, :] <= tile[:, None], axis=1), n_keys - 1).astype(jnp.int32)
    sel = (tile_expert[:, None] == jnp.arange(n_keys, dtype=jnp.int32)[None, :]).astype(jnp.int32)
    pick = lambda v: jnp.sum(sel * v[None, :], axis=1)
    within = (tile - pick(tile_first)) * MOE_BLOCKS
    tile_base = jnp.clip(pick(start) + within, 0, n_blocks).astype(jnp.int32)
    tile_blocks = jnp.clip(pick(counts) - within, 0, MOE_BLOCKS).astype(jnp.int32)
    n_used = tile_end[-1].astype(jnp.int32).reshape(1)
    order = jnp.concatenate([order, jnp.zeros((MOE_BLOCKS,), jnp.int32)])
    key = jnp.arange(n_keys, dtype=jnp.int32)
    present = jnp.logical_and(counts > 0, key < N_EXPERTS)
    slot_of = (jnp.cumsum(present) - present) % 2
    later = jnp.logical_and(key[None, :] > key[:, None], present[None, :])
    next_of = jnp.min(jnp.where(later, key[None, :], N_EXPERTS), axis=1)
    return (order, tile_expert, tile_base, tile_blocks, pick(slot_of).astype(jnp.int32),
            pick(next_of).astype(jnp.int32), n_used)


def _layer(x, norm1_g, w_in, q_norm_a, k_norm_a, sink_a, q_norm_b, k_norm_b, rpb_b, w_up_a, w_up_b, w_out,
           rel_table, norm2_g, w_rg, b_rg, w_re, b_re, w_gate, w_up, w_down):
    batch, seq, d = x.shape
    t = batch * seq
    x2 = x.reshape(t, d)

    cuts = [0, A_WIDTH, A_WIDTH + A_KV_WIDTH, A_WIDTH + 2 * A_KV_WIDTH]
    cuts += [cuts[-1] + B_WIDTH, cuts[-1] + 2 * B_WIDTH, cuts[-1] + 3 * B_WIDTH]
    cuts += [cuts[-1] + D_MODEL, cuts[-1] + 2 * D_MODEL]
    qa, ka, va, qb, kb, vb, ga, gb = [w_in[:, cuts[n]:cuts[n + 1]] for n in range(8)]
    w_cat = jnp.concatenate([qa, qb, kb, vb, ga, gb, ka, va], axis=1).astype(BF16)
    scale = HEAD_DIM ** -0.5
    ones = lambda n: jnp.ones((n,), F32)
    gain_cat = jnp.concatenate([
        jnp.tile(q_norm_a.astype(F32) * scale, A_HEADS), jnp.tile(q_norm_b.astype(F32) * scale, B_HEADS),
        jnp.tile(k_norm_b.astype(F32), B_HEADS), ones(B_WIDTH + 2 * D_MODEL),
        jnp.tile(k_norm_a.astype(F32), A_KV_HEADS), ones(A_KV_WIDTH)]).reshape(1, IN_COLS)

    kind = lambda v, n: jnp.full((n,), v, F32)
    kind_cat = jnp.concatenate([kind(1.0, COL_VB), kind(0.0, B_WIDTH), kind(2.0, 2 * D_MODEL),
                                kind(1.0, A_KV_WIDTH), kind(0.0, A_KV_WIDTH)]).reshape(1, IN_COLS)

    proj = _in_projection(x2, norm1_g.astype(F32).reshape(1, d), w_cat, gain_cat, kind_cat)
    oa = _window_attention(proj, _window_bias(rel_table), sink_a.astype(F32), batch, seq)
    ob = _neighbourhood_attention(proj, _neighbourhood_bias(rpb_b, seq // GRID_W), batch, seq)

    wr = jnp.concatenate([w_re.astype(F32), w_rg.astype(F32)], axis=1)
    wr_hi = wr.astype(BF16)
    wr_lo = (wr - wr_hi.astype(F32)).astype(BF16)
    half = ROUTER_LANES // 2
    padc = lambda a: jnp.pad(a, ((0, 0), (0, half - a.shape[1])))
    wr_cat = jnp.concatenate([padc(wr_hi), padc(wr_lo)], axis=1)
    br = jnp.pad(jnp.concatenate([b_re.astype(F32), b_rg.astype(F32)]), (0, ROUTER_LANES - N_EXPERTS - N_GROUPS))

    h, xs_local, block_expert, wcol = _merge_and_route(
        oa, ob, proj, x2, w_up_a.astype(BF16), w_up_b.astype(BF16), w_out.astype(BF16),
        norm2_g.astype(F32).reshape(1, d), wr_cat, br.reshape(1, ROUTER_LANES))

    plan = _dispatch_plan(block_expert[:, 0, :LOCAL_BLOCKS])
    y_local = _expert_ffn(plan, xs_local, w_gate.astype(F32), w_up.astype(F32), w_down.astype(F32))
    out = _combine(h, wcol, y_local.reshape(-1, d // 2))
    return out.reshape(batch, seq, d)


def kernel(x, norm1_g, w_in, q_norm_a, k_norm_a, sink_a, q_norm_b, k_norm_b, rpb_b, w_up_a, w_up_b, w_out,
           rel_bias_table, norm2_g, w_router_group, b_router_group, w_router_expert, b_router_expert,
           w_gate, w_up, w_down):
    return _layer(x, norm1_g[0], w_in[0], q_norm_a[0], k_norm_a[0], sink_a[0], q_norm_b[0], k_norm_b[0],
                  rpb_b[0], w_up_a[0], w_up_b[0], w_out[0], rel_bias_table, norm2_g[0],
                  w_router_group[0], b_router_group[0], w_router_expert[0], b_router_expert[0],
                  w_gate[0], w_up[0], w_down[0])
```

```python
import functools
import math

import jax
import jax.numpy as jnp
from jax import lax
from jax.experimental import pallas as pl
from jax.experimental.pallas import tpu as pltpu

F32 = jnp.float32
BF16 = jnp.bfloat16

D_MODEL = 2048
HEAD_DIM = 128
A_HEADS = 8
A_KV_HEADS = 2
A_GROUP = A_HEADS // A_KV_HEADS
A_WIDTH = A_HEADS * HEAD_DIM
A_KV_WIDTH = A_KV_HEADS * HEAD_DIM
WINDOW = 128
A_BLOCK = 128
NUM_BUCKETS = 32
MAX_DISTANCE = 128
B_HEADS = 8
B_WIDTH = B_HEADS * HEAD_DIM
GRID_W = 64
NB_ROWS = 8
NB_COLS = 16
IN_COLS = A_WIDTH + 2 * A_KV_WIDTH + 3 * B_WIDTH + 2 * D_MODEL
N_GROUPS = 4
EXPERTS_PER_GROUP = 8
N_EXPERTS = N_GROUPS * EXPERTS_PER_GROUP
TOP_K = 2
D_EXPERT = 512
EPS = 1e-6
NEG_INF = -1e30

COL_QA = 0
COL_QB = COL_QA + A_WIDTH
COL_KB = COL_QB + B_WIDTH
COL_VB = COL_KB + B_WIDTH
COL_GA = COL_VB + B_WIDTH
COL_GB = COL_GA + D_MODEL
COL_KA = COL_GB + D_MODEL
COL_VA = COL_KA + A_KV_WIDTH

PROJ_TM = 1024
PROJ_TN = 512
PROJ_SUB = 128
WIN_STEP = 4
NA_ROWS = 4
NA_BLOCK = NA_ROWS * GRID_W
MERGE_TM = 256
ROUTER_LANES = 128
LOCAL_BLOCK = 8
LOCAL_ROWS = 768
LOCAL_BLOCKS = LOCAL_ROWS // LOCAL_BLOCK
assert LOCAL_ROWS >= TOP_K * MERGE_TM + N_EXPERTS * (LOCAL_BLOCK - 1) and LOCAL_BLOCKS <= ROUTER_LANES
MOE_TM = 256
MOE_BLOCKS = MOE_TM // LOCAL_BLOCK
VMEM_LIMIT = 56 * 1024 * 1024


def _head_norm(a, gain):
    outs = []
    for hd in range(a.shape[1] // HEAD_DIM):
        sl = slice(hd * HEAD_DIM, (hd + 1) * HEAD_DIM)
        ah = a[:, sl]
        ss = jnp.sum(ah * ah, axis=-1, keepdims=True)
        outs.append(ah * lax.rsqrt(ss * (1.0 / HEAD_DIM) + EPS) * gain[:, sl])
    return outs[0] if len(outs) == 1 else jnp.concatenate(outs, axis=1)


def _inproj_kernel(x_ref, g1_ref, w_ref, gain_ref, kind_ref, o_ref, xn_ref):
    j = pl.program_id(1)

    def project(normalise_first):
        kind = kind_ref[...]
        for r in range(PROJ_TM // PROJ_SUB):
            rows = slice(r * PROJ_SUB, (r + 1) * PROJ_SUB)
            if normalise_first:
                x = x_ref[rows, :]
                ms = jnp.mean(x * x, axis=-1, keepdims=True)
                xn_ref[rows, :] = (x * lax.rsqrt(ms + EPS) * g1_ref[...]).astype(BF16)
            acc = jnp.dot(xn_ref[rows, :], w_ref[...], preferred_element_type=F32)
            out = jnp.where(kind == 1.0, _head_norm(acc, gain_ref[...]),
                            jnp.where(kind == 2.0, jax.nn.sigmoid(acc), acc))
            o_ref[rows, :] = out.astype(o_ref.dtype)

    pl.when(j == 0)(functools.partial(project, True))
    pl.when(j != 0)(functools.partial(project, False))


def _in_projection(x2, g1, w_cat, gain_cat, kind_cat):
    t = x2.shape[0]
    return pl.pallas_call(
        _inproj_kernel,
        grid=(t // PROJ_TM, IN_COLS // PROJ_TN),
        in_specs=[
            pl.BlockSpec((PROJ_TM, D_MODEL), lambda i, j: (i, 0)),
            pl.BlockSpec((1, D_MODEL), lambda i, j: (0, 0)),
            pl.BlockSpec((D_MODEL, PROJ_TN), lambda i, j: (0, j)),
            pl.BlockSpec((1, PROJ_TN), lambda i, j: (0, j)),
            pl.BlockSpec((1, PROJ_TN), lambda i, j: (0, j)),
        ],
        out_specs=pl.BlockSpec((PROJ_TM, PROJ_TN), lambda i, j: (i, j)),
        out_shape=jax.ShapeDtypeStruct((t, IN_COLS), BF16),
        scratch_shapes=[pltpu.VMEM((PROJ_TM, D_MODEL), BF16)],
        compiler_params=pltpu.CompilerParams(
            dimension_semantics=("parallel", "arbitrary"), vmem_limit_bytes=VMEM_LIMIT),
        name="in_projection",
    )(x2, g1, w_cat, gain_cat, kind_cat)


def _t5_bucket(rel):
    nb = NUM_BUCKETS // 2
    ret = jnp.where(rel > 0, nb, 0)
    n = jnp.abs(rel)
    max_exact = nb // 2
    nf = jnp.maximum(n, 1).astype(jnp.float32)
    large = max_exact + (jnp.log(nf / max_exact) / math.log(MAX_DISTANCE / max_exact)
                         * (nb - max_exact)).astype(jnp.int32)
    large = jnp.minimum(large, nb - 1)
    return ret + jnp.where(n < max_exact, n, large)


def _window_bias(rel_table):
    q_loc = jnp.arange(A_BLOCK)
    k_loc = jnp.arange(3 * A_BLOCK) - WINDOW
    rel = k_loc[None, :] - q_loc[:, None]
    onehot = (_t5_bucket(rel)[..., None] == jnp.arange(NUM_BUCKETS)).astype(F32)
    bias = jnp.einsum('qkn,nh->hqk', onehot, rel_table.astype(F32), precision=lax.Precision.HIGHEST)
    in_window = jnp.abs(rel) <= WINDOW
    has_prev = jnp.array([False, True, True])[:, None, None]
    has_next = jnp.array([True, True, False])[:, None, None]
    k_blk = (jnp.arange(3 * A_BLOCK) // A_BLOCK)[None, None, :]
    valid = in_window[None] & ((k_blk != 0) | has_prev) & ((k_blk != 2) | has_next)
    full = jnp.where(valid[:, None], bias[None], NEG_INF)
    return full.reshape(3, A_KV_HEADS, A_GROUP * A_BLOCK, 3 * A_BLOCK)


def _window_kernel(sink_ref, q_ref, kp_ref, ko_ref, kn_ref, vp_ref, vo_ref, vn_ref,
                   bias_lo_ref, bias_mid_ref, bias_hi_ref, o_ref):
    for kvh in range(A_KV_HEADS):
        hs = slice(kvh * HEAD_DIM, (kvh + 1) * HEAD_DIM)
        heads = [kvh * A_GROUP + g for g in range(A_GROUP)]
        kc = jnp.concatenate([kp_ref[:, hs], ko_ref[:, hs], kn_ref[:, hs]], axis=0)
        vc = jnp.concatenate([vp_ref[:, hs], vo_ref[:, hs], vn_ref[:, hs]], axis=0)
        vc1 = jnp.concatenate([vc, jnp.ones_like(vc)], axis=1)
        sink = jnp.concatenate([jnp.full((A_BLOCK, 1), sink_ref[h], F32) for h in heads], axis=0)
        for jq in range(WIN_STEP):
            rows = slice(jq * A_BLOCK, (jq + 1) * A_BLOCK)
            band = slice(jq * A_BLOCK, (jq + 3) * A_BLOCK)
            bias_ref = bias_lo_ref if jq == 0 else (bias_hi_ref if jq == WIN_STEP - 1 else bias_mid_ref)
            q4 = jnp.concatenate([q_ref[rows, h * HEAD_DIM:(h + 1) * HEAD_DIM] for h in heads], axis=0)
            s = lax.dot_general(q4, kc[band], (((1,), (1,)), ((), ())), preferred_element_type=F32)
            s = s + bias_ref[0, kvh]
            m = jnp.max(s, axis=-1, keepdims=True)
            p = jnp.exp((s - m).astype(BF16))
            ov = jnp.dot(p, vc1[band], preferred_element_type=F32)
            o = ov[:, :HEAD_DIM] / (ov[:, HEAD_DIM:] + jnp.exp(sink - m))
            for g, h in enumerate(heads):
                o_ref[rows, h * HEAD_DIM:(h + 1) * HEAD_DIM] = o[g * A_BLOCK:(g + 1) * A_BLOCK].astype(o_ref.dtype)


def _window_attention(proj, bias_a, sink, batch, seq):
    nb = seq // A_BLOCK
    ns = nb // WIN_STEP
    kcol, vcol = COL_KA // A_KV_WIDTH, COL_VA // A_KV_WIDTH
    prev = lambda b, n: b * nb + jnp.maximum(n * WIN_STEP - 1, 0)
    nxt = lambda b, n: b * nb + jnp.minimum((n + 1) * WIN_STEP, nb - 1)
    own = lambda b, n: b * ns + n
    edge = lambda rfn, col: pl.BlockSpec((A_BLOCK, A_KV_WIDTH), lambda b, n: (rfn(b, n), col))
    main = lambda col: pl.BlockSpec((WIN_STEP * A_BLOCK, A_KV_WIDTH), lambda b, n: (own(b, n), col))
    bias_block = (1, A_KV_HEADS, A_GROUP * A_BLOCK, 3 * A_BLOCK)
    return pl.pallas_call(
        _window_kernel,
        grid=(batch, ns),
        in_specs=[
            pl.BlockSpec(memory_space=pltpu.SMEM),
            pl.BlockSpec((WIN_STEP * A_BLOCK, A_WIDTH), lambda b, n: (own(b, n), COL_QA // A_WIDTH)),
            edge(prev, kcol), main(kcol), edge(nxt, kcol),
            edge(prev, vcol), main(vcol), edge(nxt, vcol),
            pl.BlockSpec(bias_block, lambda b, n: (jnp.where(n == 0, 0, 1), 0, 0, 0)),
            pl.BlockSpec(bias_block, lambda b, n: (1, 0, 0, 0)),
            pl.BlockSpec(bias_block, lambda b, n: (jnp.where(n == ns - 1, 2, 1), 0, 0, 0)),
        ],
        out_specs=pl.BlockSpec((WIN_STEP * A_BLOCK, A_WIDTH), lambda b, n: (own(b, n), 0)),
        out_shape=jax.ShapeDtypeStruct((batch * seq, A_WIDTH), BF16),
        compiler_params=pltpu.CompilerParams(
            dimension_semantics=("parallel", "arbitrary"), vmem_limit_bytes=VMEM_LIMIT),
        name="window_attention",
    )(sink, proj, proj, proj, proj, proj, proj, proj, bias_a, bias_a, bias_a)


def _neighbourhood_col_bias(rpb):
    cols = jnp.arange(GRID_W)
    cs = jnp.clip(cols - NB_COLS // 2, 0, GRID_W - NB_COLS)
    col_ok = (cols[None, :] >= cs[:, None]) & (cols[None, :] < cs[:, None] + NB_COLS)
    dc = cols[None, :] - cols[:, None] + (NB_COLS - 1)
    col_sel = ((dc[..., None] == jnp.arange(2 * NB_COLS - 1)) & col_ok[..., None]).astype(F32)
    by_col = jnp.einsum('hde,cje->hdcj', rpb.astype(F32), col_sel, precision=lax.Precision.HIGHEST)
    return jnp.where(col_ok[None, None], by_col, NEG_INF)


def _neighbourhood_row_plan(cls, rows):
    kh = min(NB_ROWS, rows)
    r0 = (0, NA_ROWS, rows - NA_ROWS)[cls]
    plan = []
    for i in range(NA_ROWS):
        r = r0 + i
        rs = min(max(r - kh // 2, 0), rows - kh)
        plan.append([key_r - r + NB_ROWS - 1 if rs <= key_r < rs + kh else None
                     for key_r in range(r0 - NA_ROWS, r0 + 2 * NA_ROWS)])
    return plan


def _neighbourhood_kernel(q_ref, kp_ref, ko_ref, kn_ref, vp_ref, vo_ref, vn_ref, colbias_ref, o_ref, bias_ref,
                          *, rows):
    n = pl.program_id(1)
    n_blocks = pl.num_programs(1)

    def build_bias(cls):
        plan = _neighbourhood_row_plan(cls, rows)
        masked = jnp.full((GRID_W, GRID_W), NEG_INF, F32)
        for h in range(B_HEADS):
            for i in range(NA_ROWS):
                for pair in range(3 * NA_ROWS // 2):
                    tiles = [masked if plan[i][kr] is None else colbias_ref[h, plan[i][kr]]
                             for kr in (2 * pair, 2 * pair + 1)]
                    bias_ref[h, i * GRID_W:(i + 1) * GRID_W, pair * 2 * GRID_W:(pair + 1) * 2 * GRID_W] = (
                        jnp.concatenate(tiles, axis=1))

    pl.when(n == 0)(functools.partial(build_bias, 0))
    pl.when(n == 1)(functools.partial(build_bias, 1))
    pl.when(n == n_blocks - 1)(functools.partial(build_bias, 2))

    for h in range(B_HEADS):
        hs = slice(h * HEAD_DIM, (h + 1) * HEAD_DIM)
        kc = jnp.concatenate([kp_ref[:, hs], ko_ref[:, hs], kn_ref[:, hs]], axis=0)
        vc = jnp.concatenate([vp_ref[:, hs], vo_ref[:, hs], vn_ref[:, hs]], axis=0)
        s = lax.dot_general(q_ref[:, hs], kc, (((1,), (1,)), ((), ())), preferred_element_type=F32)
        s = s + bias_ref[h]
        m = jnp.max(s, axis=-1, keepdims=True)
        p = jnp.exp((s - m).astype(BF16))
        ov = jnp.dot(p, jnp.concatenate([vc, jnp.ones_like(vc)], axis=1), preferred_element_type=F32)
        o_ref[:, hs] = (ov[:, :HEAD_DIM] / ov[:, HEAD_DIM:]).astype(o_ref.dtype)


def _neighbourhood_attention(proj, col_bias, batch, seq):
    nb = seq // NA_BLOCK
    assert nb >= 2

    def row(b, n):
        return b * nb + n

    prev = lambda b, n: row(b, jnp.maximum(n - 1, 0))
    nxt = lambda b, n: row(b, jnp.minimum(n + 1, nb - 1))
    spec = lambda rfn, col: pl.BlockSpec((NA_BLOCK, B_WIDTH), lambda b, n: (rfn(b, n), col))
    qcol, kcol, vcol = COL_QB // B_WIDTH, COL_KB // B_WIDTH, COL_VB // B_WIDTH
    return pl.pallas_call(
        functools.partial(_neighbourhood_kernel, rows=seq // GRID_W),
        grid=(batch, nb),
        in_specs=[
            spec(row, qcol),
            spec(prev, kcol), spec(row, kcol), spec(nxt, kcol),
            spec(prev, vcol), spec(row, vcol), spec(nxt, vcol),
            pl.BlockSpec(col_bias.shape, lambda b, n: (0, 0, 0, 0), pipeline_mode=pl.Buffered(1)),
        ],
        out_specs=pl.BlockSpec((NA_BLOCK, B_WIDTH), lambda b, n: (row(b, n), 0)),
        out_shape=jax.ShapeDtypeStruct((batch * seq, B_WIDTH), BF16),
        scratch_shapes=[pltpu.VMEM((B_HEADS, NA_BLOCK, 3 * NA_BLOCK), F32)],
        compiler_params=pltpu.CompilerParams(
            dimension_semantics=("arbitrary", "arbitrary"), vmem_limit_bytes=VMEM_LIMIT),
        name="neighbourhood_attention",
    )(proj, proj, proj, proj, proj, proj, proj, col_bias)


def _pack_bf16_pairs(a):
    n = a.shape[1] // 2
    bits = lax.bitcast_convert_type(a, jnp.uint32)
    return bits[:, n:] | (bits[:, :n] >> 16)


def _unpack_bf16_pairs(p):
    lo = lax.bitcast_convert_type(p << 16, F32).astype(BF16)
    hi = lax.bitcast_convert_type(p & jnp.uint32(0xFFFF0000), F32).astype(BF16)
    return lo, hi


def _merge_kernel(oa_ref, ob_ref, ga_ref, gb_ref, x_ref, wa_ref, wb_ref, wo_ref, g2_ref, wr_ref, br_ref,
                  upper_ref, lower_ref, h_ref, xs_ref, eb_ref, wcol_ref, hn_scr):
    s = pl.program_id(0)
    cur = lax.rem(s, 2)
    prev = 1 - cur
    tm = MERGE_TM

    @pl.when(s == 0)
    def _():
        hn_scr[...] = jnp.zeros_like(hn_scr)

    hn_split = hn_scr[prev]
    logits = _router_logits(hn_split, wr_ref, br_ref)
    ua = jnp.dot(oa_ref[...], wa_ref[...], preferred_element_type=F32)
    ub = jnp.dot(ob_ref[...], wb_ref[...], preferred_element_type=F32)
    ids, w1, w2 = _route(logits)
    lrow, eb = _local_sort(ids, upper_ref, lower_ref)
    merged = ga_ref[...].astype(F32) * ua + gb_ref[...].astype(F32) * ub
    acc = jnp.dot(merged.astype(BF16), wo_ref[...], preferred_element_type=F32)
    xs = _place(lrow, hn_split[:tm])
    h = x_ref[...] + acc
    h_ref[...] = h
    ms = jnp.mean(h * h, axis=-1, keepdims=True)
    hn = h * lax.rsqrt(ms + EPS) * g2_ref[...]
    hn_hi = hn.astype(BF16)
    hn_scr[cur, :tm, :] = hn_hi
    hn_scr[cur, tm:, :] = (hn - hn_hi.astype(F32)).astype(BF16)

    xs_ref[...] = _pack_bf16_pairs(xs)
    eb_ref[0] = eb
    riota = lax.broadcasted_iota(jnp.int32, (ROUTER_LANES, tm), 0)
    wrows = jnp.where(riota == 0, w1, jnp.where(riota == 1, w2,
                      jnp.where(riota == 2, lrow[:, :tm], jnp.where(riota == 3, lrow[:, tm:], 0.0))))
    wcol_ref[...] = wrows.T


def _router_logits(hn_split, wr_ref, br_ref):
    tm = hn_split.shape[0] // 2
    lg = jnp.dot(hn_split, wr_ref[...], preferred_element_type=F32)
    lg_hi, lg_lo = lg[:tm], lg[tm:]
    return lg_hi + pltpu.roll(lg_hi, ROUTER_LANES // 2, axis=1) + lg_lo + br_ref[...]


def _route(logits):
    tm = logits.shape[0]
    lt = logits.T
    gl = lt[N_EXPERTS:N_EXPERTS + N_GROUPS]
    gmax = jnp.max(gl, axis=0, keepdims=True)
    giota = lax.broadcasted_iota(jnp.int32, gl.shape, 0)
    g_idx = jnp.min(jnp.where(gl == gmax, giota, N_GROUPS), axis=0, keepdims=True)
    p_g = 1.0 / jnp.sum(jnp.exp(gl - gmax), axis=0, keepdims=True)
    e_sel = jnp.zeros((EXPERTS_PER_GROUP, tm), F32)
    for g in range(N_GROUPS):
        e_sel = jnp.where(g_idx == g, lt[g * EXPERTS_PER_GROUP:(g + 1) * EXPERTS_PER_GROUP], e_sel)
    eiota = lax.broadcasted_iota(jnp.int32, e_sel.shape, 0)
    m1 = jnp.max(e_sel, axis=0, keepdims=True)
    i1 = jnp.min(jnp.where(e_sel == m1, eiota, EXPERTS_PER_GROUP), axis=0, keepdims=True)
    rest = jnp.where(eiota == i1, -jnp.inf, e_sel)
    m2 = jnp.max(rest, axis=0, keepdims=True)
    i2 = jnp.min(jnp.where(rest == m2, eiota, EXPERTS_PER_GROUP), axis=0, keepdims=True)
    t = jnp.exp(m2 - m1)
    ids = jnp.concatenate([g_idx * EXPERTS_PER_GROUP + i1, g_idx * EXPERTS_PER_GROUP + i2], axis=1)
    return ids, p_g / (1.0 + t), p_g * t / (1.0 + t)


def _local_sort(ids, upper_ref, lower_ref):
    eiota32 = lax.broadcasted_iota(jnp.int32, (N_EXPERTS, ids.shape[1]), 0)
    onehot = jnp.where(eiota32 == ids, 1.0, 0.0)
    count = jnp.sum(onehot, axis=1, keepdims=True)
    before = jnp.dot(onehot.astype(BF16), upper_ref[...], preferred_element_type=F32)
    nblk = jnp.floor((count + (LOCAL_BLOCK - 1)) * (1.0 / LOCAL_BLOCK))
    boff = jnp.dot(lower_ref[...], jnp.broadcast_to(nblk, (N_EXPERTS, ROUTER_LANES)).astype(BF16),
                   preferred_element_type=F32)[:, 0:1]
    lrow = jnp.sum(onehot * (before + LOCAL_BLOCK * boff), axis=0, keepdims=True)
    biota = lax.broadcasted_iota(jnp.int32, (N_EXPERTS, ROUTER_LANES), 1).astype(F32)
    owns = jnp.where(jnp.logical_and(biota >= boff, biota < boff + nblk), 1.0, 0.0)
    eiota_f = lax.broadcasted_iota(jnp.int32, (N_EXPERTS, ROUTER_LANES), 0).astype(F32)
    eb = jnp.sum(owns * eiota_f, axis=0, keepdims=True) + N_EXPERTS * (1.0 - jnp.sum(owns, axis=0, keepdims=True))
    return lrow, eb.astype(jnp.int32)


def _place(lrow, hn_hi):
    tm = hn_hi.shape[0]
    jiota = lax.broadcasted_iota(jnp.int32, (LOCAL_ROWS, tm), 0).astype(F32)
    place = jnp.where(jiota == lrow[:, :tm], 1.0, jnp.where(jiota == lrow[:, tm:], 1.0, 0.0)).astype(BF16)
    return jnp.dot(place, hn_hi, preferred_element_type=F32)


def _merge_and_route(oa, ob, proj, x2, wa, wb, wo, g2, wr_cat, br):
    t = x2.shape[0]
    n_assign = TOP_K * MERGE_TM
    upper = (jnp.arange(n_assign)[:, None] < jnp.arange(n_assign)[None, :]).astype(BF16)
    lower = (jnp.arange(N_EXPERTS)[:, None] > jnp.arange(N_EXPERTS)[None, :]).astype(BF16)
    const = lambda shape: pl.BlockSpec(shape, lambda s: (0, 0), pipeline_mode=pl.Buffered(1))
    n_tiles = t // MERGE_TM
    cur = lambda s: jnp.minimum(s, n_tiles - 1)
    prev = lambda s: jnp.maximum(s - 1, 0)
    return pl.pallas_call(
        _merge_kernel,
        grid=(n_tiles + 1,),
        in_specs=[
            pl.BlockSpec((MERGE_TM, A_WIDTH), lambda s: (cur(s), 0)),
            pl.BlockSpec((MERGE_TM, B_WIDTH), lambda s: (cur(s), 0)),
            pl.BlockSpec((MERGE_TM, D_MODEL), lambda s: (cur(s), COL_GA // D_MODEL)),
            pl.BlockSpec((MERGE_TM, D_MODEL), lambda s: (cur(s), COL_GB // D_MODEL)),
            pl.BlockSpec((MERGE_TM, D_MODEL), lambda s: (cur(s), 0)),
            const((A_WIDTH, D_MODEL)), const((B_WIDTH, D_MODEL)), const((D_MODEL, D_MODEL)),
            const((1, D_MODEL)), const((D_MODEL, ROUTER_LANES)), const((1, ROUTER_LANES)),
            const((n_assign, n_assign)), const((N_EXPERTS, N_EXPERTS)),
        ],
        out_specs=[
            pl.BlockSpec((MERGE_TM, D_MODEL), lambda s: (cur(s), 0)),
            pl.BlockSpec((LOCAL_ROWS, D_MODEL // 2), lambda s: (prev(s), 0)),
            pl.BlockSpec((1, 1, ROUTER_LANES), lambda s: (prev(s), 0, 0)),
            pl.BlockSpec((MERGE_TM, ROUTER_LANES), lambda s: (prev(s), 0)),
        ],
        scratch_shapes=[pltpu.VMEM((2, 2 * MERGE_TM, D_MODEL), BF16)],
        out_shape=[
            jax.ShapeDtypeStruct((t, D_MODEL), F32),
            jax.ShapeDtypeStruct((t // MERGE_TM * LOCAL_ROWS, D_MODEL // 2), jnp.uint32),
            jax.ShapeDtypeStruct((t // MERGE_TM, 1, ROUTER_LANES), jnp.int32),
            jax.ShapeDtypeStruct((t, ROUTER_LANES), F32),
        ],
        compiler_params=pltpu.CompilerParams(
            dimension_semantics=("arbitrary",), vmem_limit_bytes=VMEM_LIMIT),
        name="merge_and_route",
    )(oa, ob, proj, proj, x2, wa, wb, wo, g2, wr_cat, br, upper, lower)


def _expert_kernel(blk_ref, texp_ref, tcb_ref, tnb_ref, wslot_ref, wnext_ref, nused_ref,
                   xs_hbm, wg_hbm, wu_hbm, wd_hbm, y_hbm,
                   xbuf, ybuf, wgf, wuf, wdf, wgb, wub, wdb, gsem, ssem, wsem):
    n_used = nused_ref[0]

    def weight_copies(e, ws):
        return [pltpu.make_async_copy(src.at[e], dst.at[ws], wsem.at[ws])
                for src, dst in ((wg_hbm, wgf), (wu_hbm, wuf), (wd_hbm, wdf))]

    def gather_copy(blk, dst_slot, j):
        return pltpu.make_async_copy(xs_hbm.at[blk], xbuf.at[dst_slot, pl.ds(j * LOCAL_BLOCK, LOCAL_BLOCK)],
                                     gsem.at[dst_slot])

    def scatter_copy(blk, src_slot, j):
        return pltpu.make_async_copy(ybuf.at[src_slot, pl.ds(j * LOCAL_BLOCK, LOCAL_BLOCK)], y_hbm.at[blk],
                                     ssem.at[src_slot])

    def issue_gather(tile, dst_slot):
        @pl.when(texp_ref[tile] < N_EXPERTS)
        def _():
            base = tcb_ref[tile]
            for j in range(MOE_BLOCKS):
                gather_copy(blk_ref[base + j], dst_slot, j).start()

    def wait_gather(tile, s):
        @pl.when(texp_ref[tile] < N_EXPERTS)
        def _():
            for j in range(MOE_BLOCKS):
                gather_copy(0, s, j).wait()

    def for_valid_blocks(tile, fn):
        n_valid = tnb_ref[tile]

        @pl.when(n_valid == MOE_BLOCKS)
        def _():
            for j in range(MOE_BLOCKS):
                fn(j)

        @pl.when(n_valid < MOE_BLOCKS)
        def _():
            for j in range(MOE_BLOCKS):
                pl.when(j < n_valid)(functools.partial(fn, j))

    def issue_scatter(tile, src_slot):
        base = tcb_ref[tile]
        for_valid_blocks(tile, lambda j: scatter_copy(blk_ref[base + j], src_slot, j).start())

    def wait_scatter(tile, s):
        for_valid_blocks(tile, lambda j: scatter_copy(0, s, j).wait())

    @pl.when(n_used > 0)
    def _():
        issue_gather(0, 0)

        @pl.when(texp_ref[0] < N_EXPERTS)
        def _():
            for cp in weight_copies(texp_ref[0], wslot_ref[0]):
                cp.start()

    def tile_step(i, carry):
        slot = lax.rem(i, 2)
        expert = texp_ref[i]
        wait_gather(i, slot)

        @pl.when(i + 1 < n_used)
        def _():
            issue_gather(i + 1, 1 - slot)

        @pl.when(i >= 2)
        def _():
            wait_scatter(i - 2, slot)

        @pl.when(jnp.logical_and(expert < N_EXPERTS, jnp.logical_or(i == 0, expert != texp_ref[jnp.maximum(i - 1, 0)])))
        def _():
            ws = wslot_ref[i]
            for cp in weight_copies(expert, ws):
                cp.wait()
            for static_ws in (0, 1):
                @pl.when(ws == static_ws)
                def _():
                    wgb[...] = wgf[static_ws].astype(BF16)
                    wub[...] = wuf[static_ws].astype(BF16)
                    wdb[...] = wdf[static_ws].astype(BF16)
            nxt = wnext_ref[i]

            @pl.when(nxt < N_EXPERTS)
            def _():
                for cp in weight_copies(nxt, 1 - ws):
                    cp.start()

        @pl.when(expert < N_EXPERTS)
        def _():
            x_lo, x_hi = _unpack_bf16_pairs(xbuf[slot])
            half = D_MODEL // 2
            g = (jnp.dot(x_lo, wgb[:half, :], preferred_element_type=F32)
                 + jnp.dot(x_hi, wgb[half:, :], preferred_element_type=F32))
            u = (jnp.dot(x_lo, wub[:half, :], preferred_element_type=F32)
                 + jnp.dot(x_hi, wub[half:, :], preferred_element_type=F32))
            act = (g * jax.nn.sigmoid(g) * u).astype(BF16)
            y = jnp.dot(act, wdb[...], preferred_element_type=F32)
            ybuf[slot] = _pack_bf16_pairs(y.astype(BF16).astype(F32))

        @pl.when(expert >= N_EXPERTS)
        def _():
            ybuf[slot] = jnp.zeros((MOE_TM, D_MODEL // 2), jnp.uint32)

        issue_scatter(i, slot)
        return carry

    lax.fori_loop(0, n_used, tile_step, 0)

    for back in (1, 2):
        @pl.when(n_used >= back)
        def _():
            last = jnp.maximum(n_used - back, 0)
            wait_scatter(last, lax.rem(last, 2))


def _expert_ffn(plan, xs_local, wg, wu, wd):
    n_blocks = xs_local.shape[0] // LOCAL_BLOCK
    half = D_MODEL // 2
    any_spec = pl.BlockSpec(memory_space=pl.ANY)
    grid_spec = pltpu.PrefetchScalarGridSpec(
        num_scalar_prefetch=len(plan),
        grid=(1,),
        in_specs=[any_spec, any_spec, any_spec, any_spec],
        out_specs=any_spec,
        scratch_shapes=[
            pltpu.VMEM((2, MOE_TM, half), jnp.uint32), pltpu.VMEM((2, MOE_TM, half), jnp.uint32),
            pltpu.VMEM((2, D_MODEL, D_EXPERT), F32), pltpu.VMEM((2, D_MODEL, D_EXPERT), F32),
            pltpu.VMEM((2, D_EXPERT, D_MODEL), F32),
            pltpu.VMEM((D_MODEL, D_EXPERT), BF16), pltpu.VMEM((D_MODEL, D_EXPERT), BF16),
            pltpu.VMEM((D_EXPERT, D_MODEL), BF16),
            pltpu.SemaphoreType.DMA((2,)), pltpu.SemaphoreType.DMA((2,)), pltpu.SemaphoreType.DMA((2,)),
        ],
    )
    return pl.pallas_call(
        _expert_kernel,
        grid_spec=grid_spec,
        out_shape=jax.ShapeDtypeStruct((n_blocks, LOCAL_BLOCK, half), jnp.uint32),
        compiler_params=pltpu.CompilerParams(
            dimension_semantics=("arbitrary",), vmem_limit_bytes=VMEM_LIMIT),
        name="expert_ffn",
    )(*plan, xs_local.reshape(n_blocks, LOCAL_BLOCK, half), wg, wu, wd)


def _combine_kernel(h_ref, wcol_ref, y_ref, o_ref):
    y_lo, y_hi = _unpack_bf16_pairs(y_ref[...])
    w = wcol_ref[...]
    jiota = lax.broadcasted_iota(jnp.int32, (MERGE_TM, LOCAL_ROWS), 1).astype(F32)
    pick = (jnp.where(jiota == w[:, 2:3], w[:, 0:1], 0.0) + jnp.where(jiota == w[:, 3:4], w[:, 1:2], 0.0)).astype(BF16)
    half = D_MODEL // 2
    for cols, y in ((slice(0, half), y_lo), (slice(half, D_MODEL), y_hi)):
        o_ref[:, cols] = h_ref[:, cols] + jnp.dot(pick, y, preferred_element_type=F32)


def _combine(h, wcol, y_local):
    t = h.shape[0]
    return pl.pallas_call(
        _combine_kernel,
        grid=(t // MERGE_TM,),
        in_specs=[
            pl.BlockSpec((MERGE_TM, D_MODEL), lambda i: (i, 0)),
            pl.BlockSpec((MERGE_TM, ROUTER_LANES), lambda i: (i, 0)),
            pl.BlockSpec((LOCAL_ROWS, D_MODEL // 2), lambda i: (i, 0)),
        ],
        out_specs=pl.BlockSpec((MERGE_TM, D_MODEL), lambda i: (i, 0)),
        out_shape=jax.ShapeDtypeStruct((t, D_MODEL), F32),
        compiler_params=pltpu.CompilerParams(
            dimension_semantics=("parallel",), vmem_limit_bytes=VMEM_LIMIT),
        name="combine",
    )(h, wcol, y_local)


def _dispatch_plan(block_expert):
    n_blocks = block_expert.size
    n_keys = N_EXPERTS + 1
    n_tiles = n_blocks // MOE_BLOCKS + n_keys
    keys = block_expert.reshape(n_blocks)
    order = jnp.argsort(keys, stable=True).astype(jnp.int32)
    counts = jnp.sum(keys[None, :] == jnp.arange(n_keys, dtype=jnp.int32)[:, None], axis=1).astype(jnp.int32)
    tiles_per = (counts + MOE_BLOCKS - 1) // MOE_BLOCKS
    tile_end = jnp.cumsum(tiles_per)
    tile_first = tile_end - tiles_per
    start = jnp.cumsum(counts) - counts
    tile = jnp.arange(n_tiles, dtype=jnp.int32)
    tile_expert = jnp.minimum(jnp.sum(tile_end[None, :] <= tile[:, None], axis=1), n_keys - 1).astype(jnp.int32)
    sel = (tile_expert[:, None] == jnp.arange(n_keys, dtype=jnp.int32)[None, :]).astype(jnp.int32)
    pick = lambda v: jnp.sum(sel * v[None, :], axis=1)
    within = (tile - pick(tile_first)) * MOE_BLOCKS
    tile_base = jnp.clip(pick(start) + within, 0, n_blocks).astype(jnp.int32)
    tile_blocks = jnp.clip(pick(counts) - within, 0, MOE_BLOCKS).astype(jnp.int32)
    n_used = tile_end[-1].astype(jnp.int32).reshape(1)
    order = jnp.concatenate([order, jnp.zeros((MOE_BLOCKS,), jnp.int32)])
    key = jnp.arange(n_keys, dtype=jnp.int32)
    present = jnp.logical_and(counts > 0, key < N_EXPERTS)
    slot_of = (jnp.cumsum(present) - present) % 2
    later = jnp.logical_and(key[None, :] > key[:, None], present[None, :])
    next_of = jnp.min(jnp.where(later, key[None, :], N_EXPERTS), axis=1)
    return (order, tile_expert, tile_base, tile_blocks, pick(slot_of).astype(jnp.int32),
            pick(next_of).astype(jnp.int32), n_used)


def _layer(x, norm1_g, w_in, q_norm_a, k_norm_a, sink_a, q_norm_b, k_norm_b, rpb_b, w_up_a, w_up_b, w_out,
           rel_table, norm2_g, w_rg, b_rg, w_re, b_re, w_gate, w_up, w_down):
    batch, seq, d = x.shape
    t = batch * seq
    x2 = x.reshape(t, d)

    cuts = [0, A_WIDTH, A_WIDTH + A_KV_WIDTH, A_WIDTH + 2 * A_KV_WIDTH]
    cuts += [cuts[-1] + B_WIDTH, cuts[-1] + 2 * B_WIDTH, cuts[-1] + 3 * B_WIDTH]
    cuts += [cuts[-1] + D_MODEL, cuts[-1] + 2 * D_MODEL]
    qa, ka, va, qb, kb, vb, ga, gb = [w_in[:, cuts[n]:cuts[n + 1]] for n in range(8)]
    w_cat = jnp.concatenate([qa, qb, kb, vb, ga, gb, ka, va], axis=1).astype(BF16)
    scale = HEAD_DIM ** -0.5
    ones = lambda n: jnp.ones((n,), F32)
    gain_cat = jnp.concatenate([
        jnp.tile(q_norm_a.astype(F32) * scale, A_HEADS), jnp.tile(q_norm_b.astype(F32) * scale, B_HEADS),
        jnp.tile(k_norm_b.astype(F32), B_HEADS), ones(B_WIDTH + 2 * D_MODEL),
        jnp.tile(k_norm_a.astype(F32), A_KV_HEADS), ones(A_KV_WIDTH)]).reshape(1, IN_COLS)

    kind = lambda v, n: jnp.full((n,), v, F32)
    kind_cat = jnp.concatenate([kind(1.0, COL_VB), kind(0.0, B_WIDTH), kind(2.0, 2 * D_MODEL),
                                kind(1.0, A_KV_WIDTH), kind(0.0, A_KV_WIDTH)]).reshape(1, IN_COLS)

    proj = _in_projection(x2, norm1_g.astype(F32).reshape(1, d), w_cat, gain_cat, kind_cat)
    oa = _window_attention(proj, _window_bias(rel_table), sink_a.astype(F32), batch, seq)
    ob = _neighbourhood_attention(proj, _neighbourhood_col_bias(rpb_b), batch, seq)

    wr = jnp.concatenate([w_re.astype(F32), w_rg.astype(F32)], axis=1)
    wr_hi = wr.astype(BF16)
    wr_lo = (wr - wr_hi.astype(F32)).astype(BF16)
    half = ROUTER_LANES // 2
    padc = lambda a: jnp.pad(a, ((0, 0), (0, half - a.shape[1])))
    wr_cat = jnp.concatenate([padc(wr_hi), padc(wr_lo)], axis=1)
    br = jnp.pad(jnp.concatenate([b_re.astype(F32), b_rg.astype(F32)]), (0, ROUTER_LANES - N_EXPERTS - N_GROUPS))

    h, xs_local, block_expert, wcol = _merge_and_route(
        oa, ob, proj, x2, w_up_a.astype(BF16), w_up_b.astype(BF16), w_out.astype(BF16),
        norm2_g.astype(F32).reshape(1, d), wr_cat, br.reshape(1, ROUTER_LANES))

    plan = _dispatch_plan(block_expert[:, 0, :LOCAL_BLOCKS])
    y_local = _expert_ffn(plan, xs_local, w_gate.astype(F32), w_up.astype(F32), w_down.astype(F32))
    out = _combine(h, wcol, y_local.reshape(-1, d // 2))
    return out.reshape(batch, seq, d)


def kernel(x, norm1_g, w_in, q_norm_a, k_norm_a, sink_a, q_norm_b, k_norm_b, rpb_b, w_up_a, w_up_b, w_out,
           rel_bias_table, norm2_g, w_router_group, b_router_group, w_router_expert, b_router_expert,
           w_gate, w_up, w_down):
    return _layer(x, norm1_g[0], w_in[0], q_norm_a[0], k_norm_a[0], sink_a[0], q_norm_b[0], k_norm_b[0],
                  rpb_b[0], w_up_a[0], w_up_b[0], w_out[0], rel_bias_table, norm2_g[0],
                  w_router_group[0], b_router_group[0], w_router_expert[0], b_router_expert[0],
                  w_gate[0], w_up[0], w_down[0])
```

```python
import functools
import math

import jax
import jax.numpy as jnp
from jax import lax
from jax.experimental import pallas as pl
from jax.experimental.pallas import tpu as pltpu

F32 = jnp.float32
BF16 = jnp.bfloat16

D_MODEL = 2048
HEAD_DIM = 128
A_HEADS = 8
A_KV_HEADS = 2
A_GROUP = A_HEADS // A_KV_HEADS
A_WIDTH = A_HEADS * HEAD_DIM
A_KV_WIDTH = A_KV_HEADS * HEAD_DIM
WINDOW = 128
A_BLOCK = 128
NUM_BUCKETS = 32
MAX_DISTANCE = 128
B_HEADS = 8
B_WIDTH = B_HEADS * HEAD_DIM
GRID_W = 64
NB_ROWS = 8
NB_COLS = 16
IN_COLS = A_WIDTH + 2 * A_KV_WIDTH + 3 * B_WIDTH + 2 * D_MODEL
N_GROUPS = 4
EXPERTS_PER_GROUP = 8
N_EXPERTS = N_GROUPS * EXPERTS_PER_GROUP
TOP_K = 2
D_EXPERT = 512
EPS = 1e-6
NEG_INF = -1e30

COL_QA = 0
COL_QB = COL_QA + A_WIDTH
COL_KB = COL_QB + B_WIDTH
COL_VB = COL_KB + B_WIDTH
COL_GA = COL_VB + B_WIDTH
COL_GB = COL_GA + D_MODEL
COL_KA = COL_GB + D_MODEL
COL_VA = COL_KA + A_KV_WIDTH

PROJ_TM = 1024
PROJ_TN = 512
PROJ_SUB = 128
WIN_STEP = 4
NA_ROWS = 4
NA_BLOCK = NA_ROWS * GRID_W
MERGE_TM = 256
ROUTER_LANES = 128
LOCAL_BLOCK = 8
LOCAL_ROWS = 768
LOCAL_BLOCKS = LOCAL_ROWS // LOCAL_BLOCK
assert LOCAL_ROWS >= TOP_K * MERGE_TM + N_EXPERTS * (LOCAL_BLOCK - 1) and LOCAL_BLOCKS <= ROUTER_LANES
MOE_TM = 256
MOE_BLOCKS = MOE_TM // LOCAL_BLOCK
VMEM_LIMIT = 56 * 1024 * 1024


def _head_norm(a, gain):
    outs = []
    for hd in range(a.shape[1] // HEAD_DIM):
        sl = slice(hd * HEAD_DIM, (hd + 1) * HEAD_DIM)
        ah = a[:, sl]
        ss = jnp.sum(ah * ah, axis=-1, keepdims=True)
        outs.append(ah * lax.rsqrt(ss * (1.0 / HEAD_DIM) + EPS) * gain[:, sl])
    return outs[0] if len(outs) == 1 else jnp.concatenate(outs, axis=1)


def _inproj_kernel(x_ref, g1_ref, w_ref, gain_ref, kind_ref, o_ref, xn_ref):
    j = pl.program_id(1)

    def project(normalise_first):
        kind = kind_ref[...]
        for r in range(PROJ_TM // PROJ_SUB):
            rows = slice(r * PROJ_SUB, (r + 1) * PROJ_SUB)
            if normalise_first:
                x = x_ref[rows, :]
                ms = jnp.mean(x * x, axis=-1, keepdims=True)
                xn_ref[rows, :] = (x * lax.rsqrt(ms + EPS) * g1_ref[...]).astype(BF16)
            acc = jnp.dot(xn_ref[rows, :], w_ref[...], preferred_element_type=F32)
            out = jnp.where(kind == 1.0, _head_norm(acc, gain_ref[...]),
                            jnp.where(kind == 2.0, jax.nn.sigmoid(acc), acc))
            o_ref[rows, :] = out.astype(o_ref.dtype)

    pl.when(j == 0)(functools.partial(project, True))
    pl.when(j != 0)(functools.partial(project, False))


def _in_projection(x2, g1, w_cat, gain_cat, kind_cat):
    t = x2.shape[0]
    return pl.pallas_call(
        _inproj_kernel,
        grid=(t // PROJ_TM, IN_COLS // PROJ_TN),
        in_specs=[
            pl.BlockSpec((PROJ_TM, D_MODEL), lambda i, j: (i, 0)),
            pl.BlockSpec((1, D_MODEL), lambda i, j: (0, 0)),
            pl.BlockSpec((D_MODEL, PROJ_TN), lambda i, j: (0, j)),
            pl.BlockSpec((1, PROJ_TN), lambda i, j: (0, j)),
            pl.BlockSpec((1, PROJ_TN), lambda i, j: (0, j)),
        ],
        out_specs=pl.BlockSpec((PROJ_TM, PROJ_TN), lambda i, j: (i, j)),
        out_shape=jax.ShapeDtypeStruct((t, IN_COLS), BF16),
        scratch_shapes=[pltpu.VMEM((PROJ_TM, D_MODEL), BF16)],
        compiler_params=pltpu.CompilerParams(
            dimension_semantics=("parallel", "arbitrary"), vmem_limit_bytes=VMEM_LIMIT),
        name="in_projection",
    )(x2, g1, w_cat, gain_cat, kind_cat)


def _t5_bucket(rel):
    nb = NUM_BUCKETS // 2
    ret = jnp.where(rel > 0, nb, 0)
    n = jnp.abs(rel)
    max_exact = nb // 2
    nf = jnp.maximum(n, 1).astype(jnp.float32)
    large = max_exact + (jnp.log(nf / max_exact) / math.log(MAX_DISTANCE / max_exact)
                         * (nb - max_exact)).astype(jnp.int32)
    large = jnp.minimum(large, nb - 1)
    return ret + jnp.where(n < max_exact, n, large)


def _window_bias(rel_table):
    q_loc = jnp.arange(A_BLOCK)
    k_loc = jnp.arange(3 * A_BLOCK) - WINDOW
    rel = k_loc[None, :] - q_loc[:, None]
    onehot = (_t5_bucket(rel)[..., None] == jnp.arange(NUM_BUCKETS)).astype(F32)
    bias = jnp.einsum('qkn,nh->hqk', onehot, rel_table.astype(F32), precision=lax.Precision.HIGHEST)
    in_window = jnp.abs(rel) <= WINDOW
    has_prev = jnp.array([False, True, True])[:, None, None]
    has_next = jnp.array([True, True, False])[:, None, None]
    k_blk = (jnp.arange(3 * A_BLOCK) // A_BLOCK)[None, None, :]
    valid = in_window[None] & ((k_blk != 0) | has_prev) & ((k_blk != 2) | has_next)
    full = jnp.where(valid[:, None], bias[None], NEG_INF)
    return full.reshape(3, A_KV_HEADS, A_GROUP * A_BLOCK, 3 * A_BLOCK)


def _window_kernel(sink_ref, q_ref, kp_ref, ko_ref, kn_ref, vp_ref, vo_ref, vn_ref,
                   bias_lo_ref, bias_mid_ref, bias_hi_ref, o_ref):
    for kvh in range(A_KV_HEADS):
        hs = slice(kvh * HEAD_DIM, (kvh + 1) * HEAD_DIM)
        heads = [kvh * A_GROUP + g for g in range(A_GROUP)]
        kc = jnp.concatenate([kp_ref[:, hs], ko_ref[:, hs], kn_ref[:, hs]], axis=0)
        vc = jnp.concatenate([vp_ref[:, hs], vo_ref[:, hs], vn_ref[:, hs]], axis=0)
        vc1 = jnp.concatenate([vc, jnp.ones_like(vc)], axis=1)
        sink = jnp.concatenate([jnp.full((A_BLOCK, 1), sink_ref[h], F32) for h in heads], axis=0)
        for jq in range(WIN_STEP):
            rows = slice(jq * A_BLOCK, (jq + 1) * A_BLOCK)
            band = slice(jq * A_BLOCK, (jq + 3) * A_BLOCK)
            bias_ref = bias_lo_ref if jq == 0 else (bias_hi_ref if jq == WIN_STEP - 1 else bias_mid_ref)
            q4 = jnp.concatenate([q_ref[rows, h * HEAD_DIM:(h + 1) * HEAD_DIM] for h in heads], axis=0)
            s = lax.dot_general(q4, kc[band], (((1,), (1,)), ((), ())), preferred_element_type=F32)
            s = s + bias_ref[0, kvh]
            m = jnp.max(s, axis=-1, keepdims=True)
            p = jnp.exp((s - m).astype(BF16))
            ov = jnp.dot(p, vc1[band], preferred_element_type=F32)
            o = ov[:, :HEAD_DIM] / (ov[:, HEAD_DIM:] + jnp.exp(sink - m))
            for g, h in enumerate(heads):
                o_ref[rows, h * HEAD_DIM:(h + 1) * HEAD_DIM] = o[g * A_BLOCK:(g + 1) * A_BLOCK].astype(o_ref.dtype)


def _window_attention(proj, bias_a, sink, batch, seq):
    nb = seq // A_BLOCK
    ns = nb // WIN_STEP
    kcol, vcol = COL_KA // A_KV_WIDTH, COL_VA // A_KV_WIDTH
    prev = lambda b, n: b * nb + jnp.maximum(n * WIN_STEP - 1, 0)
    nxt = lambda b, n: b * nb + jnp.minimum((n + 1) * WIN_STEP, nb - 1)
    own = lambda b, n: b * ns + n
    edge = lambda rfn, col: pl.BlockSpec((A_BLOCK, A_KV_WIDTH), lambda b, n: (rfn(b, n), col))
    main = lambda col: pl.BlockSpec((WIN_STEP * A_BLOCK, A_KV_WIDTH), lambda b, n: (own(b, n), col))
    bias_block = (1, A_KV_HEADS, A_GROUP * A_BLOCK, 3 * A_BLOCK)
    return pl.pallas_call(
        _window_kernel,
        grid=(batch, ns),
        in_specs=[
            pl.BlockSpec(memory_space=pltpu.SMEM),
            pl.BlockSpec((WIN_STEP * A_BLOCK, A_WIDTH), lambda b, n: (own(b, n), COL_QA // A_WIDTH)),
            edge(prev, kcol), main(kcol), edge(nxt, kcol),
            edge(prev, vcol), main(vcol), edge(nxt, vcol),
            pl.BlockSpec(bias_block, lambda b, n: (jnp.where(n == 0, 0, 1), 0, 0, 0)),
            pl.BlockSpec(bias_block, lambda b, n: (1, 0, 0, 0)),
            pl.BlockSpec(bias_block, lambda b, n: (jnp.where(n == ns - 1, 2, 1), 0, 0, 0)),
        ],
        out_specs=pl.BlockSpec((WIN_STEP * A_BLOCK, A_WIDTH), lambda b, n: (own(b, n), 0)),
        out_shape=jax.ShapeDtypeStruct((batch * seq, A_WIDTH), BF16),
        compiler_params=pltpu.CompilerParams(
            dimension_semantics=("parallel", "arbitrary"), vmem_limit_bytes=VMEM_LIMIT),
        name="window_attention",
    )(sink, proj, proj, proj, proj, proj, proj, proj, bias_a, bias_a, bias_a)


def _neighbourhood_col_bias(rpb):
    cols = jnp.arange(GRID_W)
    cs = jnp.clip(cols - NB_COLS // 2, 0, GRID_W - NB_COLS)
    col_ok = (cols[None, :] >= cs[:, None]) & (cols[None, :] < cs[:, None] + NB_COLS)
    dc = cols[None, :] - cols[:, None] + (NB_COLS - 1)
    col_sel = ((dc[..., None] == jnp.arange(2 * NB_COLS - 1)) & col_ok[..., None]).astype(F32)
    by_col = jnp.einsum('hde,cje->hdcj', rpb.astype(F32), col_sel, precision=lax.Precision.HIGHEST)
    return jnp.where(col_ok[None, None], by_col, NEG_INF)


def _neighbourhood_row_plan(cls, rows):
    kh = min(NB_ROWS, rows)
    r0 = (0, NA_ROWS, rows - NA_ROWS)[cls]
    plan = []
    for i in range(NA_ROWS):
        r = r0 + i
        rs = min(max(r - kh // 2, 0), rows - kh)
        plan.append([key_r - r + NB_ROWS - 1 if rs <= key_r < rs + kh else None
                     for key_r in range(r0 - NA_ROWS, r0 + 2 * NA_ROWS)])
    return plan


def _neighbourhood_kernel(q_ref, kp_ref, ko_ref, kn_ref, vp_ref, vo_ref, vn_ref, colbias_ref, o_ref, bias_ref,
                          *, rows):
    n = pl.program_id(1)
    n_blocks = pl.num_programs(1)

    def build_bias(cls):
        plan = _neighbourhood_row_plan(cls, rows)
        masked = jnp.full((GRID_W, GRID_W), NEG_INF, F32)
        for h in range(B_HEADS):
            for i in range(NA_ROWS):
                for pair in range(3 * NA_ROWS // 2):
                    tiles = [masked if plan[i][kr] is None else colbias_ref[h, plan[i][kr]]
                             for kr in (2 * pair, 2 * pair + 1)]
                    bias_ref[h, i * GRID_W:(i + 1) * GRID_W, pair * 2 * GRID_W:(pair + 1) * 2 * GRID_W] = (
                        jnp.concatenate(tiles, axis=1))

    pl.when(n == 0)(functools.partial(build_bias, 0))
    pl.when(n == 1)(functools.partial(build_bias, 1))
    pl.when(n == n_blocks - 1)(functools.partial(build_bias, 2))

    for h in range(B_HEADS):
        hs = slice(h * HEAD_DIM, (h + 1) * HEAD_DIM)
        kc = jnp.concatenate([kp_ref[:, hs], ko_ref[:, hs], kn_ref[:, hs]], axis=0)
        vc = jnp.concatenate([vp_ref[:, hs], vo_ref[:, hs], vn_ref[:, hs]], axis=0)
        s = lax.dot_general(q_ref[:, hs], kc, (((1,), (1,)), ((), ())), preferred_element_type=F32)
        s = s + bias_ref[h]
        m = jnp.max(s, axis=-1, keepdims=True)
        p = jnp.exp((s - m).astype(BF16))
        ov = jnp.dot(p, jnp.concatenate([vc, jnp.ones_like(vc)], axis=1), preferred_element_type=F32)
        o_ref[:, hs] = (ov[:, :HEAD_DIM] / ov[:, HEAD_DIM:]).astype(o_ref.dtype)


def _neighbourhood_attention(proj, col_bias, batch, seq):
    nb = seq // NA_BLOCK
    assert nb >= 2

    def row(b, n):
        return b * nb + n

    prev = lambda b, n: row(b, jnp.maximum(n - 1, 0))
    nxt = lambda b, n: row(b, jnp.minimum(n + 1, nb - 1))
    spec = lambda rfn, col: pl.BlockSpec((NA_BLOCK, B_WIDTH), lambda b, n: (rfn(b, n), col))
    qcol, kcol, vcol = COL_QB // B_WIDTH, COL_KB // B_WIDTH, COL_VB // B_WIDTH
    return pl.pallas_call(
        functools.partial(_neighbourhood_kernel, rows=seq // GRID_W),
        grid=(batch, nb),
        in_specs=[
            spec(row, qcol),
            spec(prev, kcol), spec(row, kcol), spec(nxt, kcol),
            spec(prev, vcol), spec(row, vcol), spec(nxt, vcol),
            pl.BlockSpec(col_bias.shape, lambda b, n: (0, 0, 0, 0), pipeline_mode=pl.Buffered(1)),
        ],
        out_specs=pl.BlockSpec((NA_BLOCK, B_WIDTH), lambda b, n: (row(b, n), 0)),
        out_shape=jax.ShapeDtypeStruct((batch * seq, B_WIDTH), BF16),
        scratch_shapes=[pltpu.VMEM((B_HEADS, NA_BLOCK, 3 * NA_BLOCK), F32)],
        compiler_params=pltpu.CompilerParams(
            dimension_semantics=("arbitrary", "arbitrary"), vmem_limit_bytes=VMEM_LIMIT),
        name="neighbourhood_attention",
    )(proj, proj, proj, proj, proj, proj, proj, col_bias)


def _pack_bf16_pairs(a):
    n = a.shape[1] // 2
    bits = lax.bitcast_convert_type(a, jnp.uint32)
    return bits[:, n:] | (bits[:, :n] >> 16)


def _unpack_bf16_pairs(p):
    lo = lax.bitcast_convert_type(p << 16, F32).astype(BF16)
    hi = lax.bitcast_convert_type(p & jnp.uint32(0xFFFF0000), F32).astype(BF16)
    return lo, hi


def _merge_kernel(oa_ref, ob_ref, ga_ref, gb_ref, x_ref, wa_ref, wb_ref, wo_ref, g2_ref, wr_ref, br_ref,
                  upper_ref, lower_ref, h_ref, xs_ref, eb_ref, wcol_ref, hn_scr):
    s = pl.program_id(0)
    cur = lax.rem(s, 2)
    prev = 1 - cur
    tm = MERGE_TM

    @pl.when(s == 0)
    def _():
        hn_scr[...] = jnp.zeros_like(hn_scr)

    hn_prev = hn_scr[prev]
    logits = _router_logits(hn_prev, wr_ref, br_ref)
    ua = jnp.dot(oa_ref[...], wa_ref[...], preferred_element_type=F32)
    ub = jnp.dot(ob_ref[...], wb_ref[...], preferred_element_type=F32)
    ids, w1, w2 = _route(logits)
    lrow, eb = _local_sort(ids, upper_ref, lower_ref)
    merged = ga_ref[...].astype(F32) * ua + gb_ref[...].astype(F32) * ub
    acc = jnp.dot(merged.astype(BF16), wo_ref[...], preferred_element_type=F32)
    xs = _place(lrow, hn_prev)
    h = x_ref[...] + acc
    h_ref[...] = h
    ms = jnp.mean(h * h, axis=-1, keepdims=True)
    hn_scr[cur] = (h * lax.rsqrt(ms + EPS) * g2_ref[...]).astype(BF16)

    xs_ref[...] = _pack_bf16_pairs(xs)
    eb_ref[0] = eb
    riota = lax.broadcasted_iota(jnp.int32, (ROUTER_LANES, tm), 0)
    wrows = jnp.where(riota == 0, w1, jnp.where(riota == 1, w2,
                      jnp.where(riota == 2, lrow[:, :tm], jnp.where(riota == 3, lrow[:, tm:], 0.0))))
    wcol_ref[...] = wrows.T


def _router_logits(hn, wr_ref, br_ref):
    lg = jnp.dot(hn, wr_ref[...], preferred_element_type=F32)
    return lg + pltpu.roll(lg, ROUTER_LANES // 2, axis=1) + br_ref[...]


def _route(logits):
    tm = logits.shape[0]
    lt = logits.T
    gl = lt[N_EXPERTS:N_EXPERTS + N_GROUPS]
    gmax = jnp.max(gl, axis=0, keepdims=True)
    giota = lax.broadcasted_iota(jnp.int32, gl.shape, 0)
    g_idx = jnp.min(jnp.where(gl == gmax, giota, N_GROUPS), axis=0, keepdims=True)
    p_g = 1.0 / jnp.sum(jnp.exp(gl - gmax), axis=0, keepdims=True)
    e_sel = jnp.zeros((EXPERTS_PER_GROUP, tm), F32)
    for g in range(N_GROUPS):
        e_sel = jnp.where(g_idx == g, lt[g * EXPERTS_PER_GROUP:(g + 1) * EXPERTS_PER_GROUP], e_sel)
    eiota = lax.broadcasted_iota(jnp.int32, e_sel.shape, 0)
    m1 = jnp.max(e_sel, axis=0, keepdims=True)
    i1 = jnp.min(jnp.where(e_sel == m1, eiota, EXPERTS_PER_GROUP), axis=0, keepdims=True)
    rest = jnp.where(eiota == i1, -jnp.inf, e_sel)
    m2 = jnp.max(rest, axis=0, keepdims=True)
    i2 = jnp.min(jnp.where(rest == m2, eiota, EXPERTS_PER_GROUP), axis=0, keepdims=True)
    t = jnp.exp(m2 - m1)
    ids = jnp.concatenate([g_idx * EXPERTS_PER_GROUP + i1, g_idx * EXPERTS_PER_GROUP + i2], axis=1)
    return ids, p_g / (1.0 + t), p_g * t / (1.0 + t)


def _local_sort(ids, upper_ref, lower_ref):
    eiota32 = lax.broadcasted_iota(jnp.int32, (N_EXPERTS, ids.shape[1]), 0)
    onehot = jnp.where(eiota32 == ids, 1.0, 0.0)
    count = jnp.sum(onehot, axis=1, keepdims=True)
    before = jnp.dot(onehot.astype(BF16), upper_ref[...], preferred_element_type=F32)
    nblk = jnp.floor((count + (LOCAL_BLOCK - 1)) * (1.0 / LOCAL_BLOCK))
    boff = jnp.dot(lower_ref[...], jnp.broadcast_to(nblk, (N_EXPERTS, ROUTER_LANES)).astype(BF16),
                   preferred_element_type=F32)[:, 0:1]
    lrow = jnp.sum(onehot * (before + LOCAL_BLOCK * boff), axis=0, keepdims=True)
    biota = lax.broadcasted_iota(jnp.int32, (N_EXPERTS, ROUTER_LANES), 1).astype(F32)
    owns = jnp.where(jnp.logical_and(biota >= boff, biota < boff + nblk), 1.0, 0.0)
    eiota_f = lax.broadcasted_iota(jnp.int32, (N_EXPERTS, ROUTER_LANES), 0).astype(F32)
    eb = jnp.sum(owns * eiota_f, axis=0, keepdims=True) + N_EXPERTS * (1.0 - jnp.sum(owns, axis=0, keepdims=True))
    return lrow, eb.astype(jnp.int32)


def _place(lrow, hn_hi):
    tm = hn_hi.shape[0]
    jiota = lax.broadcasted_iota(jnp.int32, (LOCAL_ROWS, tm), 0).astype(F32)
    place = jnp.where(jiota == lrow[:, :tm], 1.0, jnp.where(jiota == lrow[:, tm:], 1.0, 0.0)).astype(BF16)
    return jnp.dot(place, hn_hi, preferred_element_type=F32)


def _merge_and_route(oa, ob, proj, x2, wa, wb, wo, g2, wr_cat, br):
    t = x2.shape[0]
    n_assign = TOP_K * MERGE_TM
    upper = (jnp.arange(n_assign)[:, None] < jnp.arange(n_assign)[None, :]).astype(BF16)
    lower = (jnp.arange(N_EXPERTS)[:, None] > jnp.arange(N_EXPERTS)[None, :]).astype(BF16)
    const = lambda shape: pl.BlockSpec(shape, lambda s: (0, 0), pipeline_mode=pl.Buffered(1))
    n_tiles = t // MERGE_TM
    cur = lambda s: jnp.minimum(s, n_tiles - 1)
    prev = lambda s: jnp.maximum(s - 1, 0)
    return pl.pallas_call(
        _merge_kernel,
        grid=(n_tiles + 1,),
        in_specs=[
            pl.BlockSpec((MERGE_TM, A_WIDTH), lambda s: (cur(s), 0)),
            pl.BlockSpec((MERGE_TM, B_WIDTH), lambda s: (cur(s), 0)),
            pl.BlockSpec((MERGE_TM, D_MODEL), lambda s: (cur(s), COL_GA // D_MODEL)),
            pl.BlockSpec((MERGE_TM, D_MODEL), lambda s: (cur(s), COL_GB // D_MODEL)),
            pl.BlockSpec((MERGE_TM, D_MODEL), lambda s: (cur(s), 0)),
            const((A_WIDTH, D_MODEL)), const((B_WIDTH, D_MODEL)), const((D_MODEL, D_MODEL)),
            const((1, D_MODEL)), const((D_MODEL, ROUTER_LANES)), const((1, ROUTER_LANES)),
            const((n_assign, n_assign)), const((N_EXPERTS, N_EXPERTS)),
        ],
        out_specs=[
            pl.BlockSpec((MERGE_TM, D_MODEL), lambda s: (cur(s), 0)),
            pl.BlockSpec((LOCAL_ROWS, D_MODEL // 2), lambda s: (prev(s), 0)),
            pl.BlockSpec((1, 1, ROUTER_LANES), lambda s: (prev(s), 0, 0)),
            pl.BlockSpec((MERGE_TM, ROUTER_LANES), lambda s: (prev(s), 0)),
        ],
        scratch_shapes=[pltpu.VMEM((2, MERGE_TM, D_MODEL), BF16)],
        out_shape=[
            jax.ShapeDtypeStruct((t, D_MODEL), F32),
            jax.ShapeDtypeStruct((t // MERGE_TM * LOCAL_ROWS, D_MODEL // 2), jnp.uint32),
            jax.ShapeDtypeStruct((t // MERGE_TM, 1, ROUTER_LANES), jnp.int32),
            jax.ShapeDtypeStruct((t, ROUTER_LANES), F32),
        ],
        compiler_params=pltpu.CompilerParams(
            dimension_semantics=("arbitrary",), vmem_limit_bytes=VMEM_LIMIT),
        name="merge_and_route",
    )(oa, ob, proj, proj, x2, wa, wb, wo, g2, wr_cat, br, upper, lower)


def _expert_kernel(blk_ref, texp_ref, tcb_ref, tnb_ref, wslot_ref, wnext_ref, nused_ref,
                   xs_hbm, wg_hbm, wu_hbm, wd_hbm, y_hbm,
                   xbuf, ybuf, wgf, wuf, wdf, wgb, wub, wdb, gsem, ssem, wsem):
    n_used = nused_ref[0]

    def weight_copies(e, ws):
        return [pltpu.make_async_copy(src.at[e], dst.at[ws], wsem.at[ws])
                for src, dst in ((wg_hbm, wgf), (wu_hbm, wuf), (wd_hbm, wdf))]

    def gather_copy(blk, dst_slot, j):
        return pltpu.make_async_copy(xs_hbm.at[blk], xbuf.at[dst_slot, pl.ds(j * LOCAL_BLOCK, LOCAL_BLOCK)],
                                     gsem.at[dst_slot])

    def scatter_copy(blk, src_slot, j):
        return pltpu.make_async_copy(ybuf.at[src_slot, pl.ds(j * LOCAL_BLOCK, LOCAL_BLOCK)], y_hbm.at[blk],
                                     ssem.at[src_slot])

    def issue_gather(tile, dst_slot):
        @pl.when(texp_ref[tile] < N_EXPERTS)
        def _():
            base = tcb_ref[tile]
            for j in range(MOE_BLOCKS):
                gather_copy(blk_ref[base + j], dst_slot, j).start()

    def wait_gather(tile, s):
        @pl.when(texp_ref[tile] < N_EXPERTS)
        def _():
            for j in range(MOE_BLOCKS):
                gather_copy(0, s, j).wait()

    def for_valid_blocks(tile, fn):
        n_valid = tnb_ref[tile]

        @pl.when(n_valid == MOE_BLOCKS)
        def _():
            for j in range(MOE_BLOCKS):
                fn(j)

        @pl.when(n_valid < MOE_BLOCKS)
        def _():
            for j in range(MOE_BLOCKS):
                pl.when(j < n_valid)(functools.partial(fn, j))

    def issue_scatter(tile, src_slot):
        base = tcb_ref[tile]
        for_valid_blocks(tile, lambda j: scatter_copy(blk_ref[base + j], src_slot, j).start())

    def wait_scatter(tile, s):
        for_valid_blocks(tile, lambda j: scatter_copy(0, s, j).wait())

    @pl.when(n_used > 0)
    def _():
        issue_gather(0, 0)

        @pl.when(texp_ref[0] < N_EXPERTS)
        def _():
            for cp in weight_copies(texp_ref[0], wslot_ref[0]):
                cp.start()

    def tile_step(i, carry):
        slot = lax.rem(i, 2)
        expert = texp_ref[i]
        wait_gather(i, slot)

        @pl.when(i + 1 < n_used)
        def _():
            issue_gather(i + 1, 1 - slot)

        @pl.when(i >= 2)
        def _():
            wait_scatter(i - 2, slot)

        @pl.when(jnp.logical_and(expert < N_EXPERTS, jnp.logical_or(i == 0, expert != texp_ref[jnp.maximum(i - 1, 0)])))
        def _():
            ws = wslot_ref[i]
            for cp in weight_copies(expert, ws):
                cp.wait()
            for static_ws in (0, 1):
                @pl.when(ws == static_ws)
                def _():
                    wgb[...] = wgf[static_ws].astype(BF16)
                    wub[...] = wuf[static_ws].astype(BF16)
                    wdb[...] = wdf[static_ws].astype(BF16)
            nxt = wnext_ref[i]

            @pl.when(nxt < N_EXPERTS)
            def _():
                for cp in weight_copies(nxt, 1 - ws):
                    cp.start()

        @pl.when(expert < N_EXPERTS)
        def _():
            x_lo, x_hi = _unpack_bf16_pairs(xbuf[slot])
            half = D_MODEL // 2
            g = (jnp.dot(x_lo, wgb[:half, :], preferred_element_type=F32)
                 + jnp.dot(x_hi, wgb[half:, :], preferred_element_type=F32))
            u = (jnp.dot(x_lo, wub[:half, :], preferred_element_type=F32)
                 + jnp.dot(x_hi, wub[half:, :], preferred_element_type=F32))
            act = (g * jax.nn.sigmoid(g) * u).astype(BF16)
            y = jnp.dot(act, wdb[...], preferred_element_type=F32)
            ybuf[slot] = _pack_bf16_pairs(y.astype(BF16).astype(F32))

        @pl.when(expert >= N_EXPERTS)
        def _():
            ybuf[slot] = jnp.zeros((MOE_TM, D_MODEL // 2), jnp.uint32)

        issue_scatter(i, slot)
        return carry

    lax.fori_loop(0, n_used, tile_step, 0)

    for back in (1, 2):
        @pl.when(n_used >= back)
        def _():
            last = jnp.maximum(n_used - back, 0)
            wait_scatter(last, lax.rem(last, 2))


def _expert_ffn(plan, xs_local, wg, wu, wd):
    n_blocks = xs_local.shape[0] // LOCAL_BLOCK
    half = D_MODEL // 2
    any_spec = pl.BlockSpec(memory_space=pl.ANY)
    grid_spec = pltpu.PrefetchScalarGridSpec(
        num_scalar_prefetch=len(plan),
        grid=(1,),
        in_specs=[any_spec, any_spec, any_spec, any_spec],
        out_specs=any_spec,
        scratch_shapes=[
            pltpu.VMEM((2, MOE_TM, half), jnp.uint32), pltpu.VMEM((2, MOE_TM, half), jnp.uint32),
            pltpu.VMEM((2, D_MODEL, D_EXPERT), F32), pltpu.VMEM((2, D_MODEL, D_EXPERT), F32),
            pltpu.VMEM((2, D_EXPERT, D_MODEL), F32),
            pltpu.VMEM((D_MODEL, D_EXPERT), BF16), pltpu.VMEM((D_MODEL, D_EXPERT), BF16),
            pltpu.VMEM((D_EXPERT, D_MODEL), BF16),
            pltpu.SemaphoreType.DMA((2,)), pltpu.SemaphoreType.DMA((2,)), pltpu.SemaphoreType.DMA((2,)),
        ],
    )
    return pl.pallas_call(
        _expert_kernel,
        grid_spec=grid_spec,
        out_shape=jax.ShapeDtypeStruct((n_blocks, LOCAL_BLOCK, half), jnp.uint32),
        compiler_params=pltpu.CompilerParams(
            dimension_semantics=("arbitrary",), vmem_limit_bytes=VMEM_LIMIT),
        name="expert_ffn",
    )(*plan, xs_local.reshape(n_blocks, LOCAL_BLOCK, half), wg, wu, wd)


def _combine_kernel(h_ref, wcol_ref, y_ref, o_ref):
    y_lo, y_hi = _unpack_bf16_pairs(y_ref[...])
    w = wcol_ref[...]
    jiota = lax.broadcasted_iota(jnp.int32, (MERGE_TM, LOCAL_ROWS), 1).astype(F32)
    pick = (jnp.where(jiota == w[:, 2:3], w[:, 0:1], 0.0) + jnp.where(jiota == w[:, 3:4], w[:, 1:2], 0.0)).astype(BF16)
    half = D_MODEL // 2
    for cols, y in ((slice(0, half), y_lo), (slice(half, D_MODEL), y_hi)):
        o_ref[:, cols] = h_ref[:, cols] + jnp.dot(pick, y, preferred_element_type=F32)


def _combine(h, wcol, y_local):
    t = h.shape[0]
    return pl.pallas_call(
        _combine_kernel,
        grid=(t // MERGE_TM,),
        in_specs=[
            pl.BlockSpec((MERGE_TM, D_MODEL), lambda i: (i, 0)),
            pl.BlockSpec((MERGE_TM, ROUTER_LANES), lambda i: (i, 0)),
            pl.BlockSpec((LOCAL_ROWS, D_MODEL // 2), lambda i: (i, 0)),
        ],
        out_specs=pl.BlockSpec((MERGE_TM, D_MODEL), lambda i: (i, 0)),
        out_shape=jax.ShapeDtypeStruct((t, D_MODEL), F32),
        compiler_params=pltpu.CompilerParams(
            dimension_semantics=("parallel",), vmem_limit_bytes=VMEM_LIMIT),
        name="combine",
    )(h, wcol, y_local)


def _dispatch_plan(block_expert):
    n_blocks = block_expert.size
    n_keys = N_EXPERTS + 1
    n_tiles = n_blocks // MOE_BLOCKS + n_keys
    keys = block_expert.reshape(n_blocks)
    order = jnp.argsort(keys, stable=True).astype(jnp.int32)
    counts = jnp.sum(keys[None, :] == jnp.arange(n_keys, dtype=jnp.int32)[:, None], axis=1).astype(jnp.int32)
    tiles_per = (counts + MOE_BLOCKS - 1) // MOE_BLOCKS
    tile_end = jnp.cumsum(tiles_per)
    tile_first = tile_end - tiles_per
    start = jnp.cumsum(counts) - counts
    tile = jnp.arange(n_tiles, dtype=jnp.int32)
    tile_expert = jnp.minimum(jnp.sum(tile_end[None, :] <= tile[:, None], axis=1), n_keys - 1).astype(jnp.int32)
    sel = (tile_expert[:, None] == jnp.arange(n_keys, dtype=jnp.int32)[None, :]).astype(jnp.int32)
    pick = lambda v: jnp.sum(sel * v[None, :], axis=1)
    within = (tile - pick(tile_first)) * MOE_BLOCKS
    tile_base = jnp.clip(pick(start) + within, 0, n_blocks).astype(jnp.int32)
    tile_blocks = jnp.clip(pick(counts) - within, 0, MOE_BLOCKS).astype(jnp.int32)
    n_used = tile_end[-1].astype(jnp.int32).reshape(1)
    order = jnp.concatenate([order, jnp.zeros((MOE_BLOCKS,), jnp.int32)])
    key = jnp.arange(n_keys, dtype=jnp.int32)
    present = jnp.logical_and(counts > 0, key < N_EXPERTS)
    slot_of = (jnp.cumsum(present) - present) % 2
    later = jnp.logical_and(key[None, :] > key[:, None], present[None, :])
    next_of = jnp.min(jnp.where(later, key[None, :], N_EXPERTS), axis=1)
    return (order, tile_expert, tile_base, tile_blocks, pick(slot_of).astype(jnp.int32),
            pick(next_of).astype(jnp.int32), n_used)


def _layer(x, norm1_g, w_in, q_norm_a, k_norm_a, sink_a, q_norm_b, k_norm_b, rpb_b, w_up_a, w_up_b, w_out,
           rel_table, norm2_g, w_rg, b_rg, w_re, b_re, w_gate, w_up, w_down):
    batch, seq, d = x.shape
    t = batch * seq
    x2 = x.reshape(t, d)

    kv_a = slice(A_WIDTH, A_WIDTH + 2 * A_KV_WIDTH)
    w_cat = jnp.concatenate([w_in[:, :kv_a.start], w_in[:, kv_a.stop:], w_in[:, kv_a]], axis=1).astype(BF16)
    scale = HEAD_DIM ** -0.5
    ones = lambda n: jnp.ones((n,), F32)
    gain_cat = jnp.concatenate([
        jnp.tile(q_norm_a.astype(F32) * scale, A_HEADS), jnp.tile(q_norm_b.astype(F32) * scale, B_HEADS),
        jnp.tile(k_norm_b.astype(F32), B_HEADS), ones(B_WIDTH + 2 * D_MODEL),
        jnp.tile(k_norm_a.astype(F32), A_KV_HEADS), ones(A_KV_WIDTH)]).reshape(1, IN_COLS)

    kind = lambda v, n: jnp.full((n,), v, F32)
    kind_cat = jnp.concatenate([kind(1.0, COL_VB), kind(0.0, B_WIDTH), kind(2.0, 2 * D_MODEL),
                                kind(1.0, A_KV_WIDTH), kind(0.0, A_KV_WIDTH)]).reshape(1, IN_COLS)

    proj = _in_projection(x2, norm1_g.astype(F32).reshape(1, d), w_cat, gain_cat, kind_cat)
    oa = _window_attention(proj, _window_bias(rel_table), sink_a.astype(F32), batch, seq)
    ob = _neighbourhood_attention(proj, _neighbourhood_col_bias(rpb_b), batch, seq)

    wr = jnp.concatenate([w_re.astype(F32), w_rg.astype(F32)], axis=1)
    wr_hi = wr.astype(BF16)
    wr_lo = (wr - wr_hi.astype(F32)).astype(BF16)
    half = ROUTER_LANES // 2
    padc = lambda a: jnp.pad(a, ((0, 0), (0, half - a.shape[1])))
    wr_cat = jnp.concatenate([padc(wr_hi), padc(wr_lo)], axis=1)
    br = jnp.pad(jnp.concatenate([b_re.astype(F32), b_rg.astype(F32)]), (0, ROUTER_LANES - N_EXPERTS - N_GROUPS))

    h, xs_local, block_expert, wcol = _merge_and_route(
        oa, ob, proj, x2, w_up_a.astype(BF16), w_up_b.astype(BF16), w_out.astype(BF16),
        norm2_g.astype(F32).reshape(1, d), wr_cat, br.reshape(1, ROUTER_LANES))

    plan = _dispatch_plan(block_expert[:, 0, :LOCAL_BLOCKS])
    y_local = _expert_ffn(plan, xs_local, w_gate.astype(F32), w_up.astype(F32), w_down.astype(F32))
    out = _combine(h, wcol, y_local.reshape(-1, d // 2))
    return out.reshape(batch, seq, d)


def kernel(x, norm1_g, w_in, q_norm_a, k_norm_a, sink_a, q_norm_b, k_norm_b, rpb_b, w_up_a, w_up_b, w_out,
           rel_bias_table, norm2_g, w_router_group, b_router_group, w_router_expert, b_router_expert,
           w_gate, w_up, w_down):
    return _layer(x, norm1_g[0], w_in[0], q_norm_a[0], k_norm_a[0], sink_a[0], q_norm_b[0], k_norm_b[0],
                  rpb_b[0], w_up_a[0], w_up_b[0], w_out[0], rel_bias_table, norm2_g[0],
                  w_router_group[0], b_router_group[0], w_router_expert[0], b_router_expert[0],
                  w_gate[0], w_up[0], w_down[0])
```

```python
import functools
import math

import jax
import jax.numpy as jnp
from jax import lax
from jax.experimental import pallas as pl
from jax.experimental.pallas import tpu as pltpu

F32 = jnp.float32
BF16 = jnp.bfloat16

D_MODEL = 2048
HEAD_DIM = 128
A_HEADS = 8
A_KV_HEADS = 2
A_GROUP = A_HEADS // A_KV_HEADS
A_WIDTH = A_HEADS * HEAD_DIM
A_KV_WIDTH = A_KV_HEADS * HEAD_DIM
WINDOW = 128
A_BLOCK = 128
NUM_BUCKETS = 32
MAX_DISTANCE = 128
B_HEADS = 8
B_WIDTH = B_HEADS * HEAD_DIM
GRID_W = 64
NB_ROWS = 8
NB_COLS = 16
IN_COLS = A_WIDTH + 2 * A_KV_WIDTH + 3 * B_WIDTH + 2 * D_MODEL
N_GROUPS = 4
EXPERTS_PER_GROUP = 8
N_EXPERTS = N_GROUPS * EXPERTS_PER_GROUP
TOP_K = 2
D_EXPERT = 512
EPS = 1e-6
NEG_INF = -1e30

COL_QA = 0
COL_QB = COL_QA + A_WIDTH
COL_KB = COL_QB + B_WIDTH
COL_VB = COL_KB + B_WIDTH
COL_GA = COL_VB + B_WIDTH
COL_GB = COL_GA + D_MODEL
COL_KA = COL_GB + D_MODEL
COL_VA = COL_KA + A_KV_WIDTH

PROJ_TM = 1024
PROJ_TN = 512
PROJ_SUB = 128
WIN_STEP = 4
NA_ROWS = 4
NA_BLOCK = NA_ROWS * GRID_W
MERGE_TM = 256
ROUTER_LANES = 128
LOCAL_BLOCK = 8
LOCAL_ROWS = 768
LOCAL_BLOCKS = LOCAL_ROWS // LOCAL_BLOCK
assert LOCAL_ROWS >= TOP_K * MERGE_TM + N_EXPERTS * (LOCAL_BLOCK - 1) and LOCAL_BLOCKS <= ROUTER_LANES
MOE_TM = 256
MOE_BLOCKS = MOE_TM // LOCAL_BLOCK
VMEM_LIMIT = 56 * 1024 * 1024


def _head_norm(a, gain):
    outs = []
    for hd in range(a.shape[1] // HEAD_DIM):
        sl = slice(hd * HEAD_DIM, (hd + 1) * HEAD_DIM)
        ah = a[:, sl]
        ss = jnp.sum(ah * ah, axis=-1, keepdims=True)
        outs.append(ah * lax.rsqrt(ss * (1.0 / HEAD_DIM) + EPS) * gain[:, sl])
    return outs[0] if len(outs) == 1 else jnp.concatenate(outs, axis=1)


def _inproj_kernel(x_hbm, g1_ref, w_ref, gain_ref, kind_ref, o_ref, xn_ref, xbuf, xsem):
    i = pl.program_id(0)
    j = pl.program_id(1)

    def x_copy(tile, slot):
        return pltpu.make_async_copy(x_hbm.at[pl.ds(tile * PROJ_TM, PROJ_TM)], xbuf.at[slot], xsem.at[slot])

    @pl.when(j == 0)
    def _():
        @pl.when(i == 0)
        def _():
            x_copy(0, 0).start()

        x_copy(i, lax.rem(i, 2)).wait()

        @pl.when(i + 1 < pl.num_programs(0))
        def _():
            x_copy(i + 1, lax.rem(i + 1, 2)).start()

    def project(x_slot):
        kind = kind_ref[...]
        for r in range(PROJ_TM // PROJ_SUB):
            rows = slice(r * PROJ_SUB, (r + 1) * PROJ_SUB)
            if x_slot is not None:
                x = xbuf[x_slot, rows, :]
                ms = jnp.mean(x * x, axis=-1, keepdims=True)
                xn_ref[rows, :] = (x * lax.rsqrt(ms + EPS) * g1_ref[...]).astype(BF16)
            acc = jnp.dot(xn_ref[rows, :], w_ref[...], preferred_element_type=F32)
            out = jnp.where(kind == 1.0, _head_norm(acc, gain_ref[...]),
                            jnp.where(kind == 2.0, jax.nn.sigmoid(acc), acc))
            o_ref[rows, :] = out.astype(o_ref.dtype)

    for x_slot in (0, 1):
        pl.when(jnp.logical_and(j == 0, lax.rem(i, 2) == x_slot))(functools.partial(project, x_slot))
    pl.when(j != 0)(functools.partial(project, None))


def _in_projection(x2, g1, w_cat, gain_cat, kind_cat):
    t = x2.shape[0]
    return pl.pallas_call(
        _inproj_kernel,
        grid=(t // PROJ_TM, IN_COLS // PROJ_TN),
        in_specs=[
            pl.BlockSpec(memory_space=pl.ANY),
            pl.BlockSpec((1, D_MODEL), lambda i, j: (0, 0)),
            pl.BlockSpec((D_MODEL, PROJ_TN), lambda i, j: (0, j)),
            pl.BlockSpec((1, PROJ_TN), lambda i, j: (0, j)),
            pl.BlockSpec((1, PROJ_TN), lambda i, j: (0, j)),
        ],
        out_specs=pl.BlockSpec((PROJ_TM, PROJ_TN), lambda i, j: (i, j)),
        out_shape=jax.ShapeDtypeStruct((t, IN_COLS), BF16),
        scratch_shapes=[pltpu.VMEM((PROJ_TM, D_MODEL), BF16), pltpu.VMEM((2, PROJ_TM, D_MODEL), F32),
                        pltpu.SemaphoreType.DMA((2,))],
        compiler_params=pltpu.CompilerParams(
            dimension_semantics=("arbitrary", "arbitrary"), vmem_limit_bytes=VMEM_LIMIT),
        name="in_projection",
    )(x2, g1, w_cat, gain_cat, kind_cat)


def _t5_bucket(rel):
    nb = NUM_BUCKETS // 2
    ret = jnp.where(rel > 0, nb, 0)
    n = jnp.abs(rel)
    max_exact = nb // 2
    nf = jnp.maximum(n, 1).astype(jnp.float32)
    large = max_exact + (jnp.log(nf / max_exact) / math.log(MAX_DISTANCE / max_exact)
                         * (nb - max_exact)).astype(jnp.int32)
    large = jnp.minimum(large, nb - 1)
    return ret + jnp.where(n < max_exact, n, large)


def _window_bias(rel_table):
    q_loc = jnp.arange(A_BLOCK)
    k_loc = jnp.arange(3 * A_BLOCK) - WINDOW
    rel = k_loc[None, :] - q_loc[:, None]
    onehot = (_t5_bucket(rel)[..., None] == jnp.arange(NUM_BUCKETS)).astype(F32)
    bias = jnp.einsum('qkn,nh->hqk', onehot, rel_table.astype(F32), precision=lax.Precision.HIGHEST)
    in_window = jnp.abs(rel) <= WINDOW
    has_prev = jnp.array([False, True, True])[:, None, None]
    has_next = jnp.array([True, True, False])[:, None, None]
    k_blk = (jnp.arange(3 * A_BLOCK) // A_BLOCK)[None, None, :]
    valid = in_window[None] & ((k_blk != 0) | has_prev) & ((k_blk != 2) | has_next)
    full = jnp.where(valid[:, None], bias[None], NEG_INF)
    return full.reshape(3, A_KV_HEADS, A_GROUP * A_BLOCK, 3 * A_BLOCK)


def _window_kernel(sink_ref, q_ref, kp_ref, ko_ref, kn_ref, vp_ref, vo_ref, vn_ref,
                   bias_lo_ref, bias_mid_ref, bias_hi_ref, o_ref):
    for kvh in range(A_KV_HEADS):
        hs = slice(kvh * HEAD_DIM, (kvh + 1) * HEAD_DIM)
        heads = [kvh * A_GROUP + g for g in range(A_GROUP)]
        kc = jnp.concatenate([kp_ref[:, hs], ko_ref[:, hs], kn_ref[:, hs]], axis=0)
        vc = jnp.concatenate([vp_ref[:, hs], vo_ref[:, hs], vn_ref[:, hs]], axis=0)
        vc1 = jnp.concatenate([vc, jnp.ones_like(vc)], axis=1)
        sink = jnp.concatenate([jnp.full((A_BLOCK, 1), sink_ref[h], F32) for h in heads], axis=0)
        for jq in range(WIN_STEP):
            rows = slice(jq * A_BLOCK, (jq + 1) * A_BLOCK)
            band = slice(jq * A_BLOCK, (jq + 3) * A_BLOCK)
            bias_ref = bias_lo_ref if jq == 0 else (bias_hi_ref if jq == WIN_STEP - 1 else bias_mid_ref)
            q4 = jnp.concatenate([q_ref[rows, h * HEAD_DIM:(h + 1) * HEAD_DIM] for h in heads], axis=0)
            s = lax.dot_general(q4, kc[band], (((1,), (1,)), ((), ())), preferred_element_type=F32)
            s = s + bias_ref[0, kvh]
            m = jnp.max(s, axis=-1, keepdims=True)
            p = jnp.exp((s - m).astype(BF16))
            ov = jnp.dot(p, vc1[band], preferred_element_type=F32)
            o = ov[:, :HEAD_DIM] / (ov[:, HEAD_DIM:] + jnp.exp(sink - m))
            for g, h in enumerate(heads):
                o_ref[rows, h * HEAD_DIM:(h + 1) * HEAD_DIM] = o[g * A_BLOCK:(g + 1) * A_BLOCK].astype(o_ref.dtype)


def _window_attention(proj, bias_a, sink, batch, seq):
    nb = seq // A_BLOCK
    ns = nb // WIN_STEP
    kcol, vcol = COL_KA // A_KV_WIDTH, COL_VA // A_KV_WIDTH
    prev = lambda b, n: b * nb + jnp.maximum(n * WIN_STEP - 1, 0)
    nxt = lambda b, n: b * nb + jnp.minimum((n + 1) * WIN_STEP, nb - 1)
    own = lambda b, n: b * ns + n
    edge = lambda rfn, col: pl.BlockSpec((A_BLOCK, A_KV_WIDTH), lambda b, n: (rfn(b, n), col))
    main = lambda col: pl.BlockSpec((WIN_STEP * A_BLOCK, A_KV_WIDTH), lambda b, n: (own(b, n), col))
    bias_block = (1, A_KV_HEADS, A_GROUP * A_BLOCK, 3 * A_BLOCK)
    return pl.pallas_call(
        _window_kernel,
        grid=(batch, ns),
        in_specs=[
            pl.BlockSpec(memory_space=pltpu.SMEM),
            pl.BlockSpec((WIN_STEP * A_BLOCK, A_WIDTH), lambda b, n: (own(b, n), COL_QA // A_WIDTH)),
            edge(prev, kcol), main(kcol), edge(nxt, kcol),
            edge(prev, vcol), main(vcol), edge(nxt, vcol),
            pl.BlockSpec(bias_block, lambda b, n: (jnp.where(n == 0, 0, 1), 0, 0, 0)),
            pl.BlockSpec(bias_block, lambda b, n: (1, 0, 0, 0)),
            pl.BlockSpec(bias_block, lambda b, n: (jnp.where(n == ns - 1, 2, 1), 0, 0, 0)),
        ],
        out_specs=pl.BlockSpec((WIN_STEP * A_BLOCK, A_WIDTH), lambda b, n: (own(b, n), 0)),
        out_shape=jax.ShapeDtypeStruct((batch * seq, A_WIDTH), BF16),
        compiler_params=pltpu.CompilerParams(
            dimension_semantics=("parallel", "arbitrary"), vmem_limit_bytes=VMEM_LIMIT),
        name="window_attention",
    )(sink, proj, proj, proj, proj, proj, proj, proj, bias_a, bias_a, bias_a)


def _neighbourhood_col_bias(rpb):
    cols = jnp.arange(GRID_W)
    cs = jnp.clip(cols - NB_COLS // 2, 0, GRID_W - NB_COLS)
    col_ok = (cols[None, :] >= cs[:, None]) & (cols[None, :] < cs[:, None] + NB_COLS)
    dc = cols[None, :] - cols[:, None] + (NB_COLS - 1)
    col_sel = ((dc[..., None] == jnp.arange(2 * NB_COLS - 1)) & col_ok[..., None]).astype(F32)
    by_col = jnp.einsum('hde,cje->hdcj', rpb.astype(F32), col_sel, precision=lax.Precision.HIGHEST)
    return jnp.where(col_ok[None, None], by_col, NEG_INF)


def _neighbourhood_row_plan(cls, rows):
    kh = min(NB_ROWS, rows)
    r0 = (0, NA_ROWS, rows - NA_ROWS)[cls]
    plan = []
    for i in range(NA_ROWS):
        r = r0 + i
        rs = min(max(r - kh // 2, 0), rows - kh)
        plan.append([key_r - r + NB_ROWS - 1 if rs <= key_r < rs + kh else None
                     for key_r in range(r0 - NA_ROWS, r0 + 2 * NA_ROWS)])
    return plan


def _neighbourhood_kernel(q_ref, kp_ref, ko_ref, kn_ref, vp_ref, vo_ref, vn_ref, colbias_ref, o_ref, bias_ref,
                          *, rows):
    n = pl.program_id(1)
    n_blocks = pl.num_programs(1)

    def build_bias(cls):
        plan = _neighbourhood_row_plan(cls, rows)
        masked = jnp.full((GRID_W, GRID_W), NEG_INF, F32)
        for h in range(B_HEADS):
            for i in range(NA_ROWS):
                for pair in range(3 * NA_ROWS // 2):
                    tiles = [masked if plan[i][kr] is None else colbias_ref[h, plan[i][kr]]
                             for kr in (2 * pair, 2 * pair + 1)]
                    bias_ref[h, i * GRID_W:(i + 1) * GRID_W, pair * 2 * GRID_W:(pair + 1) * 2 * GRID_W] = (
                        jnp.concatenate(tiles, axis=1))

    pl.when(n == 0)(functools.partial(build_bias, 0))
    pl.when(n == 1)(functools.partial(build_bias, 1))
    pl.when(n == n_blocks - 1)(functools.partial(build_bias, 2))

    for h in range(B_HEADS):
        hs = slice(h * HEAD_DIM, (h + 1) * HEAD_DIM)
        kc = jnp.concatenate([kp_ref[:, hs], ko_ref[:, hs], kn_ref[:, hs]], axis=0)
        vc = jnp.concatenate([vp_ref[:, hs], vo_ref[:, hs], vn_ref[:, hs]], axis=0)
        s = lax.dot_general(q_ref[:, hs], kc, (((1,), (1,)), ((), ())), preferred_element_type=F32)
        s = s + bias_ref[h]
        m = jnp.max(s, axis=-1, keepdims=True)
        p = jnp.exp((s - m).astype(BF16))
        ov = jnp.dot(p, jnp.concatenate([vc, jnp.ones_like(vc)], axis=1), preferred_element_type=F32)
        o_ref[:, hs] = (ov[:, :HEAD_DIM] / ov[:, HEAD_DIM:]).astype(o_ref.dtype)


def _neighbourhood_attention(proj, col_bias, batch, seq):
    nb = seq // NA_BLOCK
    assert nb >= 2

    def row(b, n):
        return b * nb + n

    prev = lambda b, n: row(b, jnp.maximum(n - 1, 0))
    nxt = lambda b, n: row(b, jnp.minimum(n + 1, nb - 1))
    spec = lambda rfn, col: pl.BlockSpec((NA_BLOCK, B_WIDTH), lambda b, n: (rfn(b, n), col))
    qcol, kcol, vcol = COL_QB // B_WIDTH, COL_KB // B_WIDTH, COL_VB // B_WIDTH
    return pl.pallas_call(
        functools.partial(_neighbourhood_kernel, rows=seq // GRID_W),
        grid=(batch, nb),
        in_specs=[
            spec(row, qcol),
            spec(prev, kcol), spec(row, kcol), spec(nxt, kcol),
            spec(prev, vcol), spec(row, vcol), spec(nxt, vcol),
            pl.BlockSpec(col_bias.shape, lambda b, n: (0, 0, 0, 0), pipeline_mode=pl.Buffered(1)),
        ],
        out_specs=pl.BlockSpec((NA_BLOCK, B_WIDTH), lambda b, n: (row(b, n), 0)),
        out_shape=jax.ShapeDtypeStruct((batch * seq, B_WIDTH), BF16),
        scratch_shapes=[pltpu.VMEM((B_HEADS, NA_BLOCK, 3 * NA_BLOCK), F32)],
        compiler_params=pltpu.CompilerParams(
            dimension_semantics=("arbitrary", "arbitrary"), vmem_limit_bytes=VMEM_LIMIT),
        name="neighbourhood_attention",
    )(proj, proj, proj, proj, proj, proj, proj, col_bias)


def _pack_bf16_pairs(a):
    n = a.shape[1] // 2
    bits = lax.bitcast_convert_type(a, jnp.uint32)
    return bits[:, n:] | (bits[:, :n] >> 16)


def _unpack_bf16_pairs(p):
    lo = lax.bitcast_convert_type(p << 16, F32).astype(BF16)
    hi = lax.bitcast_convert_type(p & jnp.uint32(0xFFFF0000), F32).astype(BF16)
    return lo, hi


def _merge_kernel(oa_ref, ob_ref, ga_ref, gb_ref, x_ref, wa_ref, wb_ref, wo_ref, g2_ref, wr_ref, br_ref,
                  upper_ref, lower_ref, h_ref, xs_ref, eb_ref, wcol_ref, hn_scr):
    s = pl.program_id(0)
    cur = lax.rem(s, 2)
    prev = 1 - cur
    tm = MERGE_TM

    @pl.when(s == 0)
    def _():
        hn_scr[...] = jnp.zeros_like(hn_scr)

    hn_prev = hn_scr[prev]
    logits = _router_logits(hn_prev, wr_ref, br_ref)
    ua = jnp.dot(oa_ref[...], wa_ref[...], preferred_element_type=F32)
    ub = jnp.dot(ob_ref[...], wb_ref[...], preferred_element_type=F32)
    ids, w1, w2 = _route(logits)
    lrow, eb = _local_sort(ids, upper_ref, lower_ref)
    merged = ga_ref[...].astype(F32) * ua + gb_ref[...].astype(F32) * ub
    acc = jnp.dot(merged.astype(BF16), wo_ref[...], preferred_element_type=F32)
    xs = _place(lrow, hn_prev)
    h = x_ref[...] + acc
    h_ref[...] = h
    ms = jnp.mean(h * h, axis=-1, keepdims=True)
    hn_scr[cur] = (h * lax.rsqrt(ms + EPS) * g2_ref[...]).astype(BF16)

    xs_ref[...] = _pack_bf16_pairs(xs)
    eb_ref[0] = eb
    riota = lax.broadcasted_iota(jnp.int32, (ROUTER_LANES, tm), 0)
    wrows = jnp.where(riota == 0, w1, jnp.where(riota == 1, w2,
                      jnp.where(riota == 2, lrow[:, :tm], jnp.where(riota == 3, lrow[:, tm:], 0.0))))
    wcol_ref[...] = wrows.T


def _router_logits(hn, wr_ref, br_ref):
    lg = jnp.dot(hn, wr_ref[...], preferred_element_type=F32)
    return lg + pltpu.roll(lg, ROUTER_LANES // 2, axis=1) + br_ref[...]


def _route(logits):
    tm = logits.shape[0]
    lt = logits.T
    gl = lt[N_EXPERTS:N_EXPERTS + N_GROUPS]
    gmax = jnp.max(gl, axis=0, keepdims=True)
    giota = lax.broadcasted_iota(jnp.int32, gl.shape, 0)
    g_idx = jnp.min(jnp.where(gl == gmax, giota, N_GROUPS), axis=0, keepdims=True)
    p_g = 1.0 / jnp.sum(jnp.exp(gl - gmax), axis=0, keepdims=True)
    e_sel = jnp.zeros((EXPERTS_PER_GROUP, tm), F32)
    for g in range(N_GROUPS):
        e_sel = jnp.where(g_idx == g, lt[g * EXPERTS_PER_GROUP:(g + 1) * EXPERTS_PER_GROUP], e_sel)
    eiota = lax.broadcasted_iota(jnp.int32, e_sel.shape, 0)
    m1 = jnp.max(e_sel, axis=0, keepdims=True)
    i1 = jnp.min(jnp.where(e_sel == m1, eiota, EXPERTS_PER_GROUP), axis=0, keepdims=True)
    rest = jnp.where(eiota == i1, -jnp.inf, e_sel)
    m2 = jnp.max(rest, axis=0, keepdims=True)
    i2 = jnp.min(jnp.where(rest == m2, eiota, EXPERTS_PER_GROUP), axis=0, keepdims=True)
    t = jnp.exp(m2 - m1)
    ids = jnp.concatenate([g_idx * EXPERTS_PER_GROUP + i1, g_idx * EXPERTS_PER_GROUP + i2], axis=1)
    return ids, p_g / (1.0 + t), p_g * t / (1.0 + t)


def _local_sort(ids, upper_ref, lower_ref):
    eiota32 = lax.broadcasted_iota(jnp.int32, (N_EXPERTS, ids.shape[1]), 0)
    onehot = jnp.where(eiota32 == ids, 1.0, 0.0)
    count = jnp.sum(onehot, axis=1, keepdims=True)
    before = jnp.dot(onehot.astype(BF16), upper_ref[...], preferred_element_type=F32)
    nblk = jnp.floor((count + (LOCAL_BLOCK - 1)) * (1.0 / LOCAL_BLOCK))
    boff = jnp.dot(lower_ref[...], jnp.broadcast_to(nblk, (N_EXPERTS, ROUTER_LANES)).astype(BF16),
                   preferred_element_type=F32)[:, 0:1]
    lrow = jnp.sum(onehot * (before + LOCAL_BLOCK * boff), axis=0, keepdims=True)
    biota = lax.broadcasted_iota(jnp.int32, (N_EXPERTS, ROUTER_LANES), 1).astype(F32)
    owns = jnp.where(jnp.logical_and(biota >= boff, biota < boff + nblk), 1.0, 0.0)
    eiota_f = lax.broadcasted_iota(jnp.int32, (N_EXPERTS, ROUTER_LANES), 0).astype(F32)
    eb = jnp.sum(owns * eiota_f, axis=0, keepdims=True) + N_EXPERTS * (1.0 - jnp.sum(owns, axis=0, keepdims=True))
    return lrow, eb.astype(jnp.int32)


def _place(lrow, hn_hi):
    tm = hn_hi.shape[0]
    jiota = lax.broadcasted_iota(jnp.int32, (LOCAL_ROWS, tm), 0).astype(F32)
    place = jnp.where(jiota == lrow[:, :tm], 1.0, jnp.where(jiota == lrow[:, tm:], 1.0, 0.0)).astype(BF16)
    return jnp.dot(place, hn_hi, preferred_element_type=F32)


def _merge_and_route(oa, ob, proj, x2, wa, wb, wo, g2, wr_cat, br):
    t = x2.shape[0]
    n_assign = TOP_K * MERGE_TM
    upper = (jnp.arange(n_assign)[:, None] < jnp.arange(n_assign)[None, :]).astype(BF16)
    lower = (jnp.arange(N_EXPERTS)[:, None] > jnp.arange(N_EXPERTS)[None, :]).astype(BF16)
    const = lambda shape: pl.BlockSpec(shape, lambda s: (0, 0), pipeline_mode=pl.Buffered(1))
    n_tiles = t // MERGE_TM
    cur = lambda s: jnp.minimum(s, n_tiles - 1)
    prev = lambda s: jnp.maximum(s - 1, 0)
    return pl.pallas_call(
        _merge_kernel,
        grid=(n_tiles + 1,),
        in_specs=[
            pl.BlockSpec((MERGE_TM, A_WIDTH), lambda s: (cur(s), 0)),
            pl.BlockSpec((MERGE_TM, B_WIDTH), lambda s: (cur(s), 0)),
            pl.BlockSpec((MERGE_TM, D_MODEL), lambda s: (cur(s), COL_GA // D_MODEL)),
            pl.BlockSpec((MERGE_TM, D_MODEL), lambda s: (cur(s), COL_GB // D_MODEL)),
            pl.BlockSpec((MERGE_TM, D_MODEL), lambda s: (cur(s), 0)),
            const((A_WIDTH, D_MODEL)), const((B_WIDTH, D_MODEL)), const((D_MODEL, D_MODEL)),
            const((1, D_MODEL)), const((D_MODEL, ROUTER_LANES)), const((1, ROUTER_LANES)),
            const((n_assign, n_assign)), const((N_EXPERTS, N_EXPERTS)),
        ],
        out_specs=[
            pl.BlockSpec((MERGE_TM, D_MODEL), lambda s: (cur(s), 0)),
            pl.BlockSpec((LOCAL_ROWS, D_MODEL // 2), lambda s: (prev(s), 0)),
            pl.BlockSpec((1, 1, ROUTER_LANES), lambda s: (prev(s), 0, 0)),
            pl.BlockSpec((MERGE_TM, ROUTER_LANES), lambda s: (prev(s), 0)),
        ],
        scratch_shapes=[pltpu.VMEM((2, MERGE_TM, D_MODEL), BF16)],
        out_shape=[
            jax.ShapeDtypeStruct((t, D_MODEL), F32),
            jax.ShapeDtypeStruct((t // MERGE_TM * LOCAL_ROWS, D_MODEL // 2), jnp.uint32),
            jax.ShapeDtypeStruct((t // MERGE_TM, 1, ROUTER_LANES), jnp.int32),
            jax.ShapeDtypeStruct((t, ROUTER_LANES), F32),
        ],
        compiler_params=pltpu.CompilerParams(
            dimension_semantics=("arbitrary",), vmem_limit_bytes=VMEM_LIMIT),
        name="merge_and_route",
    )(oa, ob, proj, proj, x2, wa, wb, wo, g2, wr_cat, br, upper, lower)


def _expert_kernel(blk_ref, texp_ref, tcb_ref, tnb_ref, wslot_ref, wnext_ref, nused_ref,
                   xs_hbm, wg_hbm, wu_hbm, wd_hbm, y_hbm,
                   xbuf, ybuf, wgf, wuf, wdf, wgb, wub, wdb, gsem, ssem, wsem):
    n_used = nused_ref[0]

    def weight_copies(e, ws):
        return [pltpu.make_async_copy(src.at[e], dst.at[ws], wsem.at[ws])
                for src, dst in ((wg_hbm, wgf), (wu_hbm, wuf), (wd_hbm, wdf))]

    def gather_copy(blk, dst_slot, j):
        return pltpu.make_async_copy(xs_hbm.at[blk], xbuf.at[dst_slot, pl.ds(j * LOCAL_BLOCK, LOCAL_BLOCK)],
                                     gsem.at[dst_slot])

    def scatter_copy(blk, src_slot, j):
        return pltpu.make_async_copy(ybuf.at[src_slot, pl.ds(j * LOCAL_BLOCK, LOCAL_BLOCK)], y_hbm.at[blk],
                                     ssem.at[src_slot])

    def issue_gather(tile, dst_slot):
        @pl.when(texp_ref[tile] < N_EXPERTS)
        def _():
            base = tcb_ref[tile]
            for j in range(MOE_BLOCKS):
                gather_copy(blk_ref[base + j], dst_slot, j).start()

    def wait_gather(tile, s):
        @pl.when(texp_ref[tile] < N_EXPERTS)
        def _():
            for j in range(MOE_BLOCKS):
                gather_copy(0, s, j).wait()

    def for_valid_blocks(tile, fn):
        n_valid = tnb_ref[tile]

        @pl.when(n_valid == MOE_BLOCKS)
        def _():
            for j in range(MOE_BLOCKS):
                fn(j)

        @pl.when(n_valid < MOE_BLOCKS)
        def _():
            for j in range(MOE_BLOCKS):
                pl.when(j < n_valid)(functools.partial(fn, j))

    def issue_scatter(tile, src_slot):
        base = tcb_ref[tile]
        for_valid_blocks(tile, lambda j: scatter_copy(blk_ref[base + j], src_slot, j).start())

    def wait_scatter(tile, s):
        for_valid_blocks(tile, lambda j: scatter_copy(0, s, j).wait())

    @pl.when(n_used > 0)
    def _():
        issue_gather(0, 0)

        @pl.when(texp_ref[0] < N_EXPERTS)
        def _():
            for cp in weight_copies(texp_ref[0], wslot_ref[0]):
                cp.start()

    def tile_step(i, carry):
        slot = lax.rem(i, 2)
        expert = texp_ref[i]
        wait_gather(i, slot)

        @pl.when(i + 1 < n_used)
        def _():
            issue_gather(i + 1, 1 - slot)

        @pl.when(i >= 2)
        def _():
            wait_scatter(i - 2, slot)

        @pl.when(jnp.logical_and(expert < N_EXPERTS, jnp.logical_or(i == 0, expert != texp_ref[jnp.maximum(i - 1, 0)])))
        def _():
            ws = wslot_ref[i]
            for cp in weight_copies(expert, ws):
                cp.wait()
            for static_ws in (0, 1):
                @pl.when(ws == static_ws)
                def _():
                    wgb[...] = wgf[static_ws].astype(BF16)
                    wub[...] = wuf[static_ws].astype(BF16)
                    wdb[...] = wdf[static_ws].astype(BF16)
            nxt = wnext_ref[i]

            @pl.when(nxt < N_EXPERTS)
            def _():
                for cp in weight_copies(nxt, 1 - ws):
                    cp.start()

        @pl.when(expert < N_EXPERTS)
        def _():
            x_lo, x_hi = _unpack_bf16_pairs(xbuf[slot])
            half = D_MODEL // 2
            g = (jnp.dot(x_lo, wgb[:half, :], preferred_element_type=F32)
                 + jnp.dot(x_hi, wgb[half:, :], preferred_element_type=F32))
            u = (jnp.dot(x_lo, wub[:half, :], preferred_element_type=F32)
                 + jnp.dot(x_hi, wub[half:, :], preferred_element_type=F32))
            act = (g * jax.nn.sigmoid(g) * u).astype(BF16)
            y = jnp.dot(act, wdb[...], preferred_element_type=F32)
            ybuf[slot] = _pack_bf16_pairs(y.astype(BF16).astype(F32))

        @pl.when(expert >= N_EXPERTS)
        def _():
            ybuf[slot] = jnp.zeros((MOE_TM, D_MODEL // 2), jnp.uint32)

        issue_scatter(i, slot)
        return carry

    lax.fori_loop(0, n_used, tile_step, 0)

    for back in (1, 2):
        @pl.when(n_used >= back)
        def _():
            last = jnp.maximum(n_used - back, 0)
            wait_scatter(last, lax.rem(last, 2))


def _expert_ffn(plan, xs_local, wg, wu, wd):
    n_blocks = xs_local.shape[0] // LOCAL_BLOCK
    half = D_MODEL // 2
    any_spec = pl.BlockSpec(memory_space=pl.ANY)
    grid_spec = pltpu.PrefetchScalarGridSpec(
        num_scalar_prefetch=len(plan),
        grid=(1,),
        in_specs=[any_spec, any_spec, any_spec, any_spec],
        out_specs=any_spec,
        scratch_shapes=[
            pltpu.VMEM((2, MOE_TM, half), jnp.uint32), pltpu.VMEM((2, MOE_TM, half), jnp.uint32),
            pltpu.VMEM((2, D_MODEL, D_EXPERT), F32), pltpu.VMEM((2, D_MODEL, D_EXPERT), F32),
            pltpu.VMEM((2, D_EXPERT, D_MODEL), F32),
            pltpu.VMEM((D_MODEL, D_EXPERT), BF16), pltpu.VMEM((D_MODEL, D_EXPERT), BF16),
            pltpu.VMEM((D_EXPERT, D_MODEL), BF16),
            pltpu.SemaphoreType.DMA((2,)), pltpu.SemaphoreType.DMA((2,)), pltpu.SemaphoreType.DMA((2,)),
        ],
    )
    return pl.pallas_call(
        _expert_kernel,
        grid_spec=grid_spec,
        out_shape=jax.ShapeDtypeStruct((n_blocks, LOCAL_BLOCK, half), jnp.uint32),
        compiler_params=pltpu.CompilerParams(
            dimension_semantics=("arbitrary",), vmem_limit_bytes=VMEM_LIMIT),
        name="expert_ffn",
    )(*plan, xs_local.reshape(n_blocks, LOCAL_BLOCK, half), wg, wu, wd)


def _combine_kernel(h_ref, wcol_ref, y_ref, o_ref):
    y_lo, y_hi = _unpack_bf16_pairs(y_ref[...])
    w = wcol_ref[...]
    jiota = lax.broadcasted_iota(jnp.int32, (MERGE_TM, LOCAL_ROWS), 1).astype(F32)
    pick = (jnp.where(jiota == w[:, 2:3], w[:, 0:1], 0.0) + jnp.where(jiota == w[:, 3:4], w[:, 1:2], 0.0)).astype(BF16)
    half = D_MODEL // 2
    for cols, y in ((slice(0, half), y_lo), (slice(half, D_MODEL), y_hi)):
        o_ref[:, cols] = h_ref[:, cols] + jnp.dot(pick, y, preferred_element_type=F32)


def _combine(h, wcol, y_local):
    t = h.shape[0]
    return pl.pallas_call(
        _combine_kernel,
        grid=(t // MERGE_TM,),
        in_specs=[
            pl.BlockSpec((MERGE_TM, D_MODEL), lambda i: (i, 0)),
            pl.BlockSpec((MERGE_TM, ROUTER_LANES), lambda i: (i, 0)),
            pl.BlockSpec((LOCAL_ROWS, D_MODEL // 2), lambda i: (i, 0)),
        ],
        out_specs=pl.BlockSpec((MERGE_TM, D_MODEL), lambda i: (i, 0)),
        out_shape=jax.ShapeDtypeStruct((t, D_MODEL), F32),
        compiler_params=pltpu.CompilerParams(
            dimension_semantics=("parallel",), vmem_limit_bytes=VMEM_LIMIT),
        name="combine",
    )(h, wcol, y_local)


def _dispatch_plan(block_expert):
    n_blocks = block_expert.size
    n_keys = N_EXPERTS + 1
    n_tiles = n_blocks // MOE_BLOCKS + n_keys
    keys = block_expert.reshape(n_blocks)
    order = jnp.argsort(keys, stable=True).astype(jnp.int32)
    counts = jnp.sum(keys[None, :] == jnp.arange(n_keys, dtype=jnp.int32)[:, None], axis=1).astype(jnp.int32)
    tiles_per = (counts + MOE_BLOCKS - 1) // MOE_BLOCKS
    tile_end = jnp.cumsum(tiles_per)
    tile_first = tile_end - tiles_per
    start = jnp.cumsum(counts) - counts
    tile = jnp.arange(n_tiles, dtype=jnp.int32)
    tile_expert = jnp.minimum(jnp.sum(tile_end[None, :] <= tile[:, None], axis=1), n_keys - 1).astype(jnp.int32)
    sel = (tile_expert[:, None] == jnp.arange(n_keys, dtype=jnp.int32)[None, :]).astype(jnp.int32)
    pick = lambda v: jnp.sum(sel * v[None, :], axis=1)
    within = (tile - pick(tile_first)) * MOE_BLOCKS
    tile_base = jnp.clip(pick(start) + within, 0, n_blocks).astype(jnp.int32)
    tile_blocks = jnp.clip(pick(counts) - within, 0, MOE_BLOCKS).astype(jnp.int32)
    n_used = tile_end[-1].astype(jnp.int32).reshape(1)
    order = jnp.concatenate([order, jnp.zeros((MOE_BLOCKS,), jnp.int32)])
    key = jnp.arange(n_keys, dtype=jnp.int32)
    present = jnp.logical_and(counts > 0, key < N_EXPERTS)
    slot_of = (jnp.cumsum(present) - present) % 2
    later = jnp.logical_and(key[None, :] > key[:, None], present[None, :])
    next_of = jnp.min(jnp.where(later, key[None, :], N_EXPERTS), axis=1)
    return (order, tile_expert, tile_base, tile_blocks, pick(slot_of).astype(jnp.int32),
            pick(next_of).astype(jnp.int32), n_used)


def _layer(x, norm1_g, w_in, q_norm_a, k_norm_a, sink_a, q_norm_b, k_norm_b, rpb_b, w_up_a, w_up_b, w_out,
           rel_table, norm2_g, w_rg, b_rg, w_re, b_re, w_gate, w_up, w_down):
    batch, seq, d = x.shape
    t = batch * seq
    x2 = x.reshape(t, d)

    kv_a = slice(A_WIDTH, A_WIDTH + 2 * A_KV_WIDTH)
    w_cat = jnp.concatenate([w_in[:, :kv_a.start], w_in[:, kv_a.stop:], w_in[:, kv_a]], axis=1).astype(BF16)
    scale = HEAD_DIM ** -0.5
    ones = lambda n: jnp.ones((n,), F32)
    gain_cat = jnp.concatenate([
        jnp.tile(q_norm_a.astype(F32) * scale, A_HEADS), jnp.tile(q_norm_b.astype(F32) * scale, B_HEADS),
        jnp.tile(k_norm_b.astype(F32), B_HEADS), ones(B_WIDTH + 2 * D_MODEL),
        jnp.tile(k_norm_a.astype(F32), A_KV_HEADS), ones(A_KV_WIDTH)]).reshape(1, IN_COLS)

    kind = lambda v, n: jnp.full((n,), v, F32)
    kind_cat = jnp.concatenate([kind(1.0, COL_VB), kind(0.0, B_WIDTH), kind(2.0, 2 * D_MODEL),
                                kind(1.0, A_KV_WIDTH), kind(0.0, A_KV_WIDTH)]).reshape(1, IN_COLS)

    proj = _in_projection(x2, norm1_g.astype(F32).reshape(1, d), w_cat, gain_cat, kind_cat)
    oa = _window_attention(proj, _window_bias(rel_table), sink_a.astype(F32), batch, seq)
    ob = _neighbourhood_attention(proj, _neighbourhood_col_bias(rpb_b), batch, seq)

    wr = jnp.concatenate([w_re.astype(F32), w_rg.astype(F32)], axis=1)
    wr_hi = wr.astype(BF16)
    wr_lo = (wr - wr_hi.astype(F32)).astype(BF16)
    half = ROUTER_LANES // 2
    padc = lambda a: jnp.pad(a, ((0, 0), (0, half - a.shape[1])))
    wr_cat = jnp.concatenate([padc(wr_hi), padc(wr_lo)], axis=1)
    br = jnp.pad(jnp.concatenate([b_re.astype(F32), b_rg.astype(F32)]), (0, ROUTER_LANES - N_EXPERTS - N_GROUPS))

    h, xs_local, block_expert, wcol = _merge_and_route(
        oa, ob, proj, x2, w_up_a.astype(BF16), w_up_b.astype(BF16), w_out.astype(BF16),
        norm2_g.astype(F32).reshape(1, d), wr_cat, br.reshape(1, ROUTER_LANES))

    plan = _dispatch_plan(block_expert[:, 0, :LOCAL_BLOCKS])
    y_local = _expert_ffn(plan, xs_local, w_gate.astype(F32), w_up.astype(F32), w_down.astype(F32))
    out = _combine(h, wcol, y_local.reshape(-1, d // 2))
    return out.reshape(batch, seq, d)


def kernel(x, norm1_g, w_in, q_norm_a, k_norm_a, sink_a, q_norm_b, k_norm_b, rpb_b, w_up_a, w_up_b, w_out,
           rel_bias_table, norm2_g, w_router_group, b_router_group, w_router_expert, b_router_expert,
           w_gate, w_up, w_down):
    return _layer(x, norm1_g[0], w_in[0], q_norm_a[0], k_norm_a[0], sink_a[0], q_norm_b[0], k_norm_b[0],
                  rpb_b[0], w_up_a[0], w_up_b[0], w_out[0], rel_bias_table, norm2_g[0],
                  w_router_group[0], b_router_group[0], w_router_expert[0], b_router_expert[0],
                  w_gate[0], w_up[0], w_down[0])
```

```python
import functools
import math

import jax
import jax.numpy as jnp
from jax import lax
from jax.experimental import pallas as pl
from jax.experimental.pallas import tpu as pltpu

F32 = jnp.float32
BF16 = jnp.bfloat16

D_MODEL = 2048
HEAD_DIM = 128
A_HEADS = 8
A_KV_HEADS = 2
A_GROUP = A_HEADS // A_KV_HEADS
A_WIDTH = A_HEADS * HEAD_DIM
A_KV_WIDTH = A_KV_HEADS * HEAD_DIM
WINDOW = 128
A_BLOCK = 128
NUM_BUCKETS = 32
MAX_DISTANCE = 128
B_HEADS = 8
B_WIDTH = B_HEADS * HEAD_DIM
GRID_W = 64
NB_ROWS = 8
NB_COLS = 16
IN_COLS = A_WIDTH + 2 * A_KV_WIDTH + 3 * B_WIDTH + 2 * D_MODEL
N_GROUPS = 4
EXPERTS_PER_GROUP = 8
N_EXPERTS = N_GROUPS * EXPERTS_PER_GROUP
TOP_K = 2
D_EXPERT = 512
EPS = 1e-6
NEG_INF = -1e30

COL_QA = 0
COL_QB = COL_QA + A_WIDTH
COL_KB = COL_QB + B_WIDTH
COL_VB = COL_KB + B_WIDTH
COL_GA = COL_VB + B_WIDTH
COL_GB = COL_GA + D_MODEL
COL_KA = COL_GB + D_MODEL
COL_VA = COL_KA + A_KV_WIDTH

PROJ_TM = 1024
PROJ_TN = 512
PROJ_SUB = 128
WIN_STEP = 8
NA_ROWS = 4
NA_BLOCK = NA_ROWS * GRID_W
MERGE_TM = 256
ROUTER_LANES = 128
LOCAL_BLOCK = 8
LOCAL_ROWS = 768
LOCAL_BLOCKS = LOCAL_ROWS // LOCAL_BLOCK
assert LOCAL_ROWS >= TOP_K * MERGE_TM + N_EXPERTS * (LOCAL_BLOCK - 1) and LOCAL_BLOCKS <= ROUTER_LANES
MOE_TM = 256
MOE_BLOCKS = MOE_TM // LOCAL_BLOCK
VMEM_LIMIT = 56 * 1024 * 1024


def _head_norm(a, gain):
    outs = []
    for hd in range(a.shape[1] // HEAD_DIM):
        sl = slice(hd * HEAD_DIM, (hd + 1) * HEAD_DIM)
        ah = a[:, sl]
        ss = jnp.sum(ah * ah, axis=-1, keepdims=True)
        outs.append(ah * lax.rsqrt(ss * (1.0 / HEAD_DIM) + EPS) * gain[:, sl])
    return outs[0] if len(outs) == 1 else jnp.concatenate(outs, axis=1)


def _inproj_kernel(x_hbm, g1_ref, w_ref, gain_ref, kind_ref, o_ref, xn_ref, xbuf, xsem):
    i = pl.program_id(0)
    j = pl.program_id(1)

    def x_copy(tile, slot):
        return pltpu.make_async_copy(x_hbm.at[pl.ds(tile * PROJ_TM, PROJ_TM)], xbuf.at[slot], xsem.at[slot])

    @pl.when(j == 0)
    def _():
        @pl.when(i == 0)
        def _():
            x_copy(0, 0).start()

        x_copy(i, lax.rem(i, 2)).wait()

        @pl.when(i + 1 < pl.num_programs(0))
        def _():
            x_copy(i + 1, lax.rem(i + 1, 2)).start()

    def project(x_slot):
        kind = kind_ref[...]
        for r in range(PROJ_TM // PROJ_SUB):
            rows = slice(r * PROJ_SUB, (r + 1) * PROJ_SUB)
            if x_slot is not None:
                x = xbuf[x_slot, rows, :]
                ms = jnp.mean(x * x, axis=-1, keepdims=True)
                xn_ref[rows, :] = (x * lax.rsqrt(ms + EPS) * g1_ref[...]).astype(BF16)
            acc = jnp.dot(xn_ref[rows, :], w_ref[...], preferred_element_type=F32)
            out = jnp.where(kind == 1.0, _head_norm(acc, gain_ref[...]),
                            jnp.where(kind == 2.0, jax.nn.sigmoid(acc), acc))
            o_ref[rows, :] = out.astype(o_ref.dtype)

    for x_slot in (0, 1):
        pl.when(jnp.logical_and(j == 0, lax.rem(i, 2) == x_slot))(functools.partial(project, x_slot))
    pl.when(j != 0)(functools.partial(project, None))


def _in_projection(x2, g1, w_cat, gain_cat, kind_cat):
    t = x2.shape[0]
    return pl.pallas_call(
        _inproj_kernel,
        grid=(t // PROJ_TM, IN_COLS // PROJ_TN),
        in_specs=[
            pl.BlockSpec(memory_space=pl.ANY),
            pl.BlockSpec((1, D_MODEL), lambda i, j: (0, 0)),
            pl.BlockSpec((D_MODEL, PROJ_TN), lambda i, j: (0, j)),
            pl.BlockSpec((1, PROJ_TN), lambda i, j: (0, j)),
            pl.BlockSpec((1, PROJ_TN), lambda i, j: (0, j)),
        ],
        out_specs=pl.BlockSpec((PROJ_TM, PROJ_TN), lambda i, j: (i, j)),
        out_shape=jax.ShapeDtypeStruct((t, IN_COLS), BF16),
        scratch_shapes=[pltpu.VMEM((PROJ_TM, D_MODEL), BF16), pltpu.VMEM((2, PROJ_TM, D_MODEL), F32),
                        pltpu.SemaphoreType.DMA((2,))],
        compiler_params=pltpu.CompilerParams(
            dimension_semantics=("arbitrary", "arbitrary"), vmem_limit_bytes=VMEM_LIMIT),
        name="in_projection",
    )(x2, g1, w_cat, gain_cat, kind_cat)


def _t5_bucket(rel):
    nb = NUM_BUCKETS // 2
    ret = jnp.where(rel > 0, nb, 0)
    n = jnp.abs(rel)
    max_exact = nb // 2
    nf = jnp.maximum(n, 1).astype(jnp.float32)
    large = max_exact + (jnp.log(nf / max_exact) / math.log(MAX_DISTANCE / max_exact)
                         * (nb - max_exact)).astype(jnp.int32)
    large = jnp.minimum(large, nb - 1)
    return ret + jnp.where(n < max_exact, n, large)


def _window_bias(rel_table):
    q_loc = jnp.arange(A_BLOCK)
    k_loc = jnp.arange(3 * A_BLOCK) - WINDOW
    rel = k_loc[None, :] - q_loc[:, None]
    onehot = (_t5_bucket(rel)[..., None] == jnp.arange(NUM_BUCKETS)).astype(F32)
    bias = jnp.einsum('qkn,nh->hqk', onehot, rel_table.astype(F32), precision=lax.Precision.HIGHEST)
    in_window = jnp.abs(rel) <= WINDOW
    has_prev = jnp.array([False, True, True])[:, None, None]
    has_next = jnp.array([True, True, False])[:, None, None]
    k_blk = (jnp.arange(3 * A_BLOCK) // A_BLOCK)[None, None, :]
    valid = in_window[None] & ((k_blk != 0) | has_prev) & ((k_blk != 2) | has_next)
    full = jnp.where(valid[:, None], bias[None], NEG_INF)
    return full.reshape(3, A_KV_HEADS, A_GROUP * A_BLOCK, 3 * A_BLOCK)


def _window_kernel(sink_ref, q_ref, kp_ref, ko_ref, kn_ref, vp_ref, vo_ref, vn_ref,
                   bias_lo_ref, bias_mid_ref, bias_hi_ref, o_ref):
    for kvh in range(A_KV_HEADS):
        hs = slice(kvh * HEAD_DIM, (kvh + 1) * HEAD_DIM)
        heads = [kvh * A_GROUP + g for g in range(A_GROUP)]
        kc = jnp.concatenate([kp_ref[:, hs], ko_ref[:, hs], kn_ref[:, hs]], axis=0)
        vc = jnp.concatenate([vp_ref[:, hs], vo_ref[:, hs], vn_ref[:, hs]], axis=0)
        vc1 = jnp.concatenate([vc, jnp.ones_like(vc)], axis=1)
        sink = jnp.concatenate([jnp.full((A_BLOCK, 1), sink_ref[h], F32) for h in heads], axis=0)
        for jq in range(WIN_STEP):
            rows = slice(jq * A_BLOCK, (jq + 1) * A_BLOCK)
            band = slice(jq * A_BLOCK, (jq + 3) * A_BLOCK)
            bias_ref = bias_lo_ref if jq == 0 else (bias_hi_ref if jq == WIN_STEP - 1 else bias_mid_ref)
            q4 = jnp.concatenate([q_ref[rows, h * HEAD_DIM:(h + 1) * HEAD_DIM] for h in heads], axis=0)
            s = lax.dot_general(q4, kc[band], (((1,), (1,)), ((), ())), preferred_element_type=F32)
            s = s + bias_ref[0, kvh]
            m = jnp.max(s, axis=-1, keepdims=True)
            p = jnp.exp((s - m).astype(BF16))
            ov = jnp.dot(p, vc1[band], preferred_element_type=F32)
            o = ov[:, :HEAD_DIM] / (ov[:, HEAD_DIM:] + jnp.exp(sink - m))
            for g, h in enumerate(heads):
                o_ref[rows, h * HEAD_DIM:(h + 1) * HEAD_DIM] = o[g * A_BLOCK:(g + 1) * A_BLOCK].astype(o_ref.dtype)


def _window_attention(proj, bias_a, sink, batch, seq):
    nb = seq // A_BLOCK
    ns = nb // WIN_STEP
    kcol, vcol = COL_KA // A_KV_WIDTH, COL_VA // A_KV_WIDTH
    prev = lambda b, n: b * nb + jnp.maximum(n * WIN_STEP - 1, 0)
    nxt = lambda b, n: b * nb + jnp.minimum((n + 1) * WIN_STEP, nb - 1)
    own = lambda b, n: b * ns + n
    edge = lambda rfn, col: pl.BlockSpec((A_BLOCK, A_KV_WIDTH), lambda b, n: (rfn(b, n), col))
    main = lambda col: pl.BlockSpec((WIN_STEP * A_BLOCK, A_KV_WIDTH), lambda b, n: (own(b, n), col))
    bias_block = (1, A_KV_HEADS, A_GROUP * A_BLOCK, 3 * A_BLOCK)
    return pl.pallas_call(
        _window_kernel,
        grid=(batch, ns),
        in_specs=[
            pl.BlockSpec(memory_space=pltpu.SMEM),
            pl.BlockSpec((WIN_STEP * A_BLOCK, A_WIDTH), lambda b, n: (own(b, n), COL_QA // A_WIDTH)),
            edge(prev, kcol), main(kcol), edge(nxt, kcol),
            edge(prev, vcol), main(vcol), edge(nxt, vcol),
            pl.BlockSpec(bias_block, lambda b, n: (jnp.where(n == 0, 0, 1), 0, 0, 0)),
            pl.BlockSpec(bias_block, lambda b, n: (1, 0, 0, 0)),
            pl.BlockSpec(bias_block, lambda b, n: (jnp.where(n == ns - 1, 2, 1), 0, 0, 0)),
        ],
        out_specs=pl.BlockSpec((WIN_STEP * A_BLOCK, A_WIDTH), lambda b, n: (own(b, n), 0)),
        out_shape=jax.ShapeDtypeStruct((batch * seq, A_WIDTH), BF16),
        compiler_params=pltpu.CompilerParams(
            dimension_semantics=("parallel", "arbitrary"), vmem_limit_bytes=VMEM_LIMIT),
        name="window_attention",
    )(sink, proj, proj, proj, proj, proj, proj, proj, bias_a, bias_a, bias_a)


def _neighbourhood_col_bias(rpb):
    cols = jnp.arange(GRID_W)
    cs = jnp.clip(cols - NB_COLS // 2, 0, GRID_W - NB_COLS)
    col_ok = (cols[None, :] >= cs[:, None]) & (cols[None, :] < cs[:, None] + NB_COLS)
    dc = cols[None, :] - cols[:, None] + (NB_COLS - 1)
    col_sel = ((dc[..., None] == jnp.arange(2 * NB_COLS - 1)) & col_ok[..., None]).astype(F32)
    by_col = jnp.einsum('hde,cje->hdcj', rpb.astype(F32), col_sel, precision=lax.Precision.HIGHEST)
    return jnp.where(col_ok[None, None], by_col, NEG_INF)


def _neighbourhood_row_plan(cls, rows):
    kh = min(NB_ROWS, rows)
    r0 = (0, NA_ROWS, rows - NA_ROWS)[cls]
    plan = []
    for i in range(NA_ROWS):
        r = r0 + i
        rs = min(max(r - kh // 2, 0), rows - kh)
        plan.append([key_r - r + NB_ROWS - 1 if rs <= key_r < rs + kh else None
                     for key_r in range(r0 - NA_ROWS, r0 + 2 * NA_ROWS)])
    return plan


def _neighbourhood_kernel(q_ref, kp_ref, ko_ref, kn_ref, vp_ref, vo_ref, vn_ref, colbias_ref, o_ref, bias_ref,
                          *, rows):
    n = pl.program_id(1)
    n_blocks = pl.num_programs(1)

    def build_bias(cls):
        plan = _neighbourhood_row_plan(cls, rows)
        masked = jnp.full((GRID_W, GRID_W), NEG_INF, F32)
        for h in range(B_HEADS):
            for i in range(NA_ROWS):
                for pair in range(3 * NA_ROWS // 2):
                    tiles = [masked if plan[i][kr] is None else colbias_ref[h, plan[i][kr]]
                             for kr in (2 * pair, 2 * pair + 1)]
                    bias_ref[h, i * GRID_W:(i + 1) * GRID_W, pair * 2 * GRID_W:(pair + 1) * 2 * GRID_W] = (
                        jnp.concatenate(tiles, axis=1))

    pl.when(n == 0)(functools.partial(build_bias, 0))
    pl.when(n == 1)(functools.partial(build_bias, 1))
    pl.when(n == n_blocks - 1)(functools.partial(build_bias, 2))

    for h in range(B_HEADS):
        hs = slice(h * HEAD_DIM, (h + 1) * HEAD_DIM)
        kc = jnp.concatenate([kp_ref[:, hs], ko_ref[:, hs], kn_ref[:, hs]], axis=0)
        vc = jnp.concatenate([vp_ref[:, hs], vo_ref[:, hs], vn_ref[:, hs]], axis=0)
        s = lax.dot_general(q_ref[:, hs], kc, (((1,), (1,)), ((), ())), preferred_element_type=F32)
        s = s + bias_ref[h]
        m = jnp.max(s, axis=-1, keepdims=True)
        p = jnp.exp((s - m).astype(BF16))
        ov = jnp.dot(p, jnp.concatenate([vc, jnp.ones_like(vc)], axis=1), preferred_element_type=F32)
        o_ref[:, hs] = (ov[:, :HEAD_DIM] / ov[:, HEAD_DIM:]).astype(o_ref.dtype)


def _neighbourhood_attention(proj, col_bias, batch, seq):
    nb = seq // NA_BLOCK
    assert nb >= 2

    def row(b, n):
        return b * nb + n

    prev = lambda b, n: row(b, jnp.maximum(n - 1, 0))
    nxt = lambda b, n: row(b, jnp.minimum(n + 1, nb - 1))
    spec = lambda rfn, col: pl.BlockSpec((NA_BLOCK, B_WIDTH), lambda b, n: (rfn(b, n), col))
    qcol, kcol, vcol = COL_QB // B_WIDTH, COL_KB // B_WIDTH, COL_VB // B_WIDTH
    return pl.pallas_call(
        functools.partial(_neighbourhood_kernel, rows=seq // GRID_W),
        grid=(batch, nb),
        in_specs=[
            spec(row, qcol),
            spec(prev, kcol), spec(row, kcol), spec(nxt, kcol),
            spec(prev, vcol), spec(row, vcol), spec(nxt, vcol),
            pl.BlockSpec(col_bias.shape, lambda b, n: (0, 0, 0, 0), pipeline_mode=pl.Buffered(1)),
        ],
        out_specs=pl.BlockSpec((NA_BLOCK, B_WIDTH), lambda b, n: (row(b, n), 0)),
        out_shape=jax.ShapeDtypeStruct((batch * seq, B_WIDTH), BF16),
        scratch_shapes=[pltpu.VMEM((B_HEADS, NA_BLOCK, 3 * NA_BLOCK), F32)],
        compiler_params=pltpu.CompilerParams(
            dimension_semantics=("arbitrary", "arbitrary"), vmem_limit_bytes=VMEM_LIMIT),
        name="neighbourhood_attention",
    )(proj, proj, proj, proj, proj, proj, proj, col_bias)


def _pack_bf16_pairs(a):
    n = a.shape[1] // 2
    bits = lax.bitcast_convert_type(a, jnp.uint32)
    return bits[:, n:] | (bits[:, :n] >> 16)


def _unpack_bf16_pairs(p):
    lo = lax.bitcast_convert_type(p << 16, F32).astype(BF16)
    hi = lax.bitcast_convert_type(p & jnp.uint32(0xFFFF0000), F32).astype(BF16)
    return lo, hi


def _merge_kernel(oa_ref, ob_ref, ga_ref, gb_ref, x_ref, wa_ref, wb_ref, wo_ref, g2_ref, wr_ref, br_ref,
                  upper_ref, lower_ref, h_ref, xs_ref, eb_ref, wcol_ref, hn_scr):
    s = pl.program_id(0)
    cur = lax.rem(s, 2)
    prev = 1 - cur
    tm = MERGE_TM

    @pl.when(s == 0)
    def _():
        hn_scr[...] = jnp.zeros_like(hn_scr)

    hn_prev = hn_scr[prev]
    logits = _router_logits(hn_prev, wr_ref, br_ref)
    ua = jnp.dot(oa_ref[...], wa_ref[...], preferred_element_type=F32)
    ub = jnp.dot(ob_ref[...], wb_ref[...], preferred_element_type=F32)
    ids, w1, w2 = _route(logits)
    lrow, eb = _local_sort(ids, upper_ref, lower_ref)
    merged = ga_ref[...].astype(F32) * ua + gb_ref[...].astype(F32) * ub
    acc = jnp.dot(merged.astype(BF16), wo_ref[...], preferred_element_type=F32)
    xs = _place(lrow, hn_prev)
    h = x_ref[...] + acc
    h_ref[...] = h
    ms = jnp.mean(h * h, axis=-1, keepdims=True)
    hn_scr[cur] = (h * lax.rsqrt(ms + EPS) * g2_ref[...]).astype(BF16)

    xs_ref[...] = _pack_bf16_pairs(xs)
    eb_ref[0] = eb
    riota = lax.broadcasted_iota(jnp.int32, (ROUTER_LANES, tm), 0)
    wrows = jnp.where(riota == 0, w1, jnp.where(riota == 1, w2,
                      jnp.where(riota == 2, lrow[:, :tm], jnp.where(riota == 3, lrow[:, tm:], 0.0))))
    wcol_ref[...] = wrows.T


def _router_logits(hn, wr_ref, br_ref):
    lg = jnp.dot(hn, wr_ref[...], preferred_element_type=F32)
    return lg + pltpu.roll(lg, ROUTER_LANES // 2, axis=1) + br_ref[...]


def _route(logits):
    tm = logits.shape[0]
    lt = logits.T
    gl = lt[N_EXPERTS:N_EXPERTS + N_GROUPS]
    gmax = jnp.max(gl, axis=0, keepdims=True)
    giota = lax.broadcasted_iota(jnp.int32, gl.shape, 0)
    g_idx = jnp.min(jnp.where(gl == gmax, giota, N_GROUPS), axis=0, keepdims=True)
    p_g = 1.0 / jnp.sum(jnp.exp(gl - gmax), axis=0, keepdims=True)
    e_sel = jnp.zeros((EXPERTS_PER_GROUP, tm), F32)
    for g in range(N_GROUPS):
        e_sel = jnp.where(g_idx == g, lt[g * EXPERTS_PER_GROUP:(g + 1) * EXPERTS_PER_GROUP], e_sel)
    eiota = lax.broadcasted_iota(jnp.int32, e_sel.shape, 0)
    m1 = jnp.max(e_sel, axis=0, keepdims=True)
    i1 = jnp.min(jnp.where(e_sel == m1, eiota, EXPERTS_PER_GROUP), axis=0, keepdims=True)
    rest = jnp.where(eiota == i1, -jnp.inf, e_sel)
    m2 = jnp.max(rest, axis=0, keepdims=True)
    i2 = jnp.min(jnp.where(rest == m2, eiota, EXPERTS_PER_GROUP), axis=0, keepdims=True)
    t = jnp.exp(m2 - m1)
    ids = jnp.concatenate([g_idx * EXPERTS_PER_GROUP + i1, g_idx * EXPERTS_PER_GROUP + i2], axis=1)
    return ids, p_g / (1.0 + t), p_g * t / (1.0 + t)


def _local_sort(ids, upper_ref, lower_ref):
    eiota32 = lax.broadcasted_iota(jnp.int32, (N_EXPERTS, ids.shape[1]), 0)
    onehot = jnp.where(eiota32 == ids, 1.0, 0.0)
    count = jnp.sum(onehot, axis=1, keepdims=True)
    before = jnp.dot(onehot.astype(BF16), upper_ref[...], preferred_element_type=F32)
    nblk = jnp.floor((count + (LOCAL_BLOCK - 1)) * (1.0 / LOCAL_BLOCK))
    boff = jnp.dot(lower_ref[...], jnp.broadcast_to(nblk, (N_EXPERTS, ROUTER_LANES)).astype(BF16),
                   preferred_element_type=F32)[:, 0:1]
    lrow = jnp.sum(onehot * (before + LOCAL_BLOCK * boff), axis=0, keepdims=True)
    biota = lax.broadcasted_iota(jnp.int32, (N_EXPERTS, ROUTER_LANES), 1).astype(F32)
    owns = jnp.where(jnp.logical_and(biota >= boff, biota < boff + nblk), 1.0, 0.0)
    eiota_f = lax.broadcasted_iota(jnp.int32, (N_EXPERTS, ROUTER_LANES), 0).astype(F32)
    eb = jnp.sum(owns * eiota_f, axis=0, keepdims=True) + N_EXPERTS * (1.0 - jnp.sum(owns, axis=0, keepdims=True))
    return lrow, eb.astype(jnp.int32)


def _place(lrow, hn_hi):
    tm = hn_hi.shape[0]
    jiota = lax.broadcasted_iota(jnp.int32, (LOCAL_ROWS, tm), 0).astype(F32)
    place = jnp.where(jiota == lrow[:, :tm], 1.0, jnp.where(jiota == lrow[:, tm:], 1.0, 0.0)).astype(BF16)
    return jnp.dot(place, hn_hi, preferred_element_type=F32)


def _merge_and_route(oa, ob, proj, x2, wa, wb, wo, g2, wr_cat, br):
    t = x2.shape[0]
    n_assign = TOP_K * MERGE_TM
    upper = (jnp.arange(n_assign)[:, None] < jnp.arange(n_assign)[None, :]).astype(BF16)
    lower = (jnp.arange(N_EXPERTS)[:, None] > jnp.arange(N_EXPERTS)[None, :]).astype(BF16)
    const = lambda shape: pl.BlockSpec(shape, lambda s: (0, 0), pipeline_mode=pl.Buffered(1))
    n_tiles = t // MERGE_TM
    cur = lambda s: jnp.minimum(s, n_tiles - 1)
    prev = lambda s: jnp.maximum(s - 1, 0)
    return pl.pallas_call(
        _merge_kernel,
        grid=(n_tiles + 1,),
        in_specs=[
            pl.BlockSpec((MERGE_TM, A_WIDTH), lambda s: (cur(s), 0)),
            pl.BlockSpec((MERGE_TM, B_WIDTH), lambda s: (cur(s), 0)),
            pl.BlockSpec((MERGE_TM, D_MODEL), lambda s: (cur(s), COL_GA // D_MODEL)),
            pl.BlockSpec((MERGE_TM, D_MODEL), lambda s: (cur(s), COL_GB // D_MODEL)),
            pl.BlockSpec((MERGE_TM, D_MODEL), lambda s: (cur(s), 0)),
            const((A_WIDTH, D_MODEL)), const((B_WIDTH, D_MODEL)), const((D_MODEL, D_MODEL)),
            const((1, D_MODEL)), const((D_MODEL, ROUTER_LANES)), const((1, ROUTER_LANES)),
            const((n_assign, n_assign)), const((N_EXPERTS, N_EXPERTS)),
        ],
        out_specs=[
            pl.BlockSpec((MERGE_TM, D_MODEL), lambda s: (cur(s), 0)),
            pl.BlockSpec((LOCAL_ROWS, D_MODEL // 2), lambda s: (prev(s), 0)),
            pl.BlockSpec((1, 1, ROUTER_LANES), lambda s: (prev(s), 0, 0)),
            pl.BlockSpec((MERGE_TM, ROUTER_LANES), lambda s: (prev(s), 0)),
        ],
        scratch_shapes=[pltpu.VMEM((2, MERGE_TM, D_MODEL), BF16)],
        out_shape=[
            jax.ShapeDtypeStruct((t, D_MODEL), F32),
            jax.ShapeDtypeStruct((t // MERGE_TM * LOCAL_ROWS, D_MODEL // 2), jnp.uint32),
            jax.ShapeDtypeStruct((t // MERGE_TM, 1, ROUTER_LANES), jnp.int32),
            jax.ShapeDtypeStruct((t, ROUTER_LANES), F32),
        ],
        compiler_params=pltpu.CompilerParams(
            dimension_semantics=("arbitrary",), vmem_limit_bytes=VMEM_LIMIT),
        name="merge_and_route",
    )(oa, ob, proj, proj, x2, wa, wb, wo, g2, wr_cat, br, upper, lower)


def _expert_kernel(blk_ref, texp_ref, tcb_ref, tnb_ref, wslot_ref, wnext_ref, nused_ref,
                   xs_hbm, wg_hbm, wu_hbm, wd_hbm, y_hbm,
                   xbuf, ybuf, wgf, wuf, wdf, wgb, wub, wdb, gsem, ssem, wsem):
    n_used = nused_ref[0]

    def weight_copies(e, ws):
        return [pltpu.make_async_copy(src.at[e], dst.at[ws], wsem.at[ws])
                for src, dst in ((wg_hbm, wgf), (wu_hbm, wuf), (wd_hbm, wdf))]

    def gather_copy(blk, dst_slot, j):
        return pltpu.make_async_copy(xs_hbm.at[blk], xbuf.at[dst_slot, pl.ds(j * LOCAL_BLOCK, LOCAL_BLOCK)],
                                     gsem.at[dst_slot])

    def scatter_copy(blk, src_slot, j):
        return pltpu.make_async_copy(ybuf.at[src_slot, pl.ds(j * LOCAL_BLOCK, LOCAL_BLOCK)], y_hbm.at[blk],
                                     ssem.at[src_slot])

    def issue_gather(tile, dst_slot):
        @pl.when(texp_ref[tile] < N_EXPERTS)
        def _():
            base = tcb_ref[tile]
            for j in range(MOE_BLOCKS):
                gather_copy(blk_ref[base + j], dst_slot, j).start()

    def wait_gather(tile, s):
        @pl.when(texp_ref[tile] < N_EXPERTS)
        def _():
            for j in range(MOE_BLOCKS):
                gather_copy(0, s, j).wait()

    def for_valid_blocks(tile, fn):
        n_valid = tnb_ref[tile]

        @pl.when(n_valid == MOE_BLOCKS)
        def _():
            for j in range(MOE_BLOCKS):
                fn(j)

        @pl.when(n_valid < MOE_BLOCKS)
        def _():
            for j in range(MOE_BLOCKS):
                pl.when(j < n_valid)(functools.partial(fn, j))

    def issue_scatter(tile, src_slot):
        base = tcb_ref[tile]
        for_valid_blocks(tile, lambda j: scatter_copy(blk_ref[base + j], src_slot, j).start())

    def wait_scatter(tile, s):
        for_valid_blocks(tile, lambda j: scatter_copy(0, s, j).wait())

    @pl.when(n_used > 0)
    def _():
        issue_gather(0, 0)

        @pl.when(texp_ref[0] < N_EXPERTS)
        def _():
            for cp in weight_copies(texp_ref[0], wslot_ref[0]):
                cp.start()

    def tile_step(i, carry):
        slot = lax.rem(i, 2)
        expert = texp_ref[i]
        wait_gather(i, slot)

        @pl.when(i + 1 < n_used)
        def _():
            issue_gather(i + 1, 1 - slot)

        @pl.when(i >= 2)
        def _():
            wait_scatter(i - 2, slot)

        @pl.when(jnp.logical_and(expert < N_EXPERTS, jnp.logical_or(i == 0, expert != texp_ref[jnp.maximum(i - 1, 0)])))
        def _():
            ws = wslot_ref[i]
            for cp in weight_copies(expert, ws):
                cp.wait()
            for static_ws in (0, 1):
                @pl.when(ws == static_ws)
                def _():
                    wgb[...] = wgf[static_ws].astype(BF16)
                    wub[...] = wuf[static_ws].astype(BF16)
                    wdb[...] = wdf[static_ws].astype(BF16)
            nxt = wnext_ref[i]

            @pl.when(nxt < N_EXPERTS)
            def _():
                for cp in weight_copies(nxt, 1 - ws):
                    cp.start()

        @pl.when(expert < N_EXPERTS)
        def _():
            x_lo, x_hi = _unpack_bf16_pairs(xbuf[slot])
            half = D_MODEL // 2
            g = (jnp.dot(x_lo, wgb[:half, :], preferred_element_type=F32)
                 + jnp.dot(x_hi, wgb[half:, :], preferred_element_type=F32))
            u = (jnp.dot(x_lo, wub[:half, :], preferred_element_type=F32)
                 + jnp.dot(x_hi, wub[half:, :], preferred_element_type=F32))
            act = (g * jax.nn.sigmoid(g) * u).astype(BF16)
            y = jnp.dot(act, wdb[...], preferred_element_type=F32)
            ybuf[slot] = _pack_bf16_pairs(y.astype(BF16).astype(F32))

        @pl.when(expert >= N_EXPERTS)
        def _():
            ybuf[slot] = jnp.zeros((MOE_TM, D_MODEL // 2), jnp.uint32)

        issue_scatter(i, slot)
        return carry

    lax.fori_loop(0, n_used, tile_step, 0)

    for back in (1, 2):
        @pl.when(n_used >= back)
        def _():
            last = jnp.maximum(n_used - back, 0)
            wait_scatter(last, lax.rem(last, 2))


def _expert_ffn(plan, xs_local, wg, wu, wd):
    n_blocks = xs_local.shape[0] // LOCAL_BLOCK
    half = D_MODEL // 2
    any_spec = pl.BlockSpec(memory_space=pl.ANY)
    grid_spec = pltpu.PrefetchScalarGridSpec(
        num_scalar_prefetch=len(plan),
        grid=(1,),
        in_specs=[any_spec, any_spec, any_spec, any_spec],
        out_specs=any_spec,
        scratch_shapes=[
            pltpu.VMEM((2, MOE_TM, half), jnp.uint32), pltpu.VMEM((2, MOE_TM, half), jnp.uint32),
            pltpu.VMEM((2, D_MODEL, D_EXPERT), F32), pltpu.VMEM((2, D_MODEL, D_EXPERT), F32),
            pltpu.VMEM((2, D_EXPERT, D_MODEL), F32),
            pltpu.VMEM((D_MODEL, D_EXPERT), BF16), pltpu.VMEM((D_MODEL, D_EXPERT), BF16),
            pltpu.VMEM((D_EXPERT, D_MODEL), BF16),
            pltpu.SemaphoreType.DMA((2,)), pltpu.SemaphoreType.DMA((2,)), pltpu.SemaphoreType.DMA((2,)),
        ],
    )
    return pl.pallas_call(
        _expert_kernel,
        grid_spec=grid_spec,
        out_shape=jax.ShapeDtypeStruct((n_blocks, LOCAL_BLOCK, half), jnp.uint32),
        compiler_params=pltpu.CompilerParams(
            dimension_semantics=("arbitrary",), vmem_limit_bytes=VMEM_LIMIT),
        name="expert_ffn",
    )(*plan, xs_local.reshape(n_blocks, LOCAL_BLOCK, half), wg, wu, wd)


def _combine_kernel(h_ref, wcol_ref, y_ref, o_ref):
    y_lo, y_hi = _unpack_bf16_pairs(y_ref[...])
    w = wcol_ref[...]
    jiota = lax.broadcasted_iota(jnp.int32, (MERGE_TM, LOCAL_ROWS), 1).astype(F32)
    pick = (jnp.where(jiota == w[:, 2:3], w[:, 0:1], 0.0) + jnp.where(jiota == w[:, 3:4], w[:, 1:2], 0.0)).astype(BF16)
    half = D_MODEL // 2
    for cols, y in ((slice(0, half), y_lo), (slice(half, D_MODEL), y_hi)):
        o_ref[:, cols] = h_ref[:, cols] + jnp.dot(pick, y, preferred_element_type=F32)


def _combine(h, wcol, y_local):
    t = h.shape[0]
    return pl.pallas_call(
        _combine_kernel,
        grid=(t // MERGE_TM,),
        in_specs=[
            pl.BlockSpec((MERGE_TM, D_MODEL), lambda i: (i, 0)),
            pl.BlockSpec((MERGE_TM, ROUTER_LANES), lambda i: (i, 0)),
            pl.BlockSpec((LOCAL_ROWS, D_MODEL // 2), lambda i: (i, 0)),
        ],
        out_specs=pl.BlockSpec((MERGE_TM, D_MODEL), lambda i: (i, 0)),
        out_shape=jax.ShapeDtypeStruct((t, D_MODEL), F32),
        compiler_params=pltpu.CompilerParams(
            dimension_semantics=("parallel",), vmem_limit_bytes=VMEM_LIMIT),
        name="combine",
    )(h, wcol, y_local)


def _dispatch_plan(block_expert):
    n_blocks = block_expert.size
    n_keys = N_EXPERTS + 1
    n_tiles = n_blocks // MOE_BLOCKS + n_keys
    keys = block_expert.reshape(n_blocks)
    order = jnp.argsort(keys, stable=True).astype(jnp.int32)
    counts = jnp.sum(keys[None, :] == jnp.arange(n_keys, dtype=jnp.int32)[:, None], axis=1).astype(jnp.int32)
    tiles_per = (counts + MOE_BLOCKS - 1) // MOE_BLOCKS
    tile_end = jnp.cumsum(tiles_per)
    tile_first = tile_end - tiles_per
    start = jnp.cumsum(counts) - counts
    tile = jnp.arange(n_tiles, dtype=jnp.int32)
    tile_expert = jnp.minimum(jnp.sum(tile_end[None, :] <= tile[:, None], axis=1), n_keys - 1).astype(jnp.int32)
    sel = (tile_expert[:, None] == jnp.arange(n_keys, dtype=jnp.int32)[None, :]).astype(jnp.int32)
    pick = lambda v: jnp.sum(sel * v[None, :], axis=1)
    within = (tile - pick(tile_first)) * MOE_BLOCKS
    tile_base = jnp.clip(pick(start) + within, 0, n_blocks).astype(jnp.int32)
    tile_blocks = jnp.clip(pick(counts) - within, 0, MOE_BLOCKS).astype(jnp.int32)
    n_used = tile_end[-1].astype(jnp.int32).reshape(1)
    order = jnp.concatenate([order, jnp.zeros((MOE_BLOCKS,), jnp.int32)])
    key = jnp.arange(n_keys, dtype=jnp.int32)
    present = jnp.logical_and(counts > 0, key < N_EXPERTS)
    slot_of = (jnp.cumsum(present) - present) % 2
    later = jnp.logical_and(key[None, :] > key[:, None], present[None, :])
    next_of = jnp.min(jnp.where(later, key[None, :], N_EXPERTS), axis=1)
    return (order, tile_expert, tile_base, tile_blocks, pick(slot_of).astype(jnp.int32),
            pick(next_of).astype(jnp.int32), n_used)


def _layer(x, norm1_g, w_in, q_norm_a, k_norm_a, sink_a, q_norm_b, k_norm_b, rpb_b, w_up_a, w_up_b, w_out,
           rel_table, norm2_g, w_rg, b_rg, w_re, b_re, w_gate, w_up, w_down):
    batch, seq, d = x.shape
    t = batch * seq
    x2 = x.reshape(t, d)

    kv_a = slice(A_WIDTH, A_WIDTH + 2 * A_KV_WIDTH)
    w_cat = jnp.concatenate([w_in[:, :kv_a.start], w_in[:, kv_a.stop:], w_in[:, kv_a]], axis=1).astype(BF16)
    scale = HEAD_DIM ** -0.5
    ones = lambda n: jnp.ones((n,), F32)
    gain_cat = jnp.concatenate([
        jnp.tile(q_norm_a.astype(F32) * scale, A_HEADS), jnp.tile(q_norm_b.astype(F32) * scale, B_HEADS),
        jnp.tile(k_norm_b.astype(F32), B_HEADS), ones(B_WIDTH + 2 * D_MODEL),
        jnp.tile(k_norm_a.astype(F32), A_KV_HEADS), ones(A_KV_WIDTH)]).reshape(1, IN_COLS)

    kind = lambda v, n: jnp.full((n,), v, F32)
    kind_cat = jnp.concatenate([kind(1.0, COL_VB), kind(0.0, B_WIDTH), kind(2.0, 2 * D_MODEL),
                                kind(1.0, A_KV_WIDTH), kind(0.0, A_KV_WIDTH)]).reshape(1, IN_COLS)

    proj = _in_projection(x2, norm1_g.astype(F32).reshape(1, d), w_cat, gain_cat, kind_cat)
    oa = _window_attention(proj, _window_bias(rel_table), sink_a.astype(F32), batch, seq)
    ob = _neighbourhood_attention(proj, _neighbourhood_col_bias(rpb_b), batch, seq)

    half = ROUTER_LANES // 2
    wr_t = jnp.concatenate([w_re.astype(F32).T, w_rg.astype(F32).T,
                            jnp.zeros((half - N_EXPERTS - N_GROUPS, d), F32)], axis=0)
    wr_hi = wr_t.astype(BF16)
    wr_lo = (wr_t - wr_hi.astype(F32)).astype(BF16)
    wr_cat = jnp.concatenate([wr_hi, wr_lo], axis=0).T
    br = jnp.pad(jnp.concatenate([b_re.astype(F32), b_rg.astype(F32)]), (0, ROUTER_LANES - N_EXPERTS - N_GROUPS))

    h, xs_local, block_expert, wcol = _merge_and_route(
        oa, ob, proj, x2, w_up_a.astype(BF16), w_up_b.astype(BF16), w_out.astype(BF16),
        norm2_g.astype(F32).reshape(1, d), wr_cat, br.reshape(1, ROUTER_LANES))

    plan = _dispatch_plan(block_expert[:, 0, :LOCAL_BLOCKS])
    y_local = _expert_ffn(plan, xs_local, w_gate.astype(F32), w_up.astype(F32), w_down.astype(F32))
    out = _combine(h, wcol, y_local.reshape(-1, d // 2))
    return out.reshape(batch, seq, d)


def kernel(x, norm1_g, w_in, q_norm_a, k_norm_a, sink_a, q_norm_b, k_norm_b, rpb_b, w_up_a, w_up_b, w_out,
           rel_bias_table, norm2_g, w_router_group, b_router_group, w_router_expert, b_router_expert,
           w_gate, w_up, w_down):
    return _layer(x, norm1_g[0], w_in[0], q_norm_a[0], k_norm_a[0], sink_a[0], q_norm_b[0], k_norm_b[0],
                  rpb_b[0], w_up_a[0], w_up_b[0], w_out[0], rel_bias_table, norm2_g[0],
                  w_router_group[0], b_router_group[0], w_router_expert[0], b_router_expert[0],
                  w_gate[0], w_up[0], w_down[0])
```

```python
import functools
import math

import jax
import jax.numpy as jnp
from jax import lax
from jax.experimental import pallas as pl
from jax.experimental.pallas import tpu as pltpu

F32 = jnp.float32
BF16 = jnp.bfloat16

D_MODEL = 2048
HEAD_DIM = 128
A_HEADS = 8
A_KV_HEADS = 2
A_GROUP = A_HEADS // A_KV_HEADS
A_WIDTH = A_HEADS * HEAD_DIM
A_KV_WIDTH = A_KV_HEADS * HEAD_DIM
WINDOW = 128
A_BLOCK = 128
NUM_BUCKETS = 32
MAX_DISTANCE = 128
B_HEADS = 8
B_WIDTH = B_HEADS * HEAD_DIM
GRID_W = 64
NB_ROWS = 8
NB_COLS = 16
IN_COLS = A_WIDTH + 2 * A_KV_WIDTH + 3 * B_WIDTH + 2 * D_MODEL
N_GROUPS = 4
EXPERTS_PER_GROUP = 8
N_EXPERTS = N_GROUPS * EXPERTS_PER_GROUP
TOP_K = 2
D_EXPERT = 512
EPS = 1e-6
NEG_INF = -1e30

COL_QA = 0
COL_QB = COL_QA + A_WIDTH
COL_KB = COL_QB + B_WIDTH
COL_VB = COL_KB + B_WIDTH
COL_GA = COL_VB + B_WIDTH
COL_GB = COL_GA + D_MODEL
COL_KA = COL_GB + D_MODEL
COL_VA = COL_KA + A_KV_WIDTH

PROJ_TM = 2048
PROJ_TN = 512
PROJ_SUB = 128
WIN_STEP = 8
NA_ROWS = 4
NA_BLOCK = NA_ROWS * GRID_W
MERGE_TM = 256
ROUTER_LANES = 128
LOCAL_BLOCK = 8
LOCAL_ROWS = 768
LOCAL_BLOCKS = LOCAL_ROWS // LOCAL_BLOCK
assert LOCAL_ROWS >= TOP_K * MERGE_TM + N_EXPERTS * (LOCAL_BLOCK - 1) and LOCAL_BLOCKS <= ROUTER_LANES
MOE_TM = 256
MOE_BLOCKS = MOE_TM // LOCAL_BLOCK
VMEM_LIMIT = 56 * 1024 * 1024


def _head_norm(a, gain):
    outs = []
    for hd in range(a.shape[1] // HEAD_DIM):
        sl = slice(hd * HEAD_DIM, (hd + 1) * HEAD_DIM)
        ah = a[:, sl]
        ss = jnp.sum(ah * ah, axis=-1, keepdims=True)
        outs.append(ah * lax.rsqrt(ss * (1.0 / HEAD_DIM) + EPS) * gain[:, sl])
    return outs[0] if len(outs) == 1 else jnp.concatenate(outs, axis=1)


def _inproj_kernel(x_hbm, g1_ref, w_ref, gain_ref, kind_ref, o_ref, xn_ref, xbuf, xsem):
    i = pl.program_id(0)
    j = pl.program_id(1)

    def x_copy(tile, slot):
        return pltpu.make_async_copy(x_hbm.at[pl.ds(tile * PROJ_TM, PROJ_TM)], xbuf.at[slot], xsem.at[slot])

    @pl.when(j == 0)
    def _():
        @pl.when(i == 0)
        def _():
            x_copy(0, 0).start()

        x_copy(i, lax.rem(i, 2)).wait()

        @pl.when(i + 1 < pl.num_programs(0))
        def _():
            x_copy(i + 1, lax.rem(i + 1, 2)).start()

    def project(x_slot):
        kind = kind_ref[...]
        for r in range(PROJ_TM // PROJ_SUB):
            rows = slice(r * PROJ_SUB, (r + 1) * PROJ_SUB)
            if x_slot is not None:
                x = xbuf[x_slot, rows, :]
                ms = jnp.mean(x * x, axis=-1, keepdims=True)
                xn_ref[rows, :] = (x * lax.rsqrt(ms + EPS) * g1_ref[...]).astype(BF16)
            acc = jnp.dot(xn_ref[rows, :], w_ref[...], preferred_element_type=F32)
            out = jnp.where(kind == 1.0, _head_norm(acc, gain_ref[...]),
                            jnp.where(kind == 2.0, jax.nn.sigmoid(acc), acc))
            o_ref[rows, :] = out.astype(o_ref.dtype)

    for x_slot in (0, 1):
        pl.when(jnp.logical_and(j == 0, lax.rem(i, 2) == x_slot))(functools.partial(project, x_slot))
    pl.when(j != 0)(functools.partial(project, None))


def _in_projection(x2, g1, w_cat, gain_cat, kind_cat):
    t = x2.shape[0]
    return pl.pallas_call(
        _inproj_kernel,
        grid=(t // PROJ_TM, IN_COLS // PROJ_TN),
        in_specs=[
            pl.BlockSpec(memory_space=pl.ANY),
            pl.BlockSpec((1, D_MODEL), lambda i, j: (0, 0)),
            pl.BlockSpec((D_MODEL, PROJ_TN), lambda i, j: (0, j)),
            pl.BlockSpec((1, PROJ_TN), lambda i, j: (0, j)),
            pl.BlockSpec((1, PROJ_TN), lambda i, j: (0, j)),
        ],
        out_specs=pl.BlockSpec((PROJ_TM, PROJ_TN), lambda i, j: (i, j)),
        out_shape=jax.ShapeDtypeStruct((t, IN_COLS), BF16),
        scratch_shapes=[pltpu.VMEM((PROJ_TM, D_MODEL), BF16), pltpu.VMEM((2, PROJ_TM, D_MODEL), F32),
                        pltpu.SemaphoreType.DMA((2,))],
        compiler_params=pltpu.CompilerParams(
            dimension_semantics=("arbitrary", "arbitrary"), vmem_limit_bytes=VMEM_LIMIT),
        name="in_projection",
    )(x2, g1, w_cat, gain_cat, kind_cat)


def _t5_bucket(rel):
    nb = NUM_BUCKETS // 2
    ret = jnp.where(rel > 0, nb, 0)
    n = jnp.abs(rel)
    max_exact = nb // 2
    nf = jnp.maximum(n, 1).astype(jnp.float32)
    large = max_exact + (jnp.log(nf / max_exact) / math.log(MAX_DISTANCE / max_exact)
                         * (nb - max_exact)).astype(jnp.int32)
    large = jnp.minimum(large, nb - 1)
    return ret + jnp.where(n < max_exact, n, large)


def _window_bias(rel_table):
    q_loc = jnp.arange(A_BLOCK)
    k_loc = jnp.arange(3 * A_BLOCK) - WINDOW
    rel = k_loc[None, :] - q_loc[:, None]
    onehot = (_t5_bucket(rel)[..., None] == jnp.arange(NUM_BUCKETS)).astype(F32)
    bias = jnp.einsum('qkn,nh->hqk', onehot, rel_table.astype(F32), precision=lax.Precision.HIGHEST)
    in_window = jnp.abs(rel) <= WINDOW
    has_prev = jnp.array([False, True, True])[:, None, None]
    has_next = jnp.array([True, True, False])[:, None, None]
    k_blk = (jnp.arange(3 * A_BLOCK) // A_BLOCK)[None, None, :]
    valid = in_window[None] & ((k_blk != 0) | has_prev) & ((k_blk != 2) | has_next)
    full = jnp.where(valid[:, None], bias[None], NEG_INF)
    return full.reshape(3, A_KV_HEADS, A_GROUP * A_BLOCK, 3 * A_BLOCK)


def _window_kernel(sink_ref, q_ref, kp_ref, ko_ref, kn_ref, vp_ref, vo_ref, vn_ref,
                   bias_lo_ref, bias_mid_ref, bias_hi_ref, o_ref):
    for kvh in range(A_KV_HEADS):
        hs = slice(kvh * HEAD_DIM, (kvh + 1) * HEAD_DIM)
        heads = [kvh * A_GROUP + g for g in range(A_GROUP)]
        kc = jnp.concatenate([kp_ref[:, hs], ko_ref[:, hs], kn_ref[:, hs]], axis=0)
        vc = jnp.concatenate([vp_ref[:, hs], vo_ref[:, hs], vn_ref[:, hs]], axis=0)
        vc1 = jnp.concatenate([vc, jnp.ones_like(vc)], axis=1)
        sink = jnp.concatenate([jnp.full((A_BLOCK, 1), sink_ref[h], F32) for h in heads], axis=0)
        for jq in range(WIN_STEP):
            rows = slice(jq * A_BLOCK, (jq + 1) * A_BLOCK)
            band = slice(jq * A_BLOCK, (jq + 3) * A_BLOCK)
            bias_ref = bias_lo_ref if jq == 0 else (bias_hi_ref if jq == WIN_STEP - 1 else bias_mid_ref)
            q4 = jnp.concatenate([q_ref[rows, h * HEAD_DIM:(h + 1) * HEAD_DIM] for h in heads], axis=0)
            s = lax.dot_general(q4, kc[band], (((1,), (1,)), ((), ())), preferred_element_type=F32)
            s = s + bias_ref[0, kvh]
            m = jnp.max(s, axis=-1, keepdims=True)
            p = jnp.exp((s - m).astype(BF16))
            ov = jnp.dot(p, vc1[band], preferred_element_type=F32)
            o = ov[:, :HEAD_DIM] / (ov[:, HEAD_DIM:] + jnp.exp(sink - m))
            for g, h in enumerate(heads):
                o_ref[rows, h * HEAD_DIM:(h + 1) * HEAD_DIM] = o[g * A_BLOCK:(g + 1) * A_BLOCK].astype(o_ref.dtype)


def _window_attention(proj, bias_a, sink, batch, seq):
    nb = seq // A_BLOCK
    ns = nb // WIN_STEP
    kcol, vcol = COL_KA // A_KV_WIDTH, COL_VA // A_KV_WIDTH
    prev = lambda b, n: b * nb + jnp.maximum(n * WIN_STEP - 1, 0)
    nxt = lambda b, n: b * nb + jnp.minimum((n + 1) * WIN_STEP, nb - 1)
    own = lambda b, n: b * ns + n
    edge = lambda rfn, col: pl.BlockSpec((A_BLOCK, A_KV_WIDTH), lambda b, n: (rfn(b, n), col))
    main = lambda col: pl.BlockSpec((WIN_STEP * A_BLOCK, A_KV_WIDTH), lambda b, n: (own(b, n), col))
    bias_block = (1, A_KV_HEADS, A_GROUP * A_BLOCK, 3 * A_BLOCK)
    return pl.pallas_call(
        _window_kernel,
        grid=(batch, ns),
        in_specs=[
            pl.BlockSpec(memory_space=pltpu.SMEM),
            pl.BlockSpec((WIN_STEP * A_BLOCK, A_WIDTH), lambda b, n: (own(b, n), COL_QA // A_WIDTH)),
            edge(prev, kcol), main(kcol), edge(nxt, kcol),
            edge(prev, vcol), main(vcol), edge(nxt, vcol),
            pl.BlockSpec(bias_block, lambda b, n: (jnp.where(n == 0, 0, 1), 0, 0, 0)),
            pl.BlockSpec(bias_block, lambda b, n: (1, 0, 0, 0)),
            pl.BlockSpec(bias_block, lambda b, n: (jnp.where(n == ns - 1, 2, 1), 0, 0, 0)),
        ],
        out_specs=pl.BlockSpec((WIN_STEP * A_BLOCK, A_WIDTH), lambda b, n: (own(b, n), 0)),
        out_shape=jax.ShapeDtypeStruct((batch * seq, A_WIDTH), BF16),
        compiler_params=pltpu.CompilerParams(
            dimension_semantics=("parallel", "arbitrary"), vmem_limit_bytes=VMEM_LIMIT),
        name="window_attention",
    )(sink, proj, proj, proj, proj, proj, proj, proj, bias_a, bias_a, bias_a)


def _neighbourhood_col_bias(rpb):
    cols = jnp.arange(GRID_W)
    cs = jnp.clip(cols - NB_COLS // 2, 0, GRID_W - NB_COLS)
    col_ok = (cols[None, :] >= cs[:, None]) & (cols[None, :] < cs[:, None] + NB_COLS)
    dc = cols[None, :] - cols[:, None] + (NB_COLS - 1)
    col_sel = ((dc[..., None] == jnp.arange(2 * NB_COLS - 1)) & col_ok[..., None]).astype(F32)
    by_col = jnp.einsum('hde,cje->hdcj', rpb.astype(F32), col_sel, precision=lax.Precision.HIGHEST)
    return jnp.where(col_ok[None, None], by_col, NEG_INF)


def _neighbourhood_row_plan(cls, rows):
    kh = min(NB_ROWS, rows)
    r0 = (0, NA_ROWS, rows - NA_ROWS)[cls]
    plan = []
    for i in range(NA_ROWS):
        r = r0 + i
        rs = min(max(r - kh // 2, 0), rows - kh)
        plan.append([key_r - r + NB_ROWS - 1 if rs <= key_r < rs + kh else None
                     for key_r in range(r0 - NA_ROWS, r0 + 2 * NA_ROWS)])
    return plan


def _neighbourhood_kernel(q_ref, kp_ref, ko_ref, kn_ref, vp_ref, vo_ref, vn_ref, colbias_ref, o_ref, bias_ref,
                          *, rows):
    n = pl.program_id(1)
    n_blocks = pl.num_programs(1)

    def build_bias(cls):
        plan = _neighbourhood_row_plan(cls, rows)
        masked = jnp.full((GRID_W, GRID_W), NEG_INF, F32)
        for h in range(B_HEADS):
            for i in range(NA_ROWS):
                for pair in range(3 * NA_ROWS // 2):
                    tiles = [masked if plan[i][kr] is None else colbias_ref[h, plan[i][kr]]
                             for kr in (2 * pair, 2 * pair + 1)]
                    bias_ref[h, i * GRID_W:(i + 1) * GRID_W, pair * 2 * GRID_W:(pair + 1) * 2 * GRID_W] = (
                        jnp.concatenate(tiles, axis=1))

    pl.when(n == 0)(functools.partial(build_bias, 0))
    pl.when(n == 1)(functools.partial(build_bias, 1))
    pl.when(n == n_blocks - 1)(functools.partial(build_bias, 2))

    for h in range(B_HEADS):
        hs = slice(h * HEAD_DIM, (h + 1) * HEAD_DIM)
        kc = jnp.concatenate([kp_ref[:, hs], ko_ref[:, hs], kn_ref[:, hs]], axis=0)
        vc = jnp.concatenate([vp_ref[:, hs], vo_ref[:, hs], vn_ref[:, hs]], axis=0)
        s = lax.dot_general(q_ref[:, hs], kc, (((1,), (1,)), ((), ())), preferred_element_type=F32)
        s = s + bias_ref[h]
        m = jnp.max(s, axis=-1, keepdims=True)
        p = jnp.exp((s - m).astype(BF16))
        ov = jnp.dot(p, jnp.concatenate([vc, jnp.ones_like(vc)], axis=1), preferred_element_type=F32)
        o_ref[:, hs] = (ov[:, :HEAD_DIM] / ov[:, HEAD_DIM:]).astype(o_ref.dtype)


def _neighbourhood_attention(proj, col_bias, batch, seq):
    nb = seq // NA_BLOCK
    assert nb >= 2

    def row(b, n):
        return b * nb + n

    prev = lambda b, n: row(b, jnp.maximum(n - 1, 0))
    nxt = lambda b, n: row(b, jnp.minimum(n + 1, nb - 1))
    spec = lambda rfn, col: pl.BlockSpec((NA_BLOCK, B_WIDTH), lambda b, n: (rfn(b, n), col))
    qcol, kcol, vcol = COL_QB // B_WIDTH, COL_KB // B_WIDTH, COL_VB // B_WIDTH
    return pl.pallas_call(
        functools.partial(_neighbourhood_kernel, rows=seq // GRID_W),
        grid=(batch, nb),
        in_specs=[
            spec(row, qcol),
            spec(prev, kcol), spec(row, kcol), spec(nxt, kcol),
            spec(prev, vcol), spec(row, vcol), spec(nxt, vcol),
            pl.BlockSpec(col_bias.shape, lambda b, n: (0, 0, 0, 0), pipeline_mode=pl.Buffered(1)),
        ],
        out_specs=pl.BlockSpec((NA_BLOCK, B_WIDTH), lambda b, n: (row(b, n), 0)),
        out_shape=jax.ShapeDtypeStruct((batch * seq, B_WIDTH), BF16),
        scratch_shapes=[pltpu.VMEM((B_HEADS, NA_BLOCK, 3 * NA_BLOCK), F32)],
        compiler_params=pltpu.CompilerParams(
            dimension_semantics=("arbitrary", "arbitrary"), vmem_limit_bytes=VMEM_LIMIT),
        name="neighbourhood_attention",
    )(proj, proj, proj, proj, proj, proj, proj, col_bias)


def _pack_bf16_pairs(a):
    n = a.shape[1] // 2
    bits = lax.bitcast_convert_type(a, jnp.uint32)
    return bits[:, n:] | (bits[:, :n] >> 16)


def _unpack_bf16_pairs(p):
    lo = lax.bitcast_convert_type(p << 16, F32).astype(BF16)
    hi = lax.bitcast_convert_type(p & jnp.uint32(0xFFFF0000), F32).astype(BF16)
    return lo, hi


def _merge_kernel(oa_ref, ob_ref, ga_ref, gb_ref, x_ref, wa_ref, wb_ref, wo_ref, g2_ref, wr_ref, br_ref,
                  upper_ref, lower_ref, h_ref, xs_ref, eb_ref, wcol_ref, hn_scr):
    s = pl.program_id(0)
    cur = lax.rem(s, 2)
    prev = 1 - cur
    tm = MERGE_TM

    @pl.when(s == 0)
    def _():
        hn_scr[...] = jnp.zeros_like(hn_scr)

    hn_prev = hn_scr[prev]
    logits = _router_logits(hn_prev, wr_ref, br_ref)
    ua = jnp.dot(oa_ref[...], wa_ref[...], preferred_element_type=F32)
    ub = jnp.dot(ob_ref[...], wb_ref[...], preferred_element_type=F32)
    ids, w1, w2 = _route(logits)
    lrow, eb = _local_sort(ids, upper_ref, lower_ref)
    merged = ga_ref[...].astype(F32) * ua + gb_ref[...].astype(F32) * ub
    acc = jnp.dot(merged.astype(BF16), wo_ref[...], preferred_element_type=F32)
    xs = _place(lrow, hn_prev)
    h = x_ref[...] + acc
    h_ref[...] = h
    ms = jnp.mean(h * h, axis=-1, keepdims=True)
    hn_scr[cur] = (h * lax.rsqrt(ms + EPS) * g2_ref[...]).astype(BF16)

    xs_ref[...] = _pack_bf16_pairs(xs)
    eb_ref[0] = eb
    riota = lax.broadcasted_iota(jnp.int32, (ROUTER_LANES, tm), 0)
    wrows = jnp.where(riota == 0, w1, jnp.where(riota == 1, w2,
                      jnp.where(riota == 2, lrow[:, :tm], jnp.where(riota == 3, lrow[:, tm:], 0.0))))
    wcol_ref[...] = wrows.T


def _router_logits(hn, wr_ref, br_ref):
    lg = jnp.dot(hn, wr_ref[...], preferred_element_type=F32)
    return lg + pltpu.roll(lg, ROUTER_LANES // 2, axis=1) + br_ref[...]


def _route(logits):
    tm = logits.shape[0]
    lt = logits.T
    gl = lt[N_EXPERTS:N_EXPERTS + N_GROUPS]
    gmax = jnp.max(gl, axis=0, keepdims=True)
    giota = lax.broadcasted_iota(jnp.int32, gl.shape, 0)
    g_idx = jnp.min(jnp.where(gl == gmax, giota, N_GROUPS), axis=0, keepdims=True)
    p_g = 1.0 / jnp.sum(jnp.exp(gl - gmax), axis=0, keepdims=True)
    e_sel = jnp.zeros((EXPERTS_PER_GROUP, tm), F32)
    for g in range(N_GROUPS):
        e_sel = jnp.where(g_idx == g, lt[g * EXPERTS_PER_GROUP:(g + 1) * EXPERTS_PER_GROUP], e_sel)
    eiota = lax.broadcasted_iota(jnp.int32, e_sel.shape, 0)
    m1 = jnp.max(e_sel, axis=0, keepdims=True)
    i1 = jnp.min(jnp.where(e_sel == m1, eiota, EXPERTS_PER_GROUP), axis=0, keepdims=True)
    rest = jnp.where(eiota == i1, -jnp.inf, e_sel)
    m2 = jnp.max(rest, axis=0, keepdims=True)
    i2 = jnp.min(jnp.where(rest == m2, eiota, EXPERTS_PER_GROUP), axis=0, keepdims=True)
    t = jnp.exp(m2 - m1)
    ids = jnp.concatenate([g_idx * EXPERTS_PER_GROUP + i1, g_idx * EXPERTS_PER_GROUP + i2], axis=1)
    return ids, p_g / (1.0 + t), p_g * t / (1.0 + t)


def _local_sort(ids, upper_ref, lower_ref):
    eiota32 = lax.broadcasted_iota(jnp.int32, (N_EXPERTS, ids.shape[1]), 0)
    onehot = jnp.where(eiota32 == ids, 1.0, 0.0)
    count = jnp.sum(onehot, axis=1, keepdims=True)
    before = jnp.dot(onehot.astype(BF16), upper_ref[...], preferred_element_type=F32)
    nblk = jnp.floor((count + (LOCAL_BLOCK - 1)) * (1.0 / LOCAL_BLOCK))
    boff = jnp.dot(lower_ref[...], jnp.broadcast_to(nblk, (N_EXPERTS, ROUTER_LANES)).astype(BF16),
                   preferred_element_type=F32)[:, 0:1]
    lrow = jnp.sum(onehot * (before + LOCAL_BLOCK * boff), axis=0, keepdims=True)
    biota = lax.broadcasted_iota(jnp.int32, (N_EXPERTS, ROUTER_LANES), 1).astype(F32)
    owns = jnp.where(jnp.logical_and(biota >= boff, biota < boff + nblk), 1.0, 0.0)
    eiota_f = lax.broadcasted_iota(jnp.int32, (N_EXPERTS, ROUTER_LANES), 0).astype(F32)
    eb = jnp.sum(owns * eiota_f, axis=0, keepdims=True) + N_EXPERTS * (1.0 - jnp.sum(owns, axis=0, keepdims=True))
    return lrow, eb.astype(jnp.int32)


def _place(lrow, hn_hi):
    tm = hn_hi.shape[0]
    jiota = lax.broadcasted_iota(jnp.int32, (LOCAL_ROWS, tm), 0).astype(F32)
    place = jnp.where(jiota == lrow[:, :tm], 1.0, jnp.where(jiota == lrow[:, tm:], 1.0, 0.0)).astype(BF16)
    return jnp.dot(place, hn_hi, preferred_element_type=F32)


def _merge_and_route(oa, ob, proj, x2, wa, wb, wo, g2, wr_cat, br):
    t = x2.shape[0]
    n_assign = TOP_K * MERGE_TM
    upper = (jnp.arange(n_assign)[:, None] < jnp.arange(n_assign)[None, :]).astype(BF16)
    lower = (jnp.arange(N_EXPERTS)[:, None] > jnp.arange(N_EXPERTS)[None, :]).astype(BF16)
    const = lambda shape: pl.BlockSpec(shape, lambda s: (0, 0), pipeline_mode=pl.Buffered(1))
    n_tiles = t // MERGE_TM
    cur = lambda s: jnp.minimum(s, n_tiles - 1)
    prev = lambda s: jnp.maximum(s - 1, 0)
    return pl.pallas_call(
        _merge_kernel,
        grid=(n_tiles + 1,),
        in_specs=[
            pl.BlockSpec((MERGE_TM, A_WIDTH), lambda s: (cur(s), 0)),
            pl.BlockSpec((MERGE_TM, B_WIDTH), lambda s: (cur(s), 0)),
            pl.BlockSpec((MERGE_TM, D_MODEL), lambda s: (cur(s), COL_GA // D_MODEL)),
            pl.BlockSpec((MERGE_TM, D_MODEL), lambda s: (cur(s), COL_GB // D_MODEL)),
            pl.BlockSpec((MERGE_TM, D_MODEL), lambda s: (cur(s), 0)),
            const((A_WIDTH, D_MODEL)), const((B_WIDTH, D_MODEL)), const((D_MODEL, D_MODEL)),
            const((1, D_MODEL)), const((D_MODEL, ROUTER_LANES)), const((1, ROUTER_LANES)),
            const((n_assign, n_assign)), const((N_EXPERTS, N_EXPERTS)),
        ],
        out_specs=[
            pl.BlockSpec((MERGE_TM, D_MODEL), lambda s: (cur(s), 0)),
            pl.BlockSpec((LOCAL_ROWS, D_MODEL // 2), lambda s: (prev(s), 0)),
            pl.BlockSpec((1, 1, ROUTER_LANES), lambda s: (prev(s), 0, 0)),
            pl.BlockSpec((MERGE_TM, ROUTER_LANES), lambda s: (prev(s), 0)),
        ],
        scratch_shapes=[pltpu.VMEM((2, MERGE_TM, D_MODEL), BF16)],
        out_shape=[
            jax.ShapeDtypeStruct((t, D_MODEL), F32),
            jax.ShapeDtypeStruct((t // MERGE_TM * LOCAL_ROWS, D_MODEL // 2), jnp.uint32),
            jax.ShapeDtypeStruct((t // MERGE_TM, 1, ROUTER_LANES), jnp.int32),
            jax.ShapeDtypeStruct((t, ROUTER_LANES), F32),
        ],
        compiler_params=pltpu.CompilerParams(
            dimension_semantics=("arbitrary",), vmem_limit_bytes=VMEM_LIMIT),
        name="merge_and_route",
    )(oa, ob, proj, proj, x2, wa, wb, wo, g2, wr_cat, br, upper, lower)


def _expert_kernel(blk_ref, texp_ref, tcb_ref, tnb_ref, wslot_ref, wnext_ref, nused_ref,
                   xs_hbm, wg_hbm, wu_hbm, wd_hbm, y_hbm,
                   xbuf, ybuf, wgf, wuf, wdf, wgb, wub, wdb, gsem, ssem, wsem):
    n_used = nused_ref[0]

    def weight_copies(e, ws):
        return [pltpu.make_async_copy(src.at[e], dst.at[ws], wsem.at[ws])
                for src, dst in ((wg_hbm, wgf), (wu_hbm, wuf), (wd_hbm, wdf))]

    def gather_copy(blk, dst_slot, j):
        return pltpu.make_async_copy(xs_hbm.at[blk], xbuf.at[dst_slot, pl.ds(j * LOCAL_BLOCK, LOCAL_BLOCK)],
                                     gsem.at[dst_slot])

    def scatter_copy(blk, src_slot, j):
        return pltpu.make_async_copy(ybuf.at[src_slot, pl.ds(j * LOCAL_BLOCK, LOCAL_BLOCK)], y_hbm.at[blk],
                                     ssem.at[src_slot])

    def issue_gather(tile, dst_slot):
        @pl.when(texp_ref[tile] < N_EXPERTS)
        def _():
            base = tcb_ref[tile]
            for j in range(MOE_BLOCKS):
                gather_copy(blk_ref[base + j], dst_slot, j).start()

    def wait_gather(tile, s):
        @pl.when(texp_ref[tile] < N_EXPERTS)
        def _():
            for j in range(MOE_BLOCKS):
                gather_copy(0, s, j).wait()

    def for_valid_blocks(tile, fn):
        n_valid = tnb_ref[tile]

        @pl.when(n_valid == MOE_BLOCKS)
        def _():
            for j in range(MOE_BLOCKS):
                fn(j)

        @pl.when(n_valid < MOE_BLOCKS)
        def _():
            for j in range(MOE_BLOCKS):
                pl.when(j < n_valid)(functools.partial(fn, j))

    def issue_scatter(tile, src_slot):
        base = tcb_ref[tile]
        for_valid_blocks(tile, lambda j: scatter_copy(blk_ref[base + j], src_slot, j).start())

    def wait_scatter(tile, s):
        for_valid_blocks(tile, lambda j: scatter_copy(0, s, j).wait())

    @pl.when(n_used > 0)
    def _():
        issue_gather(0, 0)

        @pl.when(texp_ref[0] < N_EXPERTS)
        def _():
            for cp in weight_copies(texp_ref[0], wslot_ref[0]):
                cp.start()

    def tile_step(i, carry):
        slot = lax.rem(i, 2)
        expert = texp_ref[i]
        wait_gather(i, slot)

        @pl.when(i + 1 < n_used)
        def _():
            issue_gather(i + 1, 1 - slot)

        @pl.when(i >= 2)
        def _():
            wait_scatter(i - 2, slot)

        @pl.when(jnp.logical_and(expert < N_EXPERTS, jnp.logical_or(i == 0, expert != texp_ref[jnp.maximum(i - 1, 0)])))
        def _():
            ws = wslot_ref[i]
            for cp in weight_copies(expert, ws):
                cp.wait()
            for static_ws in (0, 1):
                @pl.when(ws == static_ws)
                def _():
                    wgb[...] = wgf[static_ws].astype(BF16)
                    wub[...] = wuf[static_ws].astype(BF16)
                    wdb[...] = wdf[static_ws].astype(BF16)
            nxt = wnext_ref[i]

            @pl.when(nxt < N_EXPERTS)
            def _():
                for cp in weight_copies(nxt, 1 - ws):
                    cp.start()

        @pl.when(expert < N_EXPERTS)
        def _():
            x_lo, x_hi = _unpack_bf16_pairs(xbuf[slot])
            half = D_MODEL // 2
            g = (jnp.dot(x_lo, wgb[:half, :], preferred_element_type=F32)
                 + jnp.dot(x_hi, wgb[half:, :], preferred_element_type=F32))
            u = (jnp.dot(x_lo, wub[:half, :], preferred_element_type=F32)
                 + jnp.dot(x_hi, wub[half:, :], preferred_element_type=F32))
            act = (g * jax.nn.sigmoid(g) * u).astype(BF16)
            y = jnp.dot(act, wdb[...], preferred_element_type=F32)
            ybuf[slot] = _pack_bf16_pairs(y.astype(BF16).astype(F32))

        @pl.when(expert >= N_EXPERTS)
        def _():
            ybuf[slot] = jnp.zeros((MOE_TM, D_MODEL // 2), jnp.uint32)

        issue_scatter(i, slot)
        return carry

    lax.fori_loop(0, n_used, tile_step, 0)

    for back in (1, 2):
        @pl.when(n_used >= back)
        def _():
            last = jnp.maximum(n_used - back, 0)
            wait_scatter(last, lax.rem(last, 2))


def _expert_ffn(plan, xs_local, wg, wu, wd):
    n_blocks = xs_local.shape[0] // LOCAL_BLOCK
    half = D_MODEL // 2
    any_spec = pl.BlockSpec(memory_space=pl.ANY)
    grid_spec = pltpu.PrefetchScalarGridSpec(
        num_scalar_prefetch=len(plan),
        grid=(1,),
        in_specs=[any_spec, any_spec, any_spec, any_spec],
        out_specs=any_spec,
        scratch_shapes=[
            pltpu.VMEM((2, MOE_TM, half), jnp.uint32), pltpu.VMEM((2, MOE_TM, half), jnp.uint32),
            pltpu.VMEM((2, D_MODEL, D_EXPERT), F32), pltpu.VMEM((2, D_MODEL, D_EXPERT), F32),
            pltpu.VMEM((2, D_EXPERT, D_MODEL), F32),
            pltpu.VMEM((D_MODEL, D_EXPERT), BF16), pltpu.VMEM((D_MODEL, D_EXPERT), BF16),
            pltpu.VMEM((D_EXPERT, D_MODEL), BF16),
            pltpu.SemaphoreType.DMA((2,)), pltpu.SemaphoreType.DMA((2,)), pltpu.SemaphoreType.DMA((2,)),
        ],
    )
    return pl.pallas_call(
        _expert_kernel,
        grid_spec=grid_spec,
        out_shape=jax.ShapeDtypeStruct((n_blocks, LOCAL_BLOCK, half), jnp.uint32),
        compiler_params=pltpu.CompilerParams(
            dimension_semantics=("arbitrary",), vmem_limit_bytes=VMEM_LIMIT),
        name="expert_ffn",
    )(*plan, xs_local.reshape(n_blocks, LOCAL_BLOCK, half), wg, wu, wd)


def _combine_kernel(h_ref, wcol_ref, y_ref, o_ref):
    y_lo, y_hi = _unpack_bf16_pairs(y_ref[...])
    w = wcol_ref[...]
    jiota = lax.broadcasted_iota(jnp.int32, (MERGE_TM, LOCAL_ROWS), 1).astype(F32)
    pick = (jnp.where(jiota == w[:, 2:3], w[:, 0:1], 0.0) + jnp.where(jiota == w[:, 3:4], w[:, 1:2], 0.0)).astype(BF16)
    half = D_MODEL // 2
    for cols, y in ((slice(0, half), y_lo), (slice(half, D_MODEL), y_hi)):
        o_ref[:, cols] = h_ref[:, cols] + jnp.dot(pick, y, preferred_element_type=F32)


def _combine(h, wcol, y_local):
    t = h.shape[0]
    return pl.pallas_call(
        _combine_kernel,
        grid=(t // MERGE_TM,),
        in_specs=[
            pl.BlockSpec((MERGE_TM, D_MODEL), lambda i: (i, 0)),
            pl.BlockSpec((MERGE_TM, ROUTER_LANES), lambda i: (i, 0)),
            pl.BlockSpec((LOCAL_ROWS, D_MODEL // 2), lambda i: (i, 0)),
        ],
        out_specs=pl.BlockSpec((MERGE_TM, D_MODEL), lambda i: (i, 0)),
        out_shape=jax.ShapeDtypeStruct((t, D_MODEL), F32),
        compiler_params=pltpu.CompilerParams(
            dimension_semantics=("parallel",), vmem_limit_bytes=VMEM_LIMIT),
        name="combine",
    )(h, wcol, y_local)


def _dispatch_plan(block_expert):
    n_blocks = block_expert.size
    n_keys = N_EXPERTS + 1
    n_tiles = n_blocks // MOE_BLOCKS + n_keys
    keys = block_expert.reshape(n_blocks)
    order = jnp.argsort(keys, stable=True).astype(jnp.int32)
    counts = jnp.sum(keys[None, :] == jnp.arange(n_keys, dtype=jnp.int32)[:, None], axis=1).astype(jnp.int32)
    tiles_per = (counts + MOE_BLOCKS - 1) // MOE_BLOCKS
    tile_end = jnp.cumsum(tiles_per)
    tile_first = tile_end - tiles_per
    start = jnp.cumsum(counts) - counts
    tile = jnp.arange(n_tiles, dtype=jnp.int32)
    tile_expert = jnp.minimum(jnp.sum(tile_end[None, :] <= tile[:, None], axis=1), n_keys - 1).astype(jnp.int32)
    sel = (tile_expert[:, None] == jnp.arange(n_keys, dtype=jnp.int32)[None, :]).astype(jnp.int32)
    pick = lambda v: jnp.sum(sel * v[None, :], axis=1)
    within = (tile - pick(tile_first)) * MOE_BLOCKS
    tile_base = jnp.clip(pick(start) + within, 0, n_blocks).astype(jnp.int32)
    tile_blocks = jnp.clip(pick(counts) - within, 0, MOE_BLOCKS).astype(jnp.int32)
    n_used = tile_end[-1].astype(jnp.int32).reshape(1)
    order = jnp.concatenate([order, jnp.zeros((MOE_BLOCKS,), jnp.int32)])
    key = jnp.arange(n_keys, dtype=jnp.int32)
    present = jnp.logical_and(counts > 0, key < N_EXPERTS)
    slot_of = (jnp.cumsum(present) - present) % 2
    later = jnp.logical_and(key[None, :] > key[:, None], present[None, :])
    next_of = jnp.min(jnp.where(later, key[None, :], N_EXPERTS), axis=1)
    return (order, tile_expert, tile_base, tile_blocks, pick(slot_of).astype(jnp.int32),
            pick(next_of).astype(jnp.int32), n_used)


def _layer(x, norm1_g, w_in, q_norm_a, k_norm_a, sink_a, q_norm_b, k_norm_b, rpb_b, w_up_a, w_up_b, w_out,
           rel_table, norm2_g, w_rg, b_rg, w_re, b_re, w_gate, w_up, w_down):
    batch, seq, d = x.shape
    t = batch * seq
    x2 = x.reshape(t, d)

    kv_a = slice(A_WIDTH, A_WIDTH + 2 * A_KV_WIDTH)
    w_cat = jnp.concatenate([w_in[:, :kv_a.start], w_in[:, kv_a.stop:], w_in[:, kv_a]], axis=1).astype(BF16)
    scale = HEAD_DIM ** -0.5
    ones = lambda n: jnp.ones((n,), F32)
    gain_cat = jnp.concatenate([
        jnp.tile(q_norm_a.astype(F32) * scale, A_HEADS), jnp.tile(q_norm_b.astype(F32) * scale, B_HEADS),
        jnp.tile(k_norm_b.astype(F32), B_HEADS), ones(B_WIDTH + 2 * D_MODEL),
        jnp.tile(k_norm_a.astype(F32), A_KV_HEADS), ones(A_KV_WIDTH)]).reshape(1, IN_COLS)

    kind = lambda v, n: jnp.full((n,), v, F32)
    kind_cat = jnp.concatenate([kind(1.0, COL_VB), kind(0.0, B_WIDTH), kind(2.0, 2 * D_MODEL),
                                kind(1.0, A_KV_WIDTH), kind(0.0, A_KV_WIDTH)]).reshape(1, IN_COLS)

    proj = _in_projection(x2, norm1_g.astype(F32).reshape(1, d), w_cat, gain_cat, kind_cat)
    oa = _window_attention(proj, _window_bias(rel_table), sink_a.astype(F32), batch, seq)
    ob = _neighbourhood_attention(proj, _neighbourhood_col_bias(rpb_b), batch, seq)

    half = ROUTER_LANES // 2
    wr_t = jnp.concatenate([w_re.astype(F32).T, w_rg.astype(F32).T,
                            jnp.zeros((half - N_EXPERTS - N_GROUPS, d), F32)], axis=0)
    wr_hi = wr_t.astype(BF16)
    wr_lo = (wr_t - wr_hi.astype(F32)).astype(BF16)
    wr_cat = jnp.concatenate([wr_hi, wr_lo], axis=0).T
    br = jnp.pad(jnp.concatenate([b_re.astype(F32), b_rg.astype(F32)]), (0, ROUTER_LANES - N_EXPERTS - N_GROUPS))

    h, xs_local, block_expert, wcol = _merge_and_route(
        oa, ob, proj, x2, w_up_a.astype(BF16), w_up_b.astype(BF16), w_out.astype(BF16),
        norm2_g.astype(F32).reshape(1, d), wr_cat, br.reshape(1, ROUTER_LANES))

    plan = _dispatch_plan(block_expert[:, 0, :LOCAL_BLOCKS])
    y_local = _expert_ffn(plan, xs_local, w_gate.astype(F32), w_up.astype(F32), w_down.astype(F32))
    out = _combine(h, wcol, y_local.reshape(-1, d // 2))
    return out.reshape(batch, seq, d)


def kernel(x, norm1_g, w_in, q_norm_a, k_norm_a, sink_a, q_norm_b, k_norm_b, rpb_b, w_up_a, w_up_b, w_out,
           rel_bias_table, norm2_g, w_router_group, b_router_group, w_router_expert, b_router_expert,
           w_gate, w_up, w_down):
    return _layer(x, norm1_g[0], w_in[0], q_norm_a[0], k_norm_a[0], sink_a[0], q_norm_b[0], k_norm_b[0],
                  rpb_b[0], w_up_a[0], w_up_b[0], w_out[0], rel_bias_table, norm2_g[0],
                  w_router_group[0], b_router_group[0], w_router_expert[0], b_router_expert[0],
                  w_gate[0], w_up[0], w_down[0])
```

```python
import functools
import math

import jax
import jax.numpy as jnp
from jax import lax
from jax.experimental import pallas as pl
from jax.experimental.pallas import tpu as pltpu

F32 = jnp.float32
BF16 = jnp.bfloat16

D_MODEL = 2048
HEAD_DIM = 128
A_HEADS = 8
A_KV_HEADS = 2
A_GROUP = A_HEADS // A_KV_HEADS
A_WIDTH = A_HEADS * HEAD_DIM
A_KV_WIDTH = A_KV_HEADS * HEAD_DIM
WINDOW = 128
A_BLOCK = 128
NUM_BUCKETS = 32
MAX_DISTANCE = 128
B_HEADS = 8
B_WIDTH = B_HEADS * HEAD_DIM
GRID_W = 64
NB_ROWS = 8
NB_COLS = 16
IN_COLS = A_WIDTH + 2 * A_KV_WIDTH + 3 * B_WIDTH + 2 * D_MODEL
N_GROUPS = 4
EXPERTS_PER_GROUP = 8
N_EXPERTS = N_GROUPS * EXPERTS_PER_GROUP
TOP_K = 2
D_EXPERT = 512
EPS = 1e-6
NEG_INF = -1e30

COL_QA = 0
COL_QB = COL_QA + A_WIDTH
COL_KB = COL_QB + B_WIDTH
COL_VB = COL_KB + B_WIDTH
COL_GA = COL_VB + B_WIDTH
COL_GB = COL_GA + D_MODEL
COL_KA = COL_GB + D_MODEL
COL_VA = COL_KA + A_KV_WIDTH

PROJ_TM = 2048
PROJ_TN = 512
PROJ_SUB = 128
WIN_STEP = 16
NA_ROWS = 4
NA_BLOCK = NA_ROWS * GRID_W
MERGE_TM = 256
ROUTER_LANES = 128
LOCAL_BLOCK = 8
LOCAL_ROWS = 768
LOCAL_BLOCKS = LOCAL_ROWS // LOCAL_BLOCK
assert LOCAL_ROWS >= TOP_K * MERGE_TM + N_EXPERTS * (LOCAL_BLOCK - 1) and LOCAL_BLOCKS <= ROUTER_LANES
MOE_TM = 256
MOE_BLOCKS = MOE_TM // LOCAL_BLOCK
VMEM_LIMIT = 56 * 1024 * 1024


def _head_norm(a, gain):
    outs = []
    for hd in range(a.shape[1] // HEAD_DIM):
        sl = slice(hd * HEAD_DIM, (hd + 1) * HEAD_DIM)
        ah = a[:, sl]
        ss = jnp.sum(ah * ah, axis=-1, keepdims=True)
        outs.append(ah * lax.rsqrt(ss * (1.0 / HEAD_DIM) + EPS) * gain[:, sl])
    return outs[0] if len(outs) == 1 else jnp.concatenate(outs, axis=1)


def _inproj_kernel(x_hbm, g1_ref, w_ref, gain_ref, kind_ref, o_ref, xn_ref, xbuf, xsem):
    i = pl.program_id(0)
    j = pl.program_id(1)

    def x_copy(tile, slot):
        return pltpu.make_async_copy(x_hbm.at[pl.ds(tile * PROJ_TM, PROJ_TM)], xbuf.at[slot], xsem.at[slot])

    @pl.when(j == 0)
    def _():
        @pl.when(i == 0)
        def _():
            x_copy(0, 0).start()

        x_copy(i, lax.rem(i, 2)).wait()

        @pl.when(i + 1 < pl.num_programs(0))
        def _():
            x_copy(i + 1, lax.rem(i + 1, 2)).start()

    def project(x_slot):
        kind = kind_ref[...]
        for r in range(PROJ_TM // PROJ_SUB):
            rows = slice(r * PROJ_SUB, (r + 1) * PROJ_SUB)
            if x_slot is not None:
                x = xbuf[x_slot, rows, :]
                ms = jnp.mean(x * x, axis=-1, keepdims=True)
                xn_ref[rows, :] = (x * lax.rsqrt(ms + EPS) * g1_ref[...]).astype(BF16)
            acc = jnp.dot(xn_ref[rows, :], w_ref[...], preferred_element_type=F32)
            out = jnp.where(kind == 1.0, _head_norm(acc, gain_ref[...]),
                            jnp.where(kind == 2.0, jax.nn.sigmoid(acc), acc))
            o_ref[rows, :] = out.astype(o_ref.dtype)

    for x_slot in (0, 1):
        pl.when(jnp.logical_and(j == 0, lax.rem(i, 2) == x_slot))(functools.partial(project, x_slot))
    pl.when(j != 0)(functools.partial(project, None))


def _source_tile(j):
    kv_first, kv_tiles, n_tiles = A_WIDTH // PROJ_TN, 2 * A_KV_WIDTH // PROJ_TN, IN_COLS // PROJ_TN
    return jnp.where(j < kv_first, j, jnp.where(j >= n_tiles - kv_tiles, j - (n_tiles - kv_tiles - kv_first), j + kv_tiles))


def _in_projection(x2, g1, w_cat, gain_cat, kind_cat):
    t = x2.shape[0]
    return pl.pallas_call(
        _inproj_kernel,
        grid=(t // PROJ_TM, IN_COLS // PROJ_TN),
        in_specs=[
            pl.BlockSpec(memory_space=pl.ANY),
            pl.BlockSpec((1, D_MODEL), lambda i, j: (0, 0)),
            pl.BlockSpec((D_MODEL, PROJ_TN), lambda i, j: (0, _source_tile(j))),
            pl.BlockSpec((1, PROJ_TN), lambda i, j: (0, j)),
            pl.BlockSpec((1, PROJ_TN), lambda i, j: (0, j)),
        ],
        out_specs=pl.BlockSpec((PROJ_TM, PROJ_TN), lambda i, j: (i, j)),
        out_shape=jax.ShapeDtypeStruct((t, IN_COLS), BF16),
        scratch_shapes=[pltpu.VMEM((PROJ_TM, D_MODEL), BF16), pltpu.VMEM((2, PROJ_TM, D_MODEL), F32),
                        pltpu.SemaphoreType.DMA((2,))],
        compiler_params=pltpu.CompilerParams(
            dimension_semantics=("arbitrary", "arbitrary"), vmem_limit_bytes=VMEM_LIMIT),
        name="in_projection",
    )(x2, g1, w_cat, gain_cat, kind_cat)


def _t5_bucket(rel):
    nb = NUM_BUCKETS // 2
    ret = jnp.where(rel > 0, nb, 0)
    n = jnp.abs(rel)
    max_exact = nb // 2
    nf = jnp.maximum(n, 1).astype(jnp.float32)
    large = max_exact + (jnp.log(nf / max_exact) / math.log(MAX_DISTANCE / max_exact)
                         * (nb - max_exact)).astype(jnp.int32)
    large = jnp.minimum(large, nb - 1)
    return ret + jnp.where(n < max_exact, n, large)


def _window_bias(rel_table):
    q_loc = jnp.arange(A_BLOCK)
    k_loc = jnp.arange(3 * A_BLOCK) - WINDOW
    rel = k_loc[None, :] - q_loc[:, None]
    onehot = (_t5_bucket(rel)[..., None] == jnp.arange(NUM_BUCKETS)).astype(F32)
    bias = jnp.einsum('qkn,nh->hqk', onehot, rel_table.astype(F32), precision=lax.Precision.HIGHEST)
    in_window = jnp.abs(rel) <= WINDOW
    has_prev = jnp.array([False, True, True])[:, None, None]
    has_next = jnp.array([True, True, False])[:, None, None]
    k_blk = (jnp.arange(3 * A_BLOCK) // A_BLOCK)[None, None, :]
    valid = in_window[None] & ((k_blk != 0) | has_prev) & ((k_blk != 2) | has_next)
    full = jnp.where(valid[:, None], bias[None], NEG_INF)
    return full.reshape(3, A_KV_HEADS, A_GROUP * A_BLOCK, 3 * A_BLOCK)


def _window_kernel(sink_ref, q_ref, kp_ref, ko_ref, kn_ref, vp_ref, vo_ref, vn_ref,
                   bias_lo_ref, bias_mid_ref, bias_hi_ref, o_ref):
    for kvh in range(A_KV_HEADS):
        hs = slice(kvh * HEAD_DIM, (kvh + 1) * HEAD_DIM)
        heads = [kvh * A_GROUP + g for g in range(A_GROUP)]
        kc = jnp.concatenate([kp_ref[:, hs], ko_ref[:, hs], kn_ref[:, hs]], axis=0)
        vc = jnp.concatenate([vp_ref[:, hs], vo_ref[:, hs], vn_ref[:, hs]], axis=0)
        vc1 = jnp.concatenate([vc, jnp.ones_like(vc)], axis=1)
        sink = jnp.concatenate([jnp.full((A_BLOCK, 1), sink_ref[h], F32) for h in heads], axis=0)
        for jq in range(WIN_STEP):
            rows = slice(jq * A_BLOCK, (jq + 1) * A_BLOCK)
            band = slice(jq * A_BLOCK, (jq + 3) * A_BLOCK)
            bias_ref = bias_lo_ref if jq == 0 else (bias_hi_ref if jq == WIN_STEP - 1 else bias_mid_ref)
            q4 = jnp.concatenate([q_ref[rows, h * HEAD_DIM:(h + 1) * HEAD_DIM] for h in heads], axis=0)
            s = lax.dot_general(q4, kc[band], (((1,), (1,)), ((), ())), preferred_element_type=F32)
            s = s + bias_ref[0, kvh]
            m = jnp.max(s, axis=-1, keepdims=True)
            p = jnp.exp((s - m).astype(BF16))
            ov = jnp.dot(p, vc1[band], preferred_element_type=F32)
            o = ov[:, :HEAD_DIM] / (ov[:, HEAD_DIM:] + jnp.exp(sink - m))
            for g, h in enumerate(heads):
                o_ref[rows, h * HEAD_DIM:(h + 1) * HEAD_DIM] = o[g * A_BLOCK:(g + 1) * A_BLOCK].astype(o_ref.dtype)


def _window_attention(proj, bias_a, sink, batch, seq):
    nb = seq // A_BLOCK
    ns = nb // WIN_STEP
    kcol, vcol = COL_KA // A_KV_WIDTH, COL_VA // A_KV_WIDTH
    prev = lambda b, n: b * nb + jnp.maximum(n * WIN_STEP - 1, 0)
    nxt = lambda b, n: b * nb + jnp.minimum((n + 1) * WIN_STEP, nb - 1)
    own = lambda b, n: b * ns + n
    edge = lambda rfn, col: pl.BlockSpec((A_BLOCK, A_KV_WIDTH), lambda b, n: (rfn(b, n), col))
    main = lambda col: pl.BlockSpec((WIN_STEP * A_BLOCK, A_KV_WIDTH), lambda b, n: (own(b, n), col))
    bias_block = (1, A_KV_HEADS, A_GROUP * A_BLOCK, 3 * A_BLOCK)
    return pl.pallas_call(
        _window_kernel,
        grid=(batch, ns),
        in_specs=[
            pl.BlockSpec(memory_space=pltpu.SMEM),
            pl.BlockSpec((WIN_STEP * A_BLOCK, A_WIDTH), lambda b, n: (own(b, n), COL_QA // A_WIDTH)),
            edge(prev, kcol), main(kcol), edge(nxt, kcol),
            edge(prev, vcol), main(vcol), edge(nxt, vcol),
            pl.BlockSpec(bias_block, lambda b, n: (jnp.where(n == 0, 0, 1), 0, 0, 0)),
            pl.BlockSpec(bias_block, lambda b, n: (1, 0, 0, 0)),
            pl.BlockSpec(bias_block, lambda b, n: (jnp.where(n == ns - 1, 2, 1), 0, 0, 0)),
        ],
        out_specs=pl.BlockSpec((WIN_STEP * A_BLOCK, A_WIDTH), lambda b, n: (own(b, n), 0)),
        out_shape=jax.ShapeDtypeStruct((batch * seq, A_WIDTH), BF16),
        compiler_params=pltpu.CompilerParams(
            dimension_semantics=("parallel", "arbitrary"), vmem_limit_bytes=VMEM_LIMIT),
        name="window_attention",
    )(sink, proj, proj, proj, proj, proj, proj, proj, bias_a, bias_a, bias_a)


def _neighbourhood_col_bias(rpb):
    cols = jnp.arange(GRID_W)
    cs = jnp.clip(cols - NB_COLS // 2, 0, GRID_W - NB_COLS)
    col_ok = (cols[None, :] >= cs[:, None]) & (cols[None, :] < cs[:, None] + NB_COLS)
    dc = cols[None, :] - cols[:, None] + (NB_COLS - 1)
    col_sel = ((dc[..., None] == jnp.arange(2 * NB_COLS - 1)) & col_ok[..., None]).astype(F32)
    by_col = jnp.einsum('hde,cje->hdcj', rpb.astype(F32), col_sel, precision=lax.Precision.HIGHEST)
    return jnp.where(col_ok[None, None], by_col, NEG_INF)


def _neighbourhood_row_plan(cls, rows):
    kh = min(NB_ROWS, rows)
    r0 = (0, NA_ROWS, rows - NA_ROWS)[cls]
    plan = []
    for i in range(NA_ROWS):
        r = r0 + i
        rs = min(max(r - kh // 2, 0), rows - kh)
        plan.append([key_r - r + NB_ROWS - 1 if rs <= key_r < rs + kh else None
                     for key_r in range(r0 - NA_ROWS, r0 + 2 * NA_ROWS)])
    return plan


def _neighbourhood_kernel(q_ref, kp_ref, ko_ref, kn_ref, vp_ref, vo_ref, vn_ref, colbias_ref, o_ref, bias_ref,
                          *, rows):
    n = pl.program_id(1)
    n_blocks = pl.num_programs(1)

    def build_bias(cls):
        plan = _neighbourhood_row_plan(cls, rows)
        masked = jnp.full((GRID_W, GRID_W), NEG_INF, F32)
        for h in range(B_HEADS):
            for i in range(NA_ROWS):
                for pair in range(3 * NA_ROWS // 2):
                    tiles = [masked if plan[i][kr] is None else colbias_ref[h, plan[i][kr]]
                             for kr in (2 * pair, 2 * pair + 1)]
                    bias_ref[h, i * GRID_W:(i + 1) * GRID_W, pair * 2 * GRID_W:(pair + 1) * 2 * GRID_W] = (
                        jnp.concatenate(tiles, axis=1))

    pl.when(n == 0)(functools.partial(build_bias, 0))
    pl.when(n == 1)(functools.partial(build_bias, 1))
    pl.when(n == n_blocks - 1)(functools.partial(build_bias, 2))

    for h in range(B_HEADS):
        hs = slice(h * HEAD_DIM, (h + 1) * HEAD_DIM)
        kc = jnp.concatenate([kp_ref[:, hs], ko_ref[:, hs], kn_ref[:, hs]], axis=0)
        vc = jnp.concatenate([vp_ref[:, hs], vo_ref[:, hs], vn_ref[:, hs]], axis=0)
        s = lax.dot_general(q_ref[:, hs], kc, (((1,), (1,)), ((), ())), preferred_element_type=F32)
        s = s + bias_ref[h]
        m = jnp.max(s, axis=-1, keepdims=True)
        p = jnp.exp((s - m).astype(BF16))
        ov = jnp.dot(p, jnp.concatenate([vc, jnp.ones_like(vc)], axis=1), preferred_element_type=F32)
        o_ref[:, hs] = (ov[:, :HEAD_DIM] / ov[:, HEAD_DIM:]).astype(o_ref.dtype)


def _neighbourhood_attention(proj, col_bias, batch, seq):
    nb = seq // NA_BLOCK
    assert nb >= 2

    def row(b, n):
        return b * nb + n

    prev = lambda b, n: row(b, jnp.maximum(n - 1, 0))
    nxt = lambda b, n: row(b, jnp.minimum(n + 1, nb - 1))
    spec = lambda rfn, col: pl.BlockSpec((NA_BLOCK, B_WIDTH), lambda b, n: (rfn(b, n), col))
    qcol, kcol, vcol = COL_QB // B_WIDTH, COL_KB // B_WIDTH, COL_VB // B_WIDTH
    return pl.pallas_call(
        functools.partial(_neighbourhood_kernel, rows=seq // GRID_W),
        grid=(batch, nb),
        in_specs=[
            spec(row, qcol),
            spec(prev, kcol), spec(row, kcol), spec(nxt, kcol),
            spec(prev, vcol), spec(row, vcol), spec(nxt, vcol),
            pl.BlockSpec(col_bias.shape, lambda b, n: (0, 0, 0, 0), pipeline_mode=pl.Buffered(1)),
        ],
        out_specs=pl.BlockSpec((NA_BLOCK, B_WIDTH), lambda b, n: (row(b, n), 0)),
        out_shape=jax.ShapeDtypeStruct((batch * seq, B_WIDTH), BF16),
        scratch_shapes=[pltpu.VMEM((B_HEADS, NA_BLOCK, 3 * NA_BLOCK), F32)],
        compiler_params=pltpu.CompilerParams(
            dimension_semantics=("arbitrary", "arbitrary"), vmem_limit_bytes=VMEM_LIMIT),
        name="neighbourhood_attention",
    )(proj, proj, proj, proj, proj, proj, proj, col_bias)


def _pack_bf16_pairs(a):
    n = a.shape[1] // 2
    bits = lax.bitcast_convert_type(a, jnp.uint32)
    return bits[:, n:] | (bits[:, :n] >> 16)


def _unpack_bf16_pairs(p):
    lo = lax.bitcast_convert_type(p << 16, F32).astype(BF16)
    hi = lax.bitcast_convert_type(p & jnp.uint32(0xFFFF0000), F32).astype(BF16)
    return lo, hi


def _merge_kernel(oa_ref, ob_ref, ga_ref, gb_ref, x_ref, wa_ref, wb_ref, wo_ref, g2_ref, wr_ref, br_ref,
                  upper_ref, lower_ref, h_ref, xs_ref, eb_ref, wcol_ref, hn_scr):
    s = pl.program_id(0)
    cur = lax.rem(s, 2)
    prev = 1 - cur
    tm = MERGE_TM

    @pl.when(s == 0)
    def _():
        hn_scr[...] = jnp.zeros_like(hn_scr)

    hn_prev = hn_scr[prev]
    logits = _router_logits(hn_prev, wr_ref, br_ref)
    ua = jnp.dot(oa_ref[...], wa_ref[...], preferred_element_type=F32)
    ub = jnp.dot(ob_ref[...], wb_ref[...], preferred_element_type=F32)
    ids, w1, w2 = _route(logits)
    lrow, eb = _local_sort(ids, upper_ref, lower_ref)
    merged = ga_ref[...].astype(F32) * ua + gb_ref[...].astype(F32) * ub
    acc = jnp.dot(merged.astype(BF16), wo_ref[...], preferred_element_type=F32)
    xs = _place(lrow, hn_prev)
    h = x_ref[...] + acc
    h_ref[...] = h
    ms = jnp.mean(h * h, axis=-1, keepdims=True)
    hn_scr[cur] = (h * lax.rsqrt(ms + EPS) * g2_ref[...]).astype(BF16)

    xs_ref[...] = _pack_bf16_pairs(xs)
    eb_ref[0] = eb
    riota = lax.broadcasted_iota(jnp.int32, (ROUTER_LANES, tm), 0)
    wrows = jnp.where(riota == 0, w1, jnp.where(riota == 1, w2,
                      jnp.where(riota == 2, lrow[:, :tm], jnp.where(riota == 3, lrow[:, tm:], 0.0))))
    wcol_ref[...] = wrows.T


def _router_logits(hn, wr_ref, br_ref):
    lg = jnp.dot(hn, wr_ref[...], preferred_element_type=F32)
    return lg + pltpu.roll(lg, ROUTER_LANES // 2, axis=1) + br_ref[...]


def _route(logits):
    tm = logits.shape[0]
    lt = logits.T
    gl = lt[N_EXPERTS:N_EXPERTS + N_GROUPS]
    gmax = jnp.max(gl, axis=0, keepdims=True)
    giota = lax.broadcasted_iota(jnp.int32, gl.shape, 0)
    g_idx = jnp.min(jnp.where(gl == gmax, giota, N_GROUPS), axis=0, keepdims=True)
    p_g = 1.0 / jnp.sum(jnp.exp(gl - gmax), axis=0, keepdims=True)
    e_sel = jnp.zeros((EXPERTS_PER_GROUP, tm), F32)
    for g in range(N_GROUPS):
        e_sel = jnp.where(g_idx == g, lt[g * EXPERTS_PER_GROUP:(g + 1) * EXPERTS_PER_GROUP], e_sel)
    eiota = lax.broadcasted_iota(jnp.int32, e_sel.shape, 0)
    m1 = jnp.max(e_sel, axis=0, keepdims=True)
    i1 = jnp.min(jnp.where(e_sel == m1, eiota, EXPERTS_PER_GROUP), axis=0, keepdims=True)
    rest = jnp.where(eiota == i1, -jnp.inf, e_sel)
    m2 = jnp.max(rest, axis=0, keepdims=True)
    i2 = jnp.min(jnp.where(rest == m2, eiota, EXPERTS_PER_GROUP), axis=0, keepdims=True)
    t = jnp.exp(m2 - m1)
    ids = jnp.concatenate([g_idx * EXPERTS_PER_GROUP + i1, g_idx * EXPERTS_PER_GROUP + i2], axis=1)
    return ids, p_g / (1.0 + t), p_g * t / (1.0 + t)


def _local_sort(ids, upper_ref, lower_ref):
    eiota32 = lax.broadcasted_iota(jnp.int32, (N_EXPERTS, ids.shape[1]), 0)
    onehot = jnp.where(eiota32 == ids, 1.0, 0.0)
    count = jnp.sum(onehot, axis=1, keepdims=True)
    before = jnp.dot(onehot.astype(BF16), upper_ref[...], preferred_element_type=F32)
    nblk = jnp.floor((count + (LOCAL_BLOCK - 1)) * (1.0 / LOCAL_BLOCK))
    boff = jnp.dot(lower_ref[...], jnp.broadcast_to(nblk, (N_EXPERTS, ROUTER_LANES)).astype(BF16),
                   preferred_element_type=F32)[:, 0:1]
    lrow = jnp.sum(onehot * (before + LOCAL_BLOCK * boff), axis=0, keepdims=True)
    biota = lax.broadcasted_iota(jnp.int32, (N_EXPERTS, ROUTER_LANES), 1).astype(F32)
    owns = jnp.where(jnp.logical_and(biota >= boff, biota < boff + nblk), 1.0, 0.0)
    eiota_f = lax.broadcasted_iota(jnp.int32, (N_EXPERTS, ROUTER_LANES), 0).astype(F32)
    eb = jnp.sum(owns * eiota_f, axis=0, keepdims=True) + N_EXPERTS * (1.0 - jnp.sum(owns, axis=0, keepdims=True))
    return lrow, eb.astype(jnp.int32)


def _place(lrow, hn_hi):
    tm = hn_hi.shape[0]
    jiota = lax.broadcasted_iota(jnp.int32, (LOCAL_ROWS, tm), 0).astype(F32)
    place = jnp.where(jiota == lrow[:, :tm], 1.0, jnp.where(jiota == lrow[:, tm:], 1.0, 0.0)).astype(BF16)
    return jnp.dot(place, hn_hi, preferred_element_type=F32)


def _merge_and_route(oa, ob, proj, x2, wa, wb, wo, g2, wr_cat, br):
    t = x2.shape[0]
    n_assign = TOP_K * MERGE_TM
    upper = (jnp.arange(n_assign)[:, None] < jnp.arange(n_assign)[None, :]).astype(BF16)
    lower = (jnp.arange(N_EXPERTS)[:, None] > jnp.arange(N_EXPERTS)[None, :]).astype(BF16)
    const = lambda shape: pl.BlockSpec(shape, lambda s: (0, 0), pipeline_mode=pl.Buffered(1))
    n_tiles = t // MERGE_TM
    cur = lambda s: jnp.minimum(s, n_tiles - 1)
    prev = lambda s: jnp.maximum(s - 1, 0)
    return pl.pallas_call(
        _merge_kernel,
        grid=(n_tiles + 1,),
        in_specs=[
            pl.BlockSpec((MERGE_TM, A_WIDTH), lambda s: (cur(s), 0)),
            pl.BlockSpec((MERGE_TM, B_WIDTH), lambda s: (cur(s), 0)),
            pl.BlockSpec((MERGE_TM, D_MODEL), lambda s: (cur(s), COL_GA // D_MODEL)),
            pl.BlockSpec((MERGE_TM, D_MODEL), lambda s: (cur(s), COL_GB // D_MODEL)),
            pl.BlockSpec((MERGE_TM, D_MODEL), lambda s: (cur(s), 0)),
            const((A_WIDTH, D_MODEL)), const((B_WIDTH, D_MODEL)), const((D_MODEL, D_MODEL)),
            const((1, D_MODEL)), const((D_MODEL, ROUTER_LANES)), const((1, ROUTER_LANES)),
            const((n_assign, n_assign)), const((N_EXPERTS, N_EXPERTS)),
        ],
        out_specs=[
            pl.BlockSpec((MERGE_TM, D_MODEL), lambda s: (cur(s), 0)),
            pl.BlockSpec((LOCAL_ROWS, D_MODEL // 2), lambda s: (prev(s), 0)),
            pl.BlockSpec((1, 1, ROUTER_LANES), lambda s: (prev(s), 0, 0)),
            pl.BlockSpec((MERGE_TM, ROUTER_LANES), lambda s: (prev(s), 0)),
        ],
        scratch_shapes=[pltpu.VMEM((2, MERGE_TM, D_MODEL), BF16)],
        out_shape=[
            jax.ShapeDtypeStruct((t, D_MODEL), F32),
            jax.ShapeDtypeStruct((t // MERGE_TM * LOCAL_ROWS, D_MODEL // 2), jnp.uint32),
            jax.ShapeDtypeStruct((t // MERGE_TM, 1, ROUTER_LANES), jnp.int32),
            jax.ShapeDtypeStruct((t, ROUTER_LANES), F32),
        ],
        compiler_params=pltpu.CompilerParams(
            dimension_semantics=("arbitrary",), vmem_limit_bytes=VMEM_LIMIT),
        name="merge_and_route",
    )(oa, ob, proj, proj, x2, wa, wb, wo, g2, wr_cat, br, upper, lower)


def _expert_kernel(blk_ref, texp_ref, tcb_ref, tnb_ref, wslot_ref, wnext_ref, nused_ref,
                   xs_hbm, wg_hbm, wu_hbm, wd_hbm, y_hbm,
                   xbuf, ybuf, wgf, wuf, wdf, wgb, wub, wdb, gsem, ssem, wsem):
    n_used = nused_ref[0]

    def weight_copies(e, ws):
        return [pltpu.make_async_copy(src.at[e], dst.at[ws], wsem.at[ws])
                for src, dst in ((wg_hbm, wgf), (wu_hbm, wuf), (wd_hbm, wdf))]

    def gather_copy(blk, dst_slot, j):
        return pltpu.make_async_copy(xs_hbm.at[blk], xbuf.at[dst_slot, pl.ds(j * LOCAL_BLOCK, LOCAL_BLOCK)],
                                     gsem.at[dst_slot])

    def scatter_copy(blk, src_slot, j):
        return pltpu.make_async_copy(ybuf.at[src_slot, pl.ds(j * LOCAL_BLOCK, LOCAL_BLOCK)], y_hbm.at[blk],
                                     ssem.at[src_slot])

    def issue_gather(tile, dst_slot):
        @pl.when(texp_ref[tile] < N_EXPERTS)
        def _():
            base = tcb_ref[tile]
            for j in range(MOE_BLOCKS):
                gather_copy(blk_ref[base + j], dst_slot, j).start()

    def wait_gather(tile, s):
        @pl.when(texp_ref[tile] < N_EXPERTS)
        def _():
            for j in range(MOE_BLOCKS):
                gather_copy(0, s, j).wait()

    def for_valid_blocks(tile, fn):
        n_valid = tnb_ref[tile]

        @pl.when(n_valid == MOE_BLOCKS)
        def _():
            for j in range(MOE_BLOCKS):
                fn(j)

        @pl.when(n_valid < MOE_BLOCKS)
        def _():
            for j in range(MOE_BLOCKS):
                pl.when(j < n_valid)(functools.partial(fn, j))

    def issue_scatter(tile, src_slot):
        base = tcb_ref[tile]
        for_valid_blocks(tile, lambda j: scatter_copy(blk_ref[base + j], src_slot, j).start())

    def wait_scatter(tile, s):
        for_valid_blocks(tile, lambda j: scatter_copy(0, s, j).wait())

    @pl.when(n_used > 0)
    def _():
        issue_gather(0, 0)

        @pl.when(texp_ref[0] < N_EXPERTS)
        def _():
            for cp in weight_copies(texp_ref[0], wslot_ref[0]):
                cp.start()

    def tile_step(i, carry):
        slot = lax.rem(i, 2)
        expert = texp_ref[i]
        wait_gather(i, slot)

        @pl.when(i + 1 < n_used)
        def _():
            issue_gather(i + 1, 1 - slot)

        @pl.when(i >= 2)
        def _():
            wait_scatter(i - 2, slot)

        @pl.when(jnp.logical_and(expert < N_EXPERTS, jnp.logical_or(i == 0, expert != texp_ref[jnp.maximum(i - 1, 0)])))
        def _():
            ws = wslot_ref[i]
            for cp in weight_copies(expert, ws):
                cp.wait()
            for static_ws in (0, 1):
                @pl.when(ws == static_ws)
                def _():
                    wgb[...] = wgf[static_ws].astype(BF16)
                    wub[...] = wuf[static_ws].astype(BF16)
                    wdb[...] = wdf[static_ws].astype(BF16)
            nxt = wnext_ref[i]

            @pl.when(nxt < N_EXPERTS)
            def _():
                for cp in weight_copies(nxt, 1 - ws):
                    cp.start()

        @pl.when(expert < N_EXPERTS)
        def _():
            x_lo, x_hi = _unpack_bf16_pairs(xbuf[slot])
            half = D_MODEL // 2
            g = (jnp.dot(x_lo, wgb[:half, :], preferred_element_type=F32)
                 + jnp.dot(x_hi, wgb[half:, :], preferred_element_type=F32))
            u = (jnp.dot(x_lo, wub[:half, :], preferred_element_type=F32)
                 + jnp.dot(x_hi, wub[half:, :], preferred_element_type=F32))
            act = (g * jax.nn.sigmoid(g) * u).astype(BF16)
            y = jnp.dot(act, wdb[...], preferred_element_type=F32)
            ybuf[slot] = _pack_bf16_pairs(y.astype(BF16).astype(F32))

        @pl.when(expert >= N_EXPERTS)
        def _():
            ybuf[slot] = jnp.zeros((MOE_TM, D_MODEL // 2), jnp.uint32)

        issue_scatter(i, slot)
        return carry

    lax.fori_loop(0, n_used, tile_step, 0)

    for back in (1, 2):
        @pl.when(n_used >= back)
        def _():
            last = jnp.maximum(n_used - back, 0)
            wait_scatter(last, lax.rem(last, 2))


def _expert_ffn(plan, xs_local, wg, wu, wd):
    n_blocks = xs_local.shape[0] // LOCAL_BLOCK
    half = D_MODEL // 2
    any_spec = pl.BlockSpec(memory_space=pl.ANY)
    grid_spec = pltpu.PrefetchScalarGridSpec(
        num_scalar_prefetch=len(plan),
        grid=(1,),
        in_specs=[any_spec, any_spec, any_spec, any_spec],
        out_specs=any_spec,
        scratch_shapes=[
            pltpu.VMEM((2, MOE_TM, half), jnp.uint32), pltpu.VMEM((2, MOE_TM, half), jnp.uint32),
            pltpu.VMEM((2, D_MODEL, D_EXPERT), F32), pltpu.VMEM((2, D_MODEL, D_EXPERT), F32),
            pltpu.VMEM((2, D_EXPERT, D_MODEL), F32),
            pltpu.VMEM((D_MODEL, D_EXPERT), BF16), pltpu.VMEM((D_MODEL, D_EXPERT), BF16),
            pltpu.VMEM((D_EXPERT, D_MODEL), BF16),
            pltpu.SemaphoreType.DMA((2,)), pltpu.SemaphoreType.DMA((2,)), pltpu.SemaphoreType.DMA((2,)),
        ],
    )
    return pl.pallas_call(
        _expert_kernel,
        grid_spec=grid_spec,
        out_shape=jax.ShapeDtypeStruct((n_blocks, LOCAL_BLOCK, half), jnp.uint32),
        compiler_params=pltpu.CompilerParams(
            dimension_semantics=("arbitrary",), vmem_limit_bytes=VMEM_LIMIT),
        name="expert_ffn",
    )(*plan, xs_local.reshape(n_blocks, LOCAL_BLOCK, half), wg, wu, wd)


def _combine_kernel(h_ref, wcol_ref, y_ref, o_ref):
    y_lo, y_hi = _unpack_bf16_pairs(y_ref[...])
    w = wcol_ref[...]
    jiota = lax.broadcasted_iota(jnp.int32, (MERGE_TM, LOCAL_ROWS), 1).astype(F32)
    pick = (jnp.where(jiota == w[:, 2:3], w[:, 0:1], 0.0) + jnp.where(jiota == w[:, 3:4], w[:, 1:2], 0.0)).astype(BF16)
    half = D_MODEL // 2
    for cols, y in ((slice(0, half), y_lo), (slice(half, D_MODEL), y_hi)):
        o_ref[:, cols] = h_ref[:, cols] + jnp.dot(pick, y, preferred_element_type=F32)


def _combine(h, wcol, y_local):
    t = h.shape[0]
    return pl.pallas_call(
        _combine_kernel,
        grid=(t // MERGE_TM,),
        in_specs=[
            pl.BlockSpec((MERGE_TM, D_MODEL), lambda i: (i, 0)),
            pl.BlockSpec((MERGE_TM, ROUTER_LANES), lambda i: (i, 0)),
            pl.BlockSpec((LOCAL_ROWS, D_MODEL // 2), lambda i: (i, 0)),
        ],
        out_specs=pl.BlockSpec((MERGE_TM, D_MODEL), lambda i: (i, 0)),
        out_shape=jax.ShapeDtypeStruct((t, D_MODEL), F32),
        compiler_params=pltpu.CompilerParams(
            dimension_semantics=("parallel",), vmem_limit_bytes=VMEM_LIMIT),
        name="combine",
    )(h, wcol, y_local)


def _dispatch_plan(block_expert):
    n_blocks = block_expert.size
    n_keys = N_EXPERTS + 1
    n_tiles = n_blocks // MOE_BLOCKS + n_keys
    keys = block_expert.reshape(n_blocks)
    order = jnp.argsort(keys, stable=True).astype(jnp.int32)
    counts = jnp.sum(keys[None, :] == jnp.arange(n_keys, dtype=jnp.int32)[:, None], axis=1).astype(jnp.int32)
    tiles_per = (counts + MOE_BLOCKS - 1) // MOE_BLOCKS
    tile_end = jnp.cumsum(tiles_per)
    tile_first = tile_end - tiles_per
    start = jnp.cumsum(counts) - counts
    tile = jnp.arange(n_tiles, dtype=jnp.int32)
    tile_expert = jnp.minimum(jnp.sum(tile_end[None, :] <= tile[:, None], axis=1), n_keys - 1).astype(jnp.int32)
    sel = (tile_expert[:, None] == jnp.arange(n_keys, dtype=jnp.int32)[None, :]).astype(jnp.int32)
    pick = lambda v: jnp.sum(sel * v[None, :], axis=1)
    within = (tile - pick(tile_first)) * MOE_BLOCKS
    tile_base = jnp.clip(pick(start) + within, 0, n_blocks).astype(jnp.int32)
    tile_blocks = jnp.clip(pick(counts) - within, 0, MOE_BLOCKS).astype(jnp.int32)
    n_used = tile_end[-1].astype(jnp.int32).reshape(1)
    order = jnp.concatenate([order, jnp.zeros((MOE_BLOCKS,), jnp.int32)])
    key = jnp.arange(n_keys, dtype=jnp.int32)
    present = jnp.logical_and(counts > 0, key < N_EXPERTS)
    slot_of = (jnp.cumsum(present) - present) % 2
    later = jnp.logical_and(key[None, :] > key[:, None], present[None, :])
    next_of = jnp.min(jnp.where(later, key[None, :], N_EXPERTS), axis=1)
    return (order, tile_expert, tile_base, tile_blocks, pick(slot_of).astype(jnp.int32),
            pick(next_of).astype(jnp.int32), n_used)


def _layer(x, norm1_g, w_in, q_norm_a, k_norm_a, sink_a, q_norm_b, k_norm_b, rpb_b, w_up_a, w_up_b, w_out,
           rel_table, norm2_g, w_rg, b_rg, w_re, b_re, w_gate, w_up, w_down):
    batch, seq, d = x.shape
    t = batch * seq
    x2 = x.reshape(t, d)

    w_cat = w_in.astype(BF16)
    scale = HEAD_DIM ** -0.5
    ones = lambda n: jnp.ones((n,), F32)
    gain_cat = jnp.concatenate([
        jnp.tile(q_norm_a.astype(F32) * scale, A_HEADS), jnp.tile(q_norm_b.astype(F32) * scale, B_HEADS),
        jnp.tile(k_norm_b.astype(F32), B_HEADS), ones(B_WIDTH + 2 * D_MODEL),
        jnp.tile(k_norm_a.astype(F32), A_KV_HEADS), ones(A_KV_WIDTH)]).reshape(1, IN_COLS)

    kind = lambda v, n: jnp.full((n,), v, F32)
    kind_cat = jnp.concatenate([kind(1.0, COL_VB), kind(0.0, B_WIDTH), kind(2.0, 2 * D_MODEL),
                                kind(1.0, A_KV_WIDTH), kind(0.0, A_KV_WIDTH)]).reshape(1, IN_COLS)

    proj = _in_projection(x2, norm1_g.astype(F32).reshape(1, d), w_cat, gain_cat, kind_cat)
    oa = _window_attention(proj, _window_bias(rel_table), sink_a.astype(F32), batch, seq)
    ob = _neighbourhood_attention(proj, _neighbourhood_col_bias(rpb_b), batch, seq)

    half = ROUTER_LANES // 2
    wr_t = jnp.concatenate([w_re.astype(F32).T, w_rg.astype(F32).T,
                            jnp.zeros((half - N_EXPERTS - N_GROUPS, d), F32)], axis=0)
    wr_hi = wr_t.astype(BF16)
    wr_lo = (wr_t - wr_hi.astype(F32)).astype(BF16)
    wr_cat = jnp.concatenate([wr_hi, wr_lo], axis=0).T
    br = jnp.pad(jnp.concatenate([b_re.astype(F32), b_rg.astype(F32)]), (0, ROUTER_LANES - N_EXPERTS - N_GROUPS))

    h, xs_local, block_expert, wcol = _merge_and_route(
        oa, ob, proj, x2, w_up_a.astype(BF16), w_up_b.astype(BF16), w_out.astype(BF16),
        norm2_g.astype(F32).reshape(1, d), wr_cat, br.reshape(1, ROUTER_LANES))

    plan = _dispatch_plan(block_expert[:, 0, :LOCAL_BLOCKS])
    y_local = _expert_ffn(plan, xs_local, w_gate.astype(F32), w_up.astype(F32), w_down.astype(F32))
    out = _combine(h, wcol, y_local.reshape(-1, d // 2))
    return out.reshape(batch, seq, d)


def kernel(x, norm1_g, w_in, q_norm_a, k_norm_a, sink_a, q_norm_b, k_norm_b, rpb_b, w_up_a, w_up_b, w_out,
           rel_bias_table, norm2_g, w_router_group, b_router_group, w_router_expert, b_router_expert,
           w_gate, w_up, w_down):
    return _layer(x, norm1_g[0], w_in[0], q_norm_a[0], k_norm_a[0], sink_a[0], q_norm_b[0], k_norm_b[0],
                  rpb_b[0], w_up_a[0], w_up_b[0], w_out[0], rel_bias_table, norm2_g[0],
                  w_router_group[0], b_router_group[0], w_router_expert[0], b_router_expert[0],
                  w_gate[0], w_up[0], w_down[0])
```
